```python
import jax, jax.numpy as jnp
from jax import lax
import numpy as np

D_MODEL = 4096
BATCH = 1
SEQ = 8192
DEPTH = 2

CTX_LEN = 256
GRID_W = 64
N_BRANCH = 4
BRANCH_W = D_MODEL // N_BRANCH
HEAD_DIM = 128
N_HEADS = BRANCH_W // HEAD_DIM
CHUNK = 128
CONV_K = 31
WIN_H = 8
WIN_W = 16
ROPE_THETA = 10000.0
GLA_CHUNK = 128
D_FF = -(-8 * D_MODEL // (3 * 256)) * 256
IN_COLS = 12 * BRANCH_W
EPS = 1e-6
F_MIN = 1e-6
NEG_BIG = -1e30
F32 = jnp.float32

kernel_name = 'hybrid_parallel_gated_diffusion_block'


def rmsnorm(x, g):
    xf = x.astype(F32)
    y = xf * lax.rsqrt(jnp.mean(xf * xf, axis=-1, keepdims=True) + EPS)
    return (y * g.astype(F32)).astype(x.dtype)


def layernorm(x, g, b):
    xf = x.astype(F32)
    mu = jnp.mean(xf, axis=-1, keepdims=True)
    var = jnp.mean(jnp.square(xf - mu), axis=-1, keepdims=True)
    y = (xf - mu) * lax.rsqrt(var + EPS)
    return (y * g.astype(F32) + b.astype(F32)).astype(x.dtype)


def adaln_params(cond, w, b):
    mod = jnp.dot(jax.nn.silu(cond), w) + b
    return jnp.split(mod, 6, axis=-1)


def split_heads(t):
    return t.reshape(t.shape[0], t.shape[1], -1, HEAD_DIM)


def heads_first(t):
    return split_heads(t).transpose(0, 2, 1, 3)


def chunk_token_mlp(pa, ng, nb, ws, bs):
    z = jax.nn.gelu(pa, approximate=False)
    u, v = jnp.split(z, 2, axis=-1)
    v = layernorm(v, ng, nb)
    B, T, _ = v.shape
    vc = v.reshape(B, T // CHUNK, CHUNK, N_HEADS, BRANCH_W // N_HEADS)
    s = jnp.einsum('gts,bnsgc->bntgc', ws, vc) + bs.T[None, None, :, :, None]
    return u * s.reshape(B, T, BRANCH_W)


def conformer_conv(pb, cw, cb, ng, nb):
    a, g = jnp.split(pb, 2, axis=-1)
    y = a * jax.nn.sigmoid(g)
    y = lax.conv_general_dilated(
        y, cw[:, None, :].astype(y.dtype), window_strides=(1,),
        padding=[(CONV_K // 2, CONV_K // 2)],
        dimension_numbers=('NWC', 'WIO', 'NWC'),
        feature_group_count=BRANCH_W) + cb
    return jax.nn.silu(layernorm(y, ng, nb))


def rope_angles(n, dim_half):
    m = dim_half // 2
    freqs = ROPE_THETA ** (-jnp.arange(m, dtype=F32) / m)
    return jnp.arange(n, dtype=F32)[:, None] * freqs[None, :]


def rotate(x, ang):
    x1, x2 = jnp.split(x, 2, axis=-1)
    cos, sin = jnp.cos(ang), jnp.sin(ang)
    return jnp.concatenate([x1 * cos - x2 * sin, x2 * cos + x1 * sin], axis=-1)


def axial_rope_grid(t):
    R, W = t.shape[1], t.shape[2]
    half = HEAD_DIM // 2
    ang_r = rope_angles(R, half)[:, None, None, :]
    ang_c = rope_angles(W, half)[None, :, None, :]
    tf = t.astype(F32)
    out = jnp.concatenate([rotate(tf[..., :half], ang_r), rotate(tf[..., half:], ang_c)], axis=-1)
    return out.astype(t.dtype)


def neighborhood_attention(q, k, v, k_ctx, v_ctx, rpb):
    B, T, H, dh = q.shape
    R = T // GRID_W
    kh = min(WIN_H, R)
    scale = dh ** -0.5
    qg = q.reshape(B, R, GRID_W, H, dh)
    kg = k.reshape(B, R, GRID_W, H, dh)
    vg = v.reshape(B, R, GRID_W, H, dh)
    q_rot = axial_rope_grid(qg)
    k_rot = axial_rope_grid(kg)
    rows = jnp.arange(R)
    cols = jnp.arange(GRID_W)
    key_rows = jnp.clip(rows - kh // 2, 0, R - kh)[:, None] + jnp.arange(kh)[None, :]
    k_band = k_rot[:, key_rows]
    v_band = vg[:, key_rows]
    s_lat = jnp.einsum('brqhd,brjkhd->bhrqjk', q_rot, k_band).astype(F32) * scale
    dr = key_rows - rows[:, None] + (WIN_H - 1)
    dc = jnp.clip(cols[None, :] - cols[:, None], -(WIN_W - 1), WIN_W - 1) + (WIN_W - 1)
    bias = rpb[:, dr[:, None, :, None], dc[None, :, None, :]].astype(F32)
    col_start = jnp.clip(cols - WIN_W // 2, 0, GRID_W - WIN_W)
    in_win = (cols[None, :] >= col_start[:, None]) & (cols[None, :] < col_start[:, None] + WIN_W)
    s_lat = jnp.where(in_win[:, None, :], s_lat + bias, NEG_BIG)
    s_ctx = jnp.einsum('brqhd,bchd->bhrqc', qg, k_ctx).astype(F32) * scale
    n_lat = kh * GRID_W
    s = jnp.concatenate([s_lat.reshape(B, H, R, GRID_W, n_lat), s_ctx], axis=-1)
    p = jax.nn.softmax(s, axis=-1).astype(v.dtype)
    p_lat = p[..., :n_lat].reshape(B, H, R, GRID_W, kh, GRID_W)
    p_ctx = p[..., n_lat:]
    o = jnp.einsum('bhrqjk,brjkhd->brqhd', p_lat, v_band) + jnp.einsum('bhrqc,bchd->brqhd', p_ctx, v_ctx)
    return o.reshape(B, T, H * dh)


def context_attention(q, k, v):
    B, L, H, dh = q.shape
    s = jnp.einsum('bqhd,bkhd->bhqk', q, k).astype(F32) * dh ** -0.5
    p = jax.nn.softmax(s, axis=-1).astype(v.dtype)
    return jnp.einsum('bhqk,bkhd->bqhd', p, v).reshape(B, L, H * dh)


def hgrn_lower_bounds(logits):
    p = jax.nn.softmax(logits.astype(F32), axis=1)
    return jnp.cumsum(p, axis=1) - p[:, :1]


def forget_gate(z, lb):
    f = lb + (1.0 - lb) * jax.nn.sigmoid(z.astype(F32))
    return jnp.log(jnp.maximum(f, F_MIN)), 1.0 - f


def gated_linear_scan(q, k, v, logf, s0):
    B, H, T, _ = q.shape
    n = T // GLA_CHUNK

    def to_chunks(t):
        return t.reshape(B, H, n, GLA_CHUNK, t.shape[-1]).transpose(2, 0, 1, 3, 4)

    lower = jnp.tril(jnp.ones((GLA_CHUNK, GLA_CHUNK), dtype=bool))[:, :, None]

    def step(S, inp):
        qc, kc, vc, lc = inp
        b = jnp.cumsum(lc, axis=-2)
        b_end = b[:, :, -1:, :]
        inter = jnp.einsum('bhtd,bhde->bhte', qc * jnp.exp(b), S)
        rel = b[:, :, :, None, :] - b[:, :, None, :, :]
        decay = jnp.where(lower, jnp.exp(jnp.where(lower, rel, 0.0)), 0.0)
        att = jnp.einsum('bhtd,bhsd,bhtsd->bhts', qc, kc, decay)
        o = inter + jnp.einsum('bhts,bhse->bhte', att, vc)
        S = jnp.exp(b_end[:, :, 0, :])[..., None] * S + jnp.einsum('bhsd,bhse->bhde', kc * jnp.exp(b_end - b), vc)
        return S, o

    S, o = lax.scan(step, s0, (to_chunks(q), to_chunks(k), to_chunks(v), to_chunks(logf)))
    return o.transpose(1, 2, 0, 3, 4).reshape(B, H, T, -1), S


def hgrn2_prepare(pd, lb_f, lb_b):
    qd, zf, zb, i, g = jnp.split(pd, 5, axis=-1)
    lf, kf = forget_gate(zf, lb_f)
    lbw, kb = forget_gate(zb, lb_b)
    q = jax.nn.silu(qd.astype(F32))
    parts = tuple(heads_first(t) for t in (q, i.astype(F32), lf, kf, lbw, kb))
    return parts, g


def hgrn2_bidir(parts, s0f, s0b):
    q, v, lf, kf, lbw, kb = parts
    o_f, s_f = gated_linear_scan(q, kf, v, lf, s0f)
    rev = lambda t: t[:, :, ::-1]
    o_b, s_b = gated_linear_scan(rev(q), rev(kb), rev(v), rev(lbw), s0b)
    return o_f + rev(o_b), s_f, s_b


def hgrn2_readout(o, g, norm_g):
    B, H, T, dv = o.shape
    o = o * lax.rsqrt(jnp.mean(o * o, axis=-1, keepdims=True) + EPS) * norm_g.astype(F32)
    o = o.transpose(0, 2, 1, 3).reshape(B, T, H * dv).astype(g.dtype)
    return o * jax.nn.silu(g)


def gated_merge(h, branches, w_branch, w_gate, b_gate, w_out):
    gates = jax.nn.sigmoid(jnp.dot(h, w_gate) + b_gate)
    merged = None
    for i, y in enumerate(branches):
        term = gates[..., i * D_MODEL:(i + 1) * D_MODEL] * jnp.dot(y, w_branch[i])
        merged = term if merged is None else merged + term
    return jnp.dot(merged, w_out)


def swiglu(h, w1, w3, w2):
    return jnp.dot(jax.nn.silu(jnp.dot(h, w1)) * jnp.dot(h, w3), w2)


def trunk_layer(x, xc, c, c_ctx, last, w_ada, b_ada, n1, n2, w_in, a_ng, a_nb, a_ws, a_bs,
                b_cw, b_cb, b_ng, b_nb, rpb, lb_f, lb_b, d_ng, w_branch, w_gate, b_gate, w_out,
                w1, w3, w2):
    sh1, sc1, gt1, sh2, sc2, gt2 = [m[:, None, :] for m in adaln_params(c, w_ada, b_ada)]
    csh1, csc1, cgt1, csh2, csc2, cgt2 = adaln_params(c_ctx, w_ada, b_ada)
    h = rmsnorm(x, n1) * (1 + sc1) + sh1
    hc = rmsnorm(xc, n1) * (1 + csc1) + csh1
    cuts = [2 * BRANCH_W, 4 * BRANCH_W, 7 * BRANCH_W]
    pa, pb, pq, pd = jnp.split(jnp.dot(h, w_in), cuts, axis=-1)
    ca, cb, cq, cd = jnp.split(jnp.dot(hc, w_in), cuts, axis=-1)

    q_c, k_c, v_c = [split_heads(t) for t in jnp.split(cq, 3, axis=-1)]
    cparts, cg = hgrn2_prepare(cd, lb_f, lb_b)
    zeros = jnp.zeros((xc.shape[0], N_HEADS, HEAD_DIM, HEAD_DIM), F32)
    co_d, s_f, s_b = hgrn2_bidir(cparts, zeros, zeros)

    q, k, v = [split_heads(t) for t in jnp.split(pq, 3, axis=-1)]
    parts, g = hgrn2_prepare(pd, lb_f, lb_b)
    o_d, _, _ = hgrn2_bidir(parts, s_f, s_b)
    ys = [chunk_token_mlp(pa, a_ng, a_nb, a_ws, a_bs),
          conformer_conv(pb, b_cw, b_cb, b_ng, b_nb),
          neighborhood_attention(q, k, v, k_c, v_c, rpb),
          hgrn2_readout(o_d, g, d_ng)]
    x = x + gt1 * gated_merge(h, ys, w_branch, w_gate, b_gate, w_out)
    hf = rmsnorm(x, n2) * (1 + sc2) + sh2
    x = x + gt2 * swiglu(hf, w1, w3, w2)

    if not last:
        ycs = [chunk_token_mlp(ca, a_ng, a_nb, a_ws, a_bs),
               conformer_conv(cb, b_cw, b_cb, b_ng, b_nb),
               context_attention(q_c, k_c, v_c),
               hgrn2_readout(co_d, cg, d_ng)]
        xc = xc + cgt1 * gated_merge(hc, ycs, w_branch, w_gate, b_gate, w_out)
        hcf = rmsnorm(xc, n2) * (1 + csc2) + csh2
        xc = xc + cgt2 * swiglu(hcf, w1, w3, w2)
    return x, xc


def setup_inputs(seed: int = 0) -> dict:
    key = jax.random.key(seed)
    ks = jax.random.split(key, 28)

    def nrm(k, shape, scale):
        return jax.random.normal(k, shape, F32) * scale

    D, W = D_MODEL, BRANCH_W
    return {
        'x': nrm(ks[0], (BATCH, SEQ, D), 1.0),
        'c': nrm(ks[1], (BATCH, D), 1.0),
        'ctx': nrm(ks[2], (BATCH, CTX_LEN, D), 1.0),
        'c_ctx': nrm(ks[3], (D,), 1.0),
        'w_ada': nrm(ks[4], (DEPTH, D, 6 * D), 0.5 * D ** -0.5),
        'b_ada': nrm(ks[5], (DEPTH, 6 * D), 0.02),
        'norm1_g': 1.0 + nrm(ks[6], (DEPTH, D), 0.02),
        'norm2_g': 1.0 + nrm(ks[7], (DEPTH, D), 0.02),
        'w_in': nrm(ks[8], (DEPTH, D, IN_COLS), D ** -0.5),
        'a_norm_g': 1.0 + nrm(ks[9], (DEPTH, W), 0.02),
        'a_norm_b': nrm(ks[10], (DEPTH, W), 0.02),
        'a_ws': nrm(ks[11], (DEPTH, N_HEADS, CHUNK, CHUNK), CHUNK ** -0.5),
        'a_bs': 1.0 + nrm(ks[12], (DEPTH, N_HEADS, CHUNK), 0.02),
        'b_conv_w': nrm(ks[13], (DEPTH, CONV_K, W), CONV_K ** -0.5),
        'b_conv_b': nrm(ks[14], (DEPTH, W), 0.02),
        'b_norm_g': 1.0 + nrm(ks[15], (DEPTH, W), 0.02),
        'b_norm_b': nrm(ks[16], (DEPTH, W), 0.02),
        'c_rpb': nrm(ks[17], (DEPTH, N_HEADS, 2 * WIN_H - 1, 2 * WIN_W - 1), 0.1),
        'd_lb_logits': nrm(ks[18], (2, DEPTH, W), 0.5),
        'd_norm_g': 1.0 + nrm(ks[19], (DEPTH, HEAD_DIM), 0.02),
        'w_branch': nrm(ks[20], (DEPTH, N_BRANCH, W, D), W ** -0.5),
        'w_gate': nrm(ks[21], (DEPTH, D, N_BRANCH * D), D ** -0.5),
        'b_gate': nrm(ks[22], (DEPTH, N_BRANCH * D), 0.02),
        'w_out': nrm(ks[23], (DEPTH, D, D), D ** -0.5),
        'w_ffn1': nrm(ks[24], (DEPTH, D, D_FF), D ** -0.5),
        'w_ffn3': nrm(ks[25], (DEPTH, D, D_FF), D ** -0.5),
        'w_ffn2': nrm(ks[26], (DEPTH, D_FF, D), D_FF ** -0.5),
        'final_g': 1.0 + nrm(ks[27], (D,), 0.02),
    }


def reference(x, c, ctx, c_ctx, w_ada, b_ada, norm1_g, norm2_g, w_in, a_norm_g, a_norm_b, a_ws,
              a_bs, b_conv_w, b_conv_b, b_norm_g, b_norm_b, c_rpb, d_lb_logits, d_norm_g, w_branch,
              w_gate, b_gate, w_out, w_ffn1, w_ffn3, w_ffn2, final_g):
    lb = hgrn_lower_bounds(d_lb_logits)
    xc = ctx
    for l in range(DEPTH):
        x, xc = trunk_layer(
            x, xc, c, c_ctx, l == DEPTH - 1, w_ada[l], b_ada[l], norm1_g[l], norm2_g[l], w_in[l],
            a_norm_g[l], a_norm_b[l], a_ws[l], a_bs[l], b_conv_w[l], b_conv_b[l], b_norm_g[l],
            b_norm_b[l], c_rpb[l], lb[0, l], lb[1, l], d_norm_g[l], w_branch[l], w_gate[l],
            b_gate[l], w_out[l], w_ffn1[l], w_ffn3[l], w_ffn2[l])
    return rmsnorm(x, final_g)
```

```python
import functools
import math

import jax
import jax.numpy as jnp
from jax import lax
from jax.experimental import pallas as pl
from jax.experimental.pallas import tpu as pltpu

F32 = jnp.float32
BF16 = jnp.bfloat16

HEAD_DIM = 128
CHUNK = 128
GRID_W = 64
WIN_H = 8
WIN_W = 16
CONV_K = 31
N_BRANCH = 4
ROPE_THETA = 10000.0
EPS = 1e-6
F_MIN = 1e-6
NEG_BIG = -1e30

V7X_VMEM_BYTES = 64 * 2**20
VMEM_CAP = V7X_VMEM_BYTES - 6 * 2**20
SUBLANES = 8
LANES = 128
CONV_HALO = 16


def _pick(n, target, quantum):
    best = None
    for t in range(quantum, min(n, target) + 1, quantum):
        if n % t == 0:
            best = t
    assert best is not None, (n, target, quantum)
    return best


def _params(sem, need_bytes):
    limit = int(min(VMEM_CAP, max(need_bytes * 5 // 4, 16 * 2**20)))
    return pltpu.CompilerParams(dimension_semantics=sem, vmem_limit_bytes=limit)


def _sigmoid(x):
    return jax.nn.sigmoid(x)


def _silu(x):
    return x * jax.nn.sigmoid(x)


def _gelu(x):
    return 0.5 * x * (1.0 + lax.erf(x * 0.7071067811865476))


def _row_select(rows_is_ctx, p_ref):
    return jnp.where(rows_is_ctx, p_ref[1:2, :], p_ref[0:1, :])


def _mod_kernel(cond_ref, w_ref, b_ref, o_ref):
    s = _silu(cond_ref[...]).astype(BF16)
    o_ref[...] = jnp.dot(s, w_ref[...].astype(BF16), preferred_element_type=F32) + b_ref[...]


def _modulation(cond, w_ada, b_ada):
    depth, d, n = w_ada.shape
    tn = _pick(n, 512, LANES)
    need = 2 * d * tn * 4 + d * tn * 2
    return pl.pallas_call(
        _mod_kernel,
        grid=(depth, n // tn),
        in_specs=[
            pl.BlockSpec((SUBLANES, d), lambda l, j: (0, 0)),
            pl.BlockSpec((None, d, tn), lambda l, j: (l, 0, j)),
            pl.BlockSpec((None, 1, tn), lambda l, j: (l, 0, j)),
        ],
        out_specs=pl.BlockSpec((None, SUBLANES, tn), lambda l, j: (l, 0, j)),
        out_shape=jax.ShapeDtypeStruct((depth, SUBLANES, n), F32),
        compiler_params=_params(("arbitrary", "arbitrary"), need),
        name="adaln_modulation",
    )(cond, w_ada, b_ada.reshape(depth, 1, n))


def _norm_mod_kernel(x_ref, g_ref, sh_ref, sc_ref, o_ref, *, t_lat, tm):
    x = x_ref[...]
    y = x * lax.rsqrt(jnp.mean(x * x, axis=-1, keepdims=True) + EPS) * g_ref[...]
    rows = pl.program_id(0) * tm + lax.broadcasted_iota(jnp.int32, (tm, 1), 0)
    is_ctx = rows >= t_lat
    o_ref[...] = (y * (1.0 + _row_select(is_ctx, sc_ref)) + _row_select(is_ctx, sh_ref)).astype(o_ref.dtype)


def _norm_mod(x, g, mod, shift_blk, scale_blk, t_lat):
    m, d = x.shape
    tm = _pick(m, 256, SUBLANES)
    return pl.pallas_call(
        functools.partial(_norm_mod_kernel, t_lat=t_lat, tm=tm),
        grid=(m // tm,),
        in_specs=[
            pl.BlockSpec((tm, d), lambda i: (i, 0)),
            pl.BlockSpec((1, d), lambda i: (0, 0)),
            pl.BlockSpec((SUBLANES, d), lambda i: (0, shift_blk)),
            pl.BlockSpec((SUBLANES, d), lambda i: (0, scale_blk)),
        ],
        out_specs=pl.BlockSpec((tm, d), lambda i: (i, 0)),
        out_shape=jax.ShapeDtypeStruct((m, d), BF16),
        compiler_params=_params(("arbitrary",), 2 * tm * d * 6 + 4 * tm * d * 4),
        name="rmsnorm_modulate",
    )(x, g.reshape(1, d), mod, mod)


def _final_norm_kernel(x_ref, g_ref, o_ref):
    x = x_ref[...]
    o_ref[...] = x * lax.rsqrt(jnp.mean(x * x, axis=-1, keepdims=True) + EPS) * g_ref[...]


def _final_norm(x, g):
    m, d = x.shape
    tm = _pick(m, 256, SUBLANES)
    return pl.pallas_call(
        _final_norm_kernel,
        grid=(m // tm,),
        in_specs=[pl.BlockSpec((tm, d), lambda i: (i, 0)), pl.BlockSpec((1, d), lambda i: (0, 0))],
        out_specs=pl.BlockSpec((tm, d), lambda i: (i, 0)),
        out_shape=jax.ShapeDtypeStruct((m, d), F32),
        compiler_params=_params(("arbitrary",), 2 * tm * d * 8 + 2 * tm * d * 4),
        name="final_rmsnorm",
    )(x, g.reshape(1, d))


def _mm_kernel(a_ref, w_ref, o_ref):
    o_ref[...] = jnp.dot(a_ref[...], w_ref[...], preferred_element_type=F32).astype(o_ref.dtype)


def _matmul(a, w, out_dtype):
    m, k = a.shape
    n = w.shape[1]
    tm = _pick(m, 1024, 256)
    tn = _pick(n, 1024, LANES)
    osz = jnp.dtype(out_dtype).itemsize
    need = 2 * (tm * k * 2 + k * tn * 2 + tm * tn * osz) + tm * tn * 4
    return pl.pallas_call(
        _mm_kernel,
        grid=(m // tm, n // tn),
        in_specs=[pl.BlockSpec((tm, k), lambda i, j: (i, 0)), pl.BlockSpec((k, tn), lambda i, j: (0, j))],
        out_specs=pl.BlockSpec((tm, tn), lambda i, j: (i, j)),
        out_shape=jax.ShapeDtypeStruct((m, n), out_dtype),
        compiler_params=_params(("arbitrary", "arbitrary"), need),
        name="input_projection",
    )(a, w)


def _mm_resid_kernel(a_ref, w_ref, x_ref, gate_ref, o_ref, *, t_lat, tm):
    rows = pl.program_id(0) * tm + lax.broadcasted_iota(jnp.int32, (tm, 1), 0)
    gate = _row_select(rows >= t_lat, gate_ref)
    o_ref[...] = x_ref[...] + gate * jnp.dot(a_ref[...], w_ref[...], preferred_element_type=F32)


def _matmul_residual(a, w, x, mod, gate_blk, t_lat, m, tm_target, tn_target, name):
    k = a.shape[1]
    n = w.shape[1]
    tm = _pick(m, tm_target, 256)
    tn = _pick(n, tn_target, LANES)
    need = 2 * (tm * k * 2 + k * tn * 2 + 2 * tm * tn * 4) + tm * tn * 4
    nb = n // tn
    return pl.pallas_call(
        functools.partial(_mm_resid_kernel, t_lat=t_lat, tm=tm),
        grid=(m // tm, nb),
        in_specs=[
            pl.BlockSpec((tm, k), lambda i, j: (i, 0)),
            pl.BlockSpec((k, tn), lambda i, j: (0, j)),
            pl.BlockSpec((tm, tn), lambda i, j: (i, j)),
            pl.BlockSpec((SUBLANES, tn), lambda i, j: (0, gate_blk * nb + j)),
        ],
        out_specs=pl.BlockSpec((tm, tn), lambda i, j: (i, j)),
        out_shape=jax.ShapeDtypeStruct((m, n), F32),
        compiler_params=_params(("arbitrary", "arbitrary"), need),
        name=name,
    )(a, w, x, mod)


def _swiglu_kernel(a_ref, w1_ref, w3_ref, o_ref):
    a = a_ref[...]
    u = jnp.dot(a, w1_ref[...], preferred_element_type=F32)
    g = jnp.dot(a, w3_ref[...], preferred_element_type=F32)
    o_ref[...] = (_silu(u) * g).astype(o_ref.dtype)


def _swiglu_up(a, w1, w3, m):
    k = a.shape[1]
    n = w1.shape[1]
    tm = _pick(m, 1024, 256)
    tn = _pick(n, 512, LANES)
    need = 2 * (tm * k * 2 + 2 * k * tn * 2 + tm * tn * 2) + 3 * tm * tn * 4
    return pl.pallas_call(
        _swiglu_kernel,
        grid=(m // tm, n // tn),
        in_specs=[
            pl.BlockSpec((tm, k), lambda i, j: (i, 0)),
            pl.BlockSpec((k, tn), lambda i, j: (0, j)),
            pl.BlockSpec((k, tn), lambda i, j: (0, j)),
        ],
        out_specs=pl.BlockSpec((tm, tn), lambda i, j: (i, j)),
        out_shape=jax.ShapeDtypeStruct((m, n), BF16),
        compiler_params=_params(("arbitrary", "arbitrary"), need),
        name="swiglu_up",
    )(a, w1, w3)


def _merge_kernel(h_ref, ya_ref, yb_ref, yc_ref, yd_ref, wg0, wg1, wg2, wg3, bg0, bg1, bg2, bg3, wb_ref, o_ref):
    h = h_ref[...]
    acc = None
    for i, (y_ref, wg, bg) in enumerate(((ya_ref, wg0, bg0), (yb_ref, wg1, bg1), (yc_ref, wg2, bg2), (yd_ref, wg3, bg3))):
        gate = _sigmoid(jnp.dot(h, wg[...], preferred_element_type=F32) + bg[...])
        term = gate * jnp.dot(y_ref[...], wb_ref[i], preferred_element_type=F32)
        acc = term if acc is None else acc + term
    o_ref[...] = acc.astype(o_ref.dtype)


def _gated_merge(h, ys, w_gate, b_gate, w_branch, m):
    k = h.shape[1]
    nbr, bw, n = w_branch.shape
    tm = _pick(m, 768, 256)
    tn = _pick(n, 256, LANES)
    nb = n // tn
    need = 2 * (tm * k * 2 + nbr * tm * bw * 2 + nbr * k * tn * 2 + nbr * bw * tn * 2 + tm * tn * 2) + 4 * tm * tn * 4
    y_spec = pl.BlockSpec((tm, bw), lambda i, j: (i, 0))
    wg_specs = [pl.BlockSpec((k, tn), functools.partial(lambda i, j, b: (0, b * nb + j), b=b)) for b in range(nbr)]
    bg_specs = [pl.BlockSpec((1, tn), functools.partial(lambda i, j, b: (0, b * nb + j), b=b)) for b in range(nbr)]
    bg = b_gate.reshape(1, -1)
    return pl.pallas_call(
        _merge_kernel,
        grid=(m // tm, nb),
        in_specs=[pl.BlockSpec((tm, k), lambda i, j: (i, 0)), y_spec, y_spec, y_spec, y_spec]
        + wg_specs + bg_specs + [pl.BlockSpec((nbr, bw, tn), lambda i, j: (0, 0, j))],
        out_specs=pl.BlockSpec((tm, tn), lambda i, j: (i, j)),
        out_shape=jax.ShapeDtypeStruct((m, n), BF16),
        compiler_params=_params(("arbitrary", "arbitrary"), need),
        name="gated_merge",
    )(h, *ys, w_gate, w_gate, w_gate, w_gate, bg, bg, bg, bg, w_branch)


def _mix_a_kernel(pu_ref, pv_ref, ws_ref, bias_ref, ng_ref, nb_ref, o_ref, *, nh):
    zu = _gelu(pu_ref[...])
    zv = _gelu(pv_ref[...])
    mu = jnp.mean(zv, axis=-1, keepdims=True)
    zc = zv - mu
    var = jnp.mean(zc * zc, axis=-1, keepdims=True)
    v = (zc * lax.rsqrt(var + EPS) * ng_ref[...] + nb_ref[...]).astype(BF16)
    for g in range(nh):
        sl = slice(g * HEAD_DIM, (g + 1) * HEAD_DIM)
        s = jnp.dot(ws_ref[g], v[:, sl], preferred_element_type=F32) + bias_ref[:, sl]
        o_ref[:, sl] = (zu[:, sl] * s).astype(o_ref.dtype)


def _mix_a(p, ws, bs, ng, nb, bw):
    m = p.shape[0]
    nh = bw // HEAD_DIM
    bias = jnp.repeat(bs.T, HEAD_DIM, axis=1)
    return pl.pallas_call(
        functools.partial(_mix_a_kernel, nh=nh),
        grid=(m // CHUNK,),
        in_specs=[
            pl.BlockSpec((CHUNK, bw), lambda i: (i, 0)),
            pl.BlockSpec((CHUNK, bw), lambda i: (i, 1)),
            pl.BlockSpec((nh, CHUNK, CHUNK), lambda i: (0, 0, 0)),
            pl.BlockSpec((CHUNK, bw), lambda i: (0, 0)),
            pl.BlockSpec((1, bw), lambda i: (0, 0)),
            pl.BlockSpec((1, bw), lambda i: (0, 0)),
        ],
        out_specs=pl.BlockSpec((CHUNK, bw), lambda i: (i, 0)),
        out_shape=jax.ShapeDtypeStruct((m, bw), BF16),
        compiler_params=_params(("arbitrary",), 16 * CHUNK * bw * 4),
        name="mixer_gmlp",
    )(p, p, ws.astype(BF16), bias, ng.reshape(1, bw), nb.reshape(1, bw))


CONV_ROWS = 16
CONV_WIN = CONV_ROWS + 2 * CONV_HALO


def _mix_b_kernel(ap_ref, ac_ref, an_ref, gp_ref, gc_ref, gn_ref, cw_ref, cb_ref, ng_ref, nb_ref, o_ref, ybuf,
                  *, tiles_lat, tiles_tot, tmb):
    i = pl.program_id(0)
    first = jnp.logical_or(i == 0, i == tiles_lat)
    last = jnp.logical_or(i == tiles_lat - 1, i == tiles_tot - 1)
    yp = ap_ref[...] * _sigmoid(gp_ref[...])
    yn = an_ref[...] * _sigmoid(gn_ref[...])
    ybuf[0:CONV_HALO, :] = jnp.where(first, 0.0, yp)
    ybuf[CONV_HALO:CONV_HALO + tmb, :] = ac_ref[...] * _sigmoid(gc_ref[...])
    ybuf[CONV_HALO + tmb:, :] = jnp.where(last, 0.0, yn)
    off = CONV_HALO - CONV_K // 2

    def row_block(rb, carry):
        r0 = pl.multiple_of(rb * CONV_ROWS, CONV_ROWS)
        win = ybuf[pl.ds(r0, CONV_WIN), :]
        acc = None
        for b in range(SUBLANES):
            shifted = win if b == 0 else pltpu.roll(win, CONV_WIN - b, 0)
            for j in range(CONV_K):
                if (off + j) % SUBLANES != b:
                    continue
                a = (off + j) - b
                term = cw_ref[j:j + 1, :] * shifted[a:a + CONV_ROWS, :]
                acc = term if acc is None else acc + term
        y = acc + cb_ref[...]
        mu = jnp.mean(y, axis=-1, keepdims=True)
        yc = y - mu
        var = jnp.mean(yc * yc, axis=-1, keepdims=True)
        z = yc * lax.rsqrt(var + EPS) * ng_ref[...] + nb_ref[...]
        o_ref[pl.ds(r0, CONV_ROWS), :] = _silu(z).astype(o_ref.dtype)
        return carry

    lax.fori_loop(0, tmb // CONV_ROWS, row_block, 0)


def _mix_b(p, cw, cb, ng, nb, bw, t_lat):
    m = p.shape[0]
    tmb = _pick(math.gcd(t_lat, m - t_lat), 256, CONV_HALO)
    hb = tmb // CONV_HALO
    nhalo = m // CONV_HALO
    a_col, g_col = 2, 3
    prev = lambda c: (lambda i: (jnp.maximum(i * hb - 1, 0), c))
    cur = lambda c: (lambda i: (i, c))
    nxt = lambda c: (lambda i: (jnp.minimum((i + 1) * hb, nhalo - 1), c))
    vec = pl.BlockSpec((1, bw), lambda i: (0, 0))
    return pl.pallas_call(
        functools.partial(_mix_b_kernel, tiles_lat=t_lat // tmb, tiles_tot=m // tmb, tmb=tmb),
        grid=(m // tmb,),
        in_specs=[
            pl.BlockSpec((CONV_HALO, bw), prev(a_col)),
            pl.BlockSpec((tmb, bw), cur(a_col)),
            pl.BlockSpec((CONV_HALO, bw), nxt(a_col)),
            pl.BlockSpec((CONV_HALO, bw), prev(g_col)),
            pl.BlockSpec((tmb, bw), cur(g_col)),
            pl.BlockSpec((CONV_HALO, bw), nxt(g_col)),
            pl.BlockSpec((CONV_K, bw), lambda i: (0, 0)),
            vec, vec, vec,
        ],
        out_specs=pl.BlockSpec((tmb, bw), lambda i: (i, 0)),
        out_shape=jax.ShapeDtypeStruct((m, bw), BF16),
        scratch_shapes=[pltpu.VMEM((tmb + 2 * CONV_HALO, bw), F32)],
        compiler_params=_params(("arbitrary",), 8 * tmb * bw * 4),
        name="mixer_conformer_conv",
    )(p, p, p, p, p, p, cw, cb.reshape(1, bw), ng.reshape(1, bw), nb.reshape(1, bw))


def _rope_tables(t_lat, n_ctx):
    m = HEAD_DIM // 4
    freqs = ROPE_THETA ** (-jnp.arange(m, dtype=F32) / m)
    r = t_lat // GRID_W
    ang_r = jnp.repeat(jnp.arange(r, dtype=F32)[:, None] * freqs[None, :], GRID_W, axis=0)
    ang_c = jnp.tile(jnp.arange(GRID_W, dtype=F32)[:, None] * freqs[None, :], (r, 1))
    cos = jnp.concatenate([jnp.cos(ang_r), jnp.cos(ang_r), jnp.cos(ang_c), jnp.cos(ang_c)], axis=-1)
    sin = jnp.concatenate([-jnp.sin(ang_r), jnp.sin(ang_r), -jnp.sin(ang_c), jnp.sin(ang_c)], axis=-1)
    cos = jnp.concatenate([cos, jnp.ones((n_ctx, HEAD_DIM), F32)], axis=0)
    sin = jnp.concatenate([sin, jnp.zeros((n_ctx, HEAD_DIM), F32)], axis=0)
    return cos, sin


def _bias_table(rpb):
    v = jnp.arange(WIN_H)[:, None]
    j = jnp.arange(WIN_H)[None, :]
    dr = j - v + (WIN_H - 1)
    qc = jnp.arange(GRID_W)[:, None]
    kc = jnp.arange(GRID_W)[None, :]
    dc = jnp.clip(kc - qc, -(WIN_W - 1), WIN_W - 1) + (WIN_W - 1)
    cs = jnp.clip(qc - WIN_W // 2, 0, GRID_W - WIN_W)
    in_win = (kc >= cs) & (kc < cs + WIN_W)
    tab = rpb[:, dr[:, None, :, None], dc[None, :, None, :]]
    tab = jnp.where(in_win[None, None, :, None, :], tab, NEG_BIG)
    return tab.reshape(rpb.shape[0], WIN_H, GRID_W, WIN_H * GRID_W)


def _attn_prep_kernel(q_ref, k_ref, v_ref, cos_ref, sin_ref, qr_ref, qp_ref, kr_ref, vb_ref, *, nh):
    cos = cos_ref[...]
    sin = sin_ref[...]
    lane = lax.broadcasted_iota(jnp.int32, cos.shape, 1)
    low = (lane % (HEAD_DIM // 2)) < (HEAD_DIM // 4)

    def rope(x):
        swapped = jnp.where(low, pltpu.roll(x, HEAD_DIM - HEAD_DIM // 4, 1), pltpu.roll(x, HEAD_DIM // 4, 1))
        return x * cos + swapped * sin

    for h in range(nh):
        sl = slice(h * HEAD_DIM, (h + 1) * HEAD_DIM)
        q = q_ref[:, sl]
        qr_ref[:, sl] = rope(q).astype(BF16)
        qp_ref[:, sl] = q.astype(BF16)
        kr_ref[:, sl] = rope(k_ref[:, sl]).astype(BF16)
    vb_ref[...] = v_ref[...].astype(BF16)


def _attn_prep(p, cos, sin, bw):
    m = p.shape[0]
    tm = _pick(m, 256, SUBLANES)
    nh = bw // HEAD_DIM
    out = jax.ShapeDtypeStruct((m, bw), BF16)
    blk = lambda c: pl.BlockSpec((tm, bw), lambda i: (i, c))
    tab = pl.BlockSpec((tm, HEAD_DIM), lambda i: (i, 0))
    o_spec = pl.BlockSpec((tm, bw), lambda i: (i, 0))
    return pl.pallas_call(
        functools.partial(_attn_prep_kernel, nh=nh),
        grid=(m // tm,),
        in_specs=[blk(4), blk(5), blk(6), tab, tab],
        out_specs=[o_spec, o_spec, o_spec, o_spec],
        out_shape=[out, out, out, out],
        compiler_params=_params(("arbitrary",), 2 * tm * bw * (3 * 4 + 4 * 2) + 6 * tm * bw * 4),
        name="attention_rope_prep",
    )(p, p, p, cos, sin)


def _softmax_pv(scores, values):
    mx = None
    for s in scores:
        smax = jnp.max(s, axis=-1, keepdims=True)
        mx = smax if mx is None else jnp.maximum(mx, smax)
    den = None
    acc = None
    for s, v in zip(scores, values):
        e = jnp.exp(s - mx)
        esum = jnp.sum(e, axis=-1, keepdims=True)
        den = esum if den is None else den + esum
        pv = jnp.dot(e.astype(BF16), v, preferred_element_type=F32)
        acc = pv if acc is None else acc + pv
    return acc / den


def _dot_nt(a, b):
    return lax.dot_general(a, b, (((1,), (1,)), ((), ())), preferred_element_type=F32)


def _attn_kernel(qr_ref, qp_ref, kr_ref, v_ref, qc_ref, kc_ref, vc_ref, bias_ref, o_ref, *, rows, t_lat, with_ctx):
    scale = HEAD_DIM ** -0.5
    kc = kc_ref[...]
    vc = vc_ref[...]
    nwin = WIN_H * GRID_W

    def body(r, carry):
        ks = jnp.clip(r - WIN_H // 2, 0, rows - WIN_H)
        q0 = pl.multiple_of(r * GRID_W, GRID_W)
        k0 = pl.multiple_of(ks * GRID_W, GRID_W)
        s_lat = _dot_nt(qr_ref[pl.ds(q0, GRID_W), :], kr_ref[pl.ds(k0, nwin), :]) * scale
        b = bias_ref[r - ks]
        s_lat = jnp.where(b > 0.5 * NEG_BIG, s_lat + b, NEG_BIG)
        s_ctx = _dot_nt(qp_ref[pl.ds(q0, GRID_W), :], kc) * scale
        o = _softmax_pv((s_lat, s_ctx), (v_ref[pl.ds(k0, nwin), :], vc))
        o_ref[pl.ds(q0, GRID_W), :] = o.astype(o_ref.dtype)
        return carry

    lax.fori_loop(0, rows, body, 0)
    if with_ctx:
        s = _dot_nt(qc_ref[...], kc) * scale
        o_ref[t_lat:, :] = _softmax_pv((s,), (vc,)).astype(o_ref.dtype)


def _attention(qr, qp, kr, vb, bias, bw, t_lat, with_ctx):
    m = qr.shape[0]
    n_ctx = m - t_lat
    nh = bw // HEAD_DIM
    rows = t_lat // GRID_W
    assert rows >= WIN_H and t_lat % n_ctx == 0
    m_out = m if with_ctx else t_lat
    lat = pl.BlockSpec((t_lat, HEAD_DIM), lambda h: (0, h))
    ctx = pl.BlockSpec((n_ctx, HEAD_DIM), lambda h: (t_lat // n_ctx, h))
    need = 2 * (5 * t_lat * HEAD_DIM * 2 + WIN_H * GRID_W * WIN_H * GRID_W * 4) + 2**22
    return pl.pallas_call(
        functools.partial(_attn_kernel, rows=rows, t_lat=t_lat, with_ctx=with_ctx),
        grid=(nh,),
        in_specs=[lat, lat, lat, lat, ctx, ctx, ctx,
                  pl.BlockSpec((None, WIN_H, GRID_W, WIN_H * GRID_W), lambda h: (h, 0, 0, 0))],
        out_specs=pl.BlockSpec((m_out, HEAD_DIM), lambda h: (0, h)),
        out_shape=jax.ShapeDtypeStruct((m_out, bw), BF16),
        compiler_params=_params(("arbitrary",), need),
        name="neighbourhood_attention",
    )(qr, qp, kr, vb, qp, kr, vb, bias)


def _scan_kernel(*refs, nh, reverse, readout):
    if readout:
        qd_ref, z_ref, i_ref, lb_ref, g_ref, other_ref, ng_ref, o_ref, st_ref = refs
    else:
        qd_ref, z_ref, i_ref, lb_ref, o_ref, st_ref = refs
    c = CHUNK

    @pl.when(pl.program_id(0) == 0)
    def _():
        st_ref[...] = jnp.zeros_like(st_ref)

    lb = lb_ref[...]
    f = lb + (1.0 - lb) * _sigmoid(z_ref[...])
    l = jnp.log(jnp.maximum(f, F_MIN))
    kg = 1.0 - f
    q = _silu(qd_ref[...])
    v = i_ref[...].astype(BF16)

    ri = lax.broadcasted_iota(jnp.int32, (c, 1), 0)
    ti = lax.broadcasted_iota(jnp.int32, (c, c), 0)
    si = lax.broadcasted_iota(jnp.int32, (c, c), 1)

    q_lv = [q.astype(BF16)]
    k_lv = [kg.astype(BF16)]
    masks = [ti == si]
    cf, tot = l, l
    w = 1
    while w < c:
        odd = (ri & w) != 0
        if reverse:
            qe = jnp.where(odd, 0.0, q * jnp.exp(tot - cf + l))
            ke = jnp.where(odd, kg * jnp.exp(cf - l), 0.0)
        else:
            qe = jnp.where(odd, q * jnp.exp(cf), 0.0)
            ke = jnp.where(odd, 0.0, kg * jnp.exp(tot - cf))
        q_lv.append(qe.astype(BF16))
        k_lv.append(ke.astype(BF16))
        masks.append((ti // (2 * w)) == (si // (2 * w)))
        prev_tot = pltpu.roll(tot, w, 0)
        next_tot = pltpu.roll(tot, c - w, 0)
        cf = cf + jnp.where(odd, prev_tot, 0.0)
        tot = tot + jnp.where(odd, prev_tot, next_tot)
        w *= 2
    to_state = (tot - cf + l) if reverse else cf
    to_end = (cf - l) if reverse else (tot - cf)
    q_in = (q * jnp.exp(to_state)).astype(BF16)
    k_out = (kg * jnp.exp(to_end)).astype(BF16)
    carry_decay = jnp.exp(tot[0:1, :])

    for h in range(nh):
        sl = slice(h * HEAD_DIM, (h + 1) * HEAD_DIM)
        att = None
        for qe, ke, mk in zip(q_lv, k_lv, masks):
            a = jnp.where(mk, _dot_nt(qe[:, sl], ke[:, sl]), 0.0)
            att = a if att is None else att + a
        st = st_ref[h]
        o = _dot_nt(q_in[:, sl], st.astype(BF16)) + jnp.dot(att.astype(BF16), v[:, sl], preferred_element_type=F32)
        st_ref[h] = st * carry_decay[:, sl] + lax.dot_general(
            v[:, sl], k_out[:, sl], (((0,), (0,)), ((), ())), preferred_element_type=F32)
        if readout:
            o = o + other_ref[:, sl]
            o = o * lax.rsqrt(jnp.mean(o * o, axis=-1, keepdims=True) + EPS) * ng_ref[...]
            o_ref[:, sl] = (o * _silu(g_ref[:, sl])).astype(o_ref.dtype)
        else:
            o_ref[:, sl] = o


def _scan(p, lb, bw, t_lat, reverse, other=None, norm_g=None):
    m = p.shape[0]
    nh = bw // HEAD_DIM
    n_lat = t_lat // CHUNK
    n_all = m // CHUNK
    n_ctx = n_all - n_lat
    readout = other is not None
    if reverse:
        row_blk = lambda j: jnp.where(j < n_ctx, n_all - 1 - j, n_lat - 1 - (j - n_ctx))
    else:
        row_blk = lambda j: jnp.where(j < n_ctx, n_lat + j, j - n_ctx)
    blk = lambda c: pl.BlockSpec((CHUNK, bw), lambda j: (row_blk(j), c))
    qd_col, zf_col, zb_col, i_col, g_col = 7, 8, 9, 10, 11
    in_specs = [blk(qd_col), blk(zb_col if reverse else zf_col), blk(i_col), pl.BlockSpec((1, bw), lambda j: (0, 0))]
    args = [p, p, p, lb.reshape(1, bw)]
    if readout:
        in_specs += [blk(g_col), blk(0), pl.BlockSpec((1, HEAD_DIM), lambda j: (0, 0))]
        args += [p, other, norm_g.reshape(1, HEAD_DIM)]
    return pl.pallas_call(
        functools.partial(_scan_kernel, nh=nh, reverse=reverse, readout=readout),
        grid=(n_all,),
        in_specs=in_specs,
        out_specs=pl.BlockSpec((CHUNK, bw), lambda j: (row_blk(j), 0)),
        out_shape=jax.ShapeDtypeStruct((m, bw), BF16 if readout else F32),
        scratch_shapes=[pltpu.VMEM((nh, HEAD_DIM, HEAD_DIM), F32)],
        compiler_params=_params(("arbitrary",), 48 * CHUNK * bw * 4),
        name="hgrn2_scan_bwd" if reverse else "hgrn2_scan_fwd",
    )(*args)


def kernel(x, c, ctx, c_ctx, w_ada, b_ada, norm1_g, norm2_g, w_in, a_norm_g, a_norm_b, a_ws, a_bs, b_conv_w,
           b_conv_b, b_norm_g, b_norm_b, c_rpb, d_lb_logits, d_norm_g, w_branch, w_gate, b_gate, w_out, w_ffn1,
           w_ffn3, w_ffn2, final_g):
    batch, t_lat, d = x.shape
    n_ctx = ctx.shape[1]
    depth = w_ada.shape[0]
    bw = d // N_BRANCH
    assert batch == 1 and t_lat % GRID_W == 0 and t_lat % CHUNK == 0 and n_ctx % CHUNK == 0

    xa = jnp.concatenate([x[0], ctx[0]], axis=0)
    cond = jnp.zeros((SUBLANES, d), F32).at[0].set(c[0]).at[1].set(c_ctx)
    mod = _modulation(cond, w_ada, b_ada)

    prob = jax.nn.softmax(d_lb_logits.astype(F32), axis=1)
    lbound = jnp.cumsum(prob, axis=1) - prob[:, :1]

    cos, sin = _rope_tables(t_lat, n_ctx)
    w_in_b, w_gate_b, w_branch_b, w_out_b = (w.astype(BF16) for w in (w_in, w_gate, w_branch, w_out))
    w1_b, w3_b, w2_b = (w.astype(BF16) for w in (w_ffn1, w_ffn3, w_ffn2))

    for l in range(depth):
        last = l == depth - 1
        m = t_lat if last else t_lat + n_ctx
        h = _norm_mod(xa, norm1_g[l], mod[l], 0, 1, t_lat)
        p = _matmul(h, w_in_b[l], F32)
        ya = _mix_a(p, a_ws[l], a_bs[l], a_norm_g[l], a_norm_b[l], bw)
        yb = _mix_b(p, b_conv_w[l], b_conv_b[l], b_norm_g[l], b_norm_b[l], bw, t_lat)
        qr, qp, kr, vb = _attn_prep(p, cos, sin, bw)
        yc = _attention(qr, qp, kr, vb, _bias_table(c_rpb[l]), bw, t_lat, with_ctx=not last)
        o_bwd = _scan(p, lbound[1, l], bw, t_lat, reverse=True)
        yd = _scan(p, lbound[0, l], bw, t_lat, reverse=False, other=o_bwd, norm_g=d_norm_g[l])
        merged = _gated_merge(h, (ya, yb, yc, yd), w_gate_b[l], b_gate[l], w_branch_b[l], m)
        x1 = _matmul_residual(merged, w_out_b[l], xa, mod[l], 2, t_lat, m, 1024, 1024, "output_projection")
        hf = _norm_mod(x1, norm2_g[l], mod[l], 3, 4, t_lat)
        act = _swiglu_up(hf, w1_b[l], w3_b[l], m)
        xa = _matmul_residual(act, w2_b[l], x1, mod[l], 5, t_lat, m, 512, 512, "ffn_down")

    return _final_norm(xa[:t_lat], final_g)[None]
```

```python
import functools
import math

import jax
import jax.numpy as jnp
from jax import lax
from jax.experimental import pallas as pl
from jax.experimental.pallas import tpu as pltpu

F32 = jnp.float32
BF16 = jnp.bfloat16

HEAD_DIM = 128
CHUNK = 128
GRID_W = 64
WIN_H = 8
WIN_W = 16
CONV_K = 31
N_BRANCH = 4
ROPE_THETA = 10000.0
EPS = 1e-6
F_MIN = 1e-6
NEG_BIG = -1e30

V7X_VMEM_BYTES = 64 * 2**20
VMEM_CAP = V7X_VMEM_BYTES - 6 * 2**20
SUBLANES = 8
LANES = 128
CONV_HALO = 16


def _pick(n, target, quantum):
    best = None
    for t in range(quantum, min(n, target) + 1, quantum):
        if n % t == 0:
            best = t
    assert best is not None, (n, target, quantum)
    return best


def _params(sem, need_bytes):
    limit = int(min(VMEM_CAP, max(need_bytes * 5 // 4, 16 * 2**20)))
    return pltpu.CompilerParams(dimension_semantics=sem, vmem_limit_bytes=limit)


def _sigmoid(x):
    return jax.nn.sigmoid(x)


def _silu(x):
    return x * jax.nn.sigmoid(x)


def _gelu(x):
    return 0.5 * x * (1.0 + lax.erf(x * 0.7071067811865476))


def _row_select(rows_is_ctx, p_ref):
    return jnp.where(rows_is_ctx, p_ref[1:2, :], p_ref[0:1, :])


def _mod_kernel(cond_ref, w_ref, b_ref, o_ref):
    s = _silu(cond_ref[...]).astype(BF16)
    o_ref[...] = jnp.dot(s, w_ref[...].astype(BF16), preferred_element_type=F32) + b_ref[...]


def _modulation(cond, w_ada, b_ada):
    depth, d, n = w_ada.shape
    tn = _pick(n, 512, LANES)
    need = 2 * d * tn * 4 + d * tn * 2
    return pl.pallas_call(
        _mod_kernel,
        grid=(depth, n // tn),
        in_specs=[
            pl.BlockSpec((SUBLANES, d), lambda l, j: (0, 0)),
            pl.BlockSpec((None, d, tn), lambda l, j: (l, 0, j)),
            pl.BlockSpec((None, 1, tn), lambda l, j: (l, 0, j)),
        ],
        out_specs=pl.BlockSpec((None, SUBLANES, tn), lambda l, j: (l, 0, j)),
        out_shape=jax.ShapeDtypeStruct((depth, SUBLANES, n), F32),
        compiler_params=_params(("arbitrary", "arbitrary"), need),
        name="adaln_modulation",
    )(cond, w_ada, b_ada.reshape(depth, 1, n))


def _norm_mod_kernel(x_ref, g_ref, sh_ref, sc_ref, o_ref, *, t_lat, tm):
    x = x_ref[...]
    y = x * lax.rsqrt(jnp.mean(x * x, axis=-1, keepdims=True) + EPS) * g_ref[...]
    rows = pl.program_id(0) * tm + lax.broadcasted_iota(jnp.int32, (tm, 1), 0)
    is_ctx = rows >= t_lat
    o_ref[...] = (y * (1.0 + _row_select(is_ctx, sc_ref)) + _row_select(is_ctx, sh_ref)).astype(o_ref.dtype)


def _norm_mod(x, g, mod, layer, shift_blk, scale_blk, t_lat):
    m, d = x.shape
    tm = _pick(m, 256, SUBLANES)
    return pl.pallas_call(
        functools.partial(_norm_mod_kernel, t_lat=t_lat, tm=tm),
        grid=(m // tm,),
        in_specs=[
            pl.BlockSpec((tm, d), lambda i: (i, 0)),
            pl.BlockSpec((1, d), lambda i: (0, 0)),
            pl.BlockSpec((None, SUBLANES, d), lambda i: (layer, 0, shift_blk)),
            pl.BlockSpec((None, SUBLANES, d), lambda i: (layer, 0, scale_blk)),
        ],
        out_specs=pl.BlockSpec((tm, d), lambda i: (i, 0)),
        out_shape=jax.ShapeDtypeStruct((m, d), BF16),
        compiler_params=_params(("arbitrary",), 2 * tm * d * 6 + 4 * tm * d * 4),
        name="rmsnorm_modulate",
    )(x, g.reshape(1, d), mod, mod)


def _final_norm_kernel(x_ref, g_ref, o_ref):
    x = x_ref[...]
    o_ref[...] = x * lax.rsqrt(jnp.mean(x * x, axis=-1, keepdims=True) + EPS) * g_ref[...]


def _final_norm(x, g):
    m, d = x.shape
    tm = _pick(m, 256, SUBLANES)
    return pl.pallas_call(
        _final_norm_kernel,
        grid=(m // tm,),
        in_specs=[pl.BlockSpec((tm, d), lambda i: (i, 0)), pl.BlockSpec((1, d), lambda i: (0, 0))],
        out_specs=pl.BlockSpec((tm, d), lambda i: (i, 0)),
        out_shape=jax.ShapeDtypeStruct((m, d), F32),
        compiler_params=_params(("arbitrary",), 2 * tm * d * 8 + 2 * tm * d * 4),
        name="final_rmsnorm",
    )(x, g.reshape(1, d))


def _ws_mm_kernel(a_ref, w_ref, o_ref, wb):
    @pl.when(pl.program_id(1) == 0)
    def _():
        wb[...] = w_ref[...].astype(BF16)

    o_ref[...] = jnp.dot(a_ref[...], wb[...], preferred_element_type=F32).astype(o_ref.dtype)


def _ws_matmul(a, w, layer, out_dtype):
    m, k = a.shape
    n = w.shape[2]
    tm = _pick(m, 1408, LANES)
    tn = _pick(n, 512, LANES)
    osz = jnp.dtype(out_dtype).itemsize
    need = 2 * (tm * k * 2 + k * tn * 4 + tm * tn * osz) + k * tn * 2 + tm * tn * 4
    return pl.pallas_call(
        _ws_mm_kernel,
        grid=(n // tn, m // tm),
        in_specs=[pl.BlockSpec((tm, k), lambda j, i: (i, 0)), pl.BlockSpec((None, k, tn), lambda j, i: (layer, 0, j))],
        out_specs=pl.BlockSpec((tm, tn), lambda j, i: (i, j)),
        out_shape=jax.ShapeDtypeStruct((m, n), out_dtype),
        scratch_shapes=[pltpu.VMEM((k, tn), BF16)],
        compiler_params=_params(("arbitrary", "arbitrary"), need),
        name="input_projection",
    )(a, w)


def _ws_resid_kernel(a_ref, w_ref, x_ref, gate_ref, o_ref, wb, *, t_lat, tm):
    @pl.when(pl.program_id(1) == 0)
    def _():
        wb[...] = w_ref[...].astype(BF16)

    rows = pl.program_id(1) * tm + lax.broadcasted_iota(jnp.int32, (tm, 1), 0)
    gate = _row_select(rows >= t_lat, gate_ref)
    o_ref[...] = x_ref[...] + gate * jnp.dot(a_ref[...], wb[...], preferred_element_type=F32)


def _ws_matmul_residual(a, w, layer, x, mod, gate_blk, t_lat, m):
    k = a.shape[1]
    n = w.shape[2]
    tm = _pick(m, 1408, LANES)
    tn = _pick(n, 512, LANES)
    nb = n // tn
    need = 2 * (tm * k * 2 + k * tn * 4 + 2 * tm * tn * 4) + k * tn * 2 + tm * tn * 4
    return pl.pallas_call(
        functools.partial(_ws_resid_kernel, t_lat=t_lat, tm=tm),
        grid=(nb, m // tm),
        in_specs=[
            pl.BlockSpec((tm, k), lambda j, i: (i, 0)),
            pl.BlockSpec((None, k, tn), lambda j, i: (layer, 0, j)),
            pl.BlockSpec((tm, tn), lambda j, i: (i, j)),
            pl.BlockSpec((None, SUBLANES, tn), lambda j, i: (layer, 0, gate_blk * nb + j)),
        ],
        out_specs=pl.BlockSpec((tm, tn), lambda j, i: (i, j)),
        out_shape=jax.ShapeDtypeStruct((m, n), F32),
        scratch_shapes=[pltpu.VMEM((k, tn), BF16)],
        compiler_params=_params(("arbitrary", "arbitrary"), need),
        name="output_projection",
    )(a, w, x, mod)


def _ws_swiglu_kernel(a_ref, w1_ref, w3_ref, o_ref, w1b, w3b):
    @pl.when(pl.program_id(1) == 0)
    def _():
        w1b[...] = w1_ref[...].astype(BF16)
        w3b[...] = w3_ref[...].astype(BF16)

    a = a_ref[...]
    u = jnp.dot(a, w1b[...], preferred_element_type=F32)
    g = jnp.dot(a, w3b[...], preferred_element_type=F32)
    o_ref[...] = (_silu(u) * g).astype(o_ref.dtype)


def _ws_swiglu_up(a, w1, w3, layer, m):
    k = a.shape[1]
    n = w1.shape[2]
    tm = _pick(m, 1408, LANES)
    tn = _pick(n, 256, LANES)
    need = 2 * (tm * k * 2 + 2 * k * tn * 4 + tm * tn * 2) + 2 * k * tn * 2 + 3 * tm * tn * 4
    w_spec = pl.BlockSpec((None, k, tn), lambda j, i: (layer, 0, j))
    return pl.pallas_call(
        _ws_swiglu_kernel,
        grid=(n // tn, m // tm),
        in_specs=[pl.BlockSpec((tm, k), lambda j, i: (i, 0)), w_spec, w_spec],
        out_specs=pl.BlockSpec((tm, tn), lambda j, i: (i, j)),
        out_shape=jax.ShapeDtypeStruct((m, n), BF16),
        scratch_shapes=[pltpu.VMEM((k, tn), BF16), pltpu.VMEM((k, tn), BF16)],
        compiler_params=_params(("arbitrary", "arbitrary"), need),
        name="swiglu_up",
    )(a, w1, w3)


def _mm_resid_kernel(a_ref, w_ref, x_ref, gate_ref, o_ref, *, t_lat, tm):
    rows = pl.program_id(0) * tm + lax.broadcasted_iota(jnp.int32, (tm, 1), 0)
    gate = _row_select(rows >= t_lat, gate_ref)
    o_ref[...] = x_ref[...] + gate * jnp.dot(a_ref[...], w_ref[...], preferred_element_type=F32)


def _matmul_residual(a, w, layer, x, mod, gate_blk, t_lat, m):
    k = a.shape[1]
    n = w.shape[2]
    tm = _pick(m, 768, 256)
    tn = _pick(n, 256 if tm > 512 else 512, LANES)
    need = 2 * (tm * k * 2 + k * tn * 2 + 2 * tm * tn * 4) + tm * tn * 4
    nb = n // tn
    return pl.pallas_call(
        functools.partial(_mm_resid_kernel, t_lat=t_lat, tm=tm),
        grid=(m // tm, nb),
        in_specs=[
            pl.BlockSpec((tm, k), lambda i, j: (i, 0)),
            pl.BlockSpec((None, k, tn), lambda i, j: (layer, 0, j)),
            pl.BlockSpec((tm, tn), lambda i, j: (i, j)),
            pl.BlockSpec((None, SUBLANES, tn), lambda i, j: (layer, 0, gate_blk * nb + j)),
        ],
        out_specs=pl.BlockSpec((tm, tn), lambda i, j: (i, j)),
        out_shape=jax.ShapeDtypeStruct((m, n), F32),
        compiler_params=_params(("arbitrary", "arbitrary"), need),
        name="ffn_down",
    )(a, w, x, mod)


def _merge_kernel(h_ref, ya_ref, yb_ref, yc_ref, yd_ref, wg0, wg1, wg2, wg3, bg0, bg1, bg2, bg3, wb_ref, o_ref):
    h = h_ref[...]
    acc = None
    for i, (y_ref, wg, bg) in enumerate(((ya_ref, wg0, bg0), (yb_ref, wg1, bg1), (yc_ref, wg2, bg2), (yd_ref, wg3, bg3))):
        gate = _sigmoid(jnp.dot(h, wg[...], preferred_element_type=F32) + bg[...])
        term = gate * jnp.dot(y_ref[...], wb_ref[i], preferred_element_type=F32)
        acc = term if acc is None else acc + term
    o_ref[...] = acc.astype(o_ref.dtype)


def _gated_merge(h, ys, w_gate, b_gate, w_branch, layer, m):
    k = h.shape[1]
    depth, nbr, bw, n = w_branch.shape
    tm = _pick(m, 768, 256)
    tn = _pick(n, 256, LANES)
    nb = n // tn
    need = 2 * (tm * k * 2 + nbr * tm * bw * 2 + nbr * k * tn * 2 + nbr * bw * tn * 2 + tm * tn * 2) + 4 * tm * tn * 4
    y_spec = pl.BlockSpec((tm, bw), lambda i, j: (i, 0))
    wg_specs = [pl.BlockSpec((None, k, tn), functools.partial(lambda i, j, b: (layer, 0, b * nb + j), b=b))
                for b in range(nbr)]
    bg_specs = [pl.BlockSpec((None, 1, tn), functools.partial(lambda i, j, b: (layer, 0, b * nb + j), b=b))
                for b in range(nbr)]
    bg = b_gate.reshape(depth, 1, -1)
    return pl.pallas_call(
        _merge_kernel,
        grid=(m // tm, nb),
        in_specs=[pl.BlockSpec((tm, k), lambda i, j: (i, 0)), y_spec, y_spec, y_spec, y_spec]
        + wg_specs + bg_specs + [pl.BlockSpec((None, nbr, bw, tn), lambda i, j: (layer, 0, 0, j))],
        out_specs=pl.BlockSpec((tm, tn), lambda i, j: (i, j)),
        out_shape=jax.ShapeDtypeStruct((m, n), BF16),
        compiler_params=_params(("arbitrary", "arbitrary"), need),
        name="gated_merge",
    )(h, *ys, w_gate, w_gate, w_gate, w_gate, bg, bg, bg, bg, w_branch)


def _mix_a_kernel(pu_ref, pv_ref, ws_ref, bias_ref, ng_ref, nb_ref, o_ref, *, nh):
    zu = _gelu(pu_ref[...])
    zv = _gelu(pv_ref[...])
    mu = jnp.mean(zv, axis=-1, keepdims=True)
    zc = zv - mu
    var = jnp.mean(zc * zc, axis=-1, keepdims=True)
    v = (zc * lax.rsqrt(var + EPS) * ng_ref[...] + nb_ref[...]).astype(BF16)
    for g in range(nh):
        sl = slice(g * HEAD_DIM, (g + 1) * HEAD_DIM)
        s = jnp.dot(ws_ref[g], v[:, sl], preferred_element_type=F32) + bias_ref[:, sl]
        o_ref[:, sl] = (zu[:, sl] * s).astype(o_ref.dtype)


def _mix_a(p, ws, bs, ng, nb, bw):
    m = p.shape[0]
    nh = bw // HEAD_DIM
    bias = jnp.repeat(bs.T, HEAD_DIM, axis=1)
    return pl.pallas_call(
        functools.partial(_mix_a_kernel, nh=nh),
        grid=(m // CHUNK,),
        in_specs=[
            pl.BlockSpec((CHUNK, bw), lambda i: (i, 0)),
            pl.BlockSpec((CHUNK, bw), lambda i: (i, 1)),
            pl.BlockSpec((nh, CHUNK, CHUNK), lambda i: (0, 0, 0)),
            pl.BlockSpec((CHUNK, bw), lambda i: (0, 0)),
            pl.BlockSpec((1, bw), lambda i: (0, 0)),
            pl.BlockSpec((1, bw), lambda i: (0, 0)),
        ],
        out_specs=pl.BlockSpec((CHUNK, bw), lambda i: (i, 0)),
        out_shape=jax.ShapeDtypeStruct((m, bw), BF16),
        compiler_params=_params(("arbitrary",), 16 * CHUNK * bw * 4),
        name="mixer_gmlp",
    )(p, p, ws.astype(BF16), bias, ng.reshape(1, bw), nb.reshape(1, bw))


CONV_ROWS = 16
CONV_WIN = CONV_ROWS + 2 * CONV_HALO


def _mix_b_kernel(ap_ref, ac_ref, an_ref, gp_ref, gc_ref, gn_ref, cw_ref, cb_ref, ng_ref, nb_ref, o_ref, ybuf,
                  *, tiles_lat, tiles_tot, tmb):
    i = pl.program_id(0)
    first = jnp.logical_or(i == 0, i == tiles_lat)
    last = jnp.logical_or(i == tiles_lat - 1, i == tiles_tot - 1)
    yp = ap_ref[...] * _sigmoid(gp_ref[...])
    yn = an_ref[...] * _sigmoid(gn_ref[...])
    ybuf[0:CONV_HALO, :] = jnp.where(first, 0.0, yp)
    ybuf[CONV_HALO:CONV_HALO + tmb, :] = ac_ref[...] * _sigmoid(gc_ref[...])
    ybuf[CONV_HALO + tmb:, :] = jnp.where(last, 0.0, yn)
    off = CONV_HALO - CONV_K // 2

    def row_block(rb, carry):
        r0 = pl.multiple_of(rb * CONV_ROWS, CONV_ROWS)
        win = ybuf[pl.ds(r0, CONV_WIN), :]
        acc = None
        for b in range(SUBLANES):
            shifted = win if b == 0 else pltpu.roll(win, CONV_WIN - b, 0)
            for j in range(CONV_K):
                if (off + j) % SUBLANES != b:
                    continue
                a = (off + j) - b
                term = cw_ref[j:j + 1, :] * shifted[a:a + CONV_ROWS, :]
                acc = term if acc is None else acc + term
        y = acc + cb_ref[...]
        mu = jnp.mean(y, axis=-1, keepdims=True)
        yc = y - mu
        var = jnp.mean(yc * yc, axis=-1, keepdims=True)
        z = yc * lax.rsqrt(var + EPS) * ng_ref[...] + nb_ref[...]
        o_ref[pl.ds(r0, CONV_ROWS), :] = _silu(z).astype(o_ref.dtype)
        return carry

    lax.fori_loop(0, tmb // CONV_ROWS, row_block, 0)


def _mix_b(p, cw, cb, ng, nb, bw, t_lat):
    m = p.shape[0]
    tmb = _pick(math.gcd(t_lat, m - t_lat), 256, CONV_HALO)
    hb = tmb // CONV_HALO
    nhalo = m // CONV_HALO
    a_col, g_col = 2, 3
    prev = lambda c: (lambda i: (jnp.maximum(i * hb - 1, 0), c))
    cur = lambda c: (lambda i: (i, c))
    nxt = lambda c: (lambda i: (jnp.minimum((i + 1) * hb, nhalo - 1), c))
    vec = pl.BlockSpec((1, bw), lambda i: (0, 0))
    return pl.pallas_call(
        functools.partial(_mix_b_kernel, tiles_lat=t_lat // tmb, tiles_tot=m // tmb, tmb=tmb),
        grid=(m // tmb,),
        in_specs=[
            pl.BlockSpec((CONV_HALO, bw), prev(a_col)),
            pl.BlockSpec((tmb, bw), cur(a_col)),
            pl.BlockSpec((CONV_HALO, bw), nxt(a_col)),
            pl.BlockSpec((CONV_HALO, bw), prev(g_col)),
            pl.BlockSpec((tmb, bw), cur(g_col)),
            pl.BlockSpec((CONV_HALO, bw), nxt(g_col)),
            pl.BlockSpec((CONV_K, bw), lambda i: (0, 0)),
            vec, vec, vec,
        ],
        out_specs=pl.BlockSpec((tmb, bw), lambda i: (i, 0)),
        out_shape=jax.ShapeDtypeStruct((m, bw), BF16),
        scratch_shapes=[pltpu.VMEM((tmb + 2 * CONV_HALO, bw), F32)],
        compiler_params=_params(("arbitrary",), 8 * tmb * bw * 4),
        name="mixer_conformer_conv",
    )(p, p, p, p, p, p, cw, cb.reshape(1, bw), ng.reshape(1, bw), nb.reshape(1, bw))


def _rope_tables(t_lat, n_ctx):
    m = HEAD_DIM // 4
    freqs = ROPE_THETA ** (-jnp.arange(m, dtype=F32) / m)
    r = t_lat // GRID_W
    ang_r = jnp.repeat(jnp.arange(r, dtype=F32)[:, None] * freqs[None, :], GRID_W, axis=0)
    ang_c = jnp.tile(jnp.arange(GRID_W, dtype=F32)[:, None] * freqs[None, :], (r, 1))
    cos = jnp.concatenate([jnp.cos(ang_r), jnp.cos(ang_r), jnp.cos(ang_c), jnp.cos(ang_c)], axis=-1)
    sin = jnp.concatenate([-jnp.sin(ang_r), jnp.sin(ang_r), -jnp.sin(ang_c), jnp.sin(ang_c)], axis=-1)
    cos = jnp.concatenate([cos, jnp.ones((n_ctx, HEAD_DIM), F32)], axis=0)
    sin = jnp.concatenate([sin, jnp.zeros((n_ctx, HEAD_DIM), F32)], axis=0)
    return cos, sin


def _bias_kernel(rpb_ref, o_ref):
    n = GRID_W * GRID_W
    col = lax.broadcasted_iota(jnp.int32, (1, n), 1)
    q = col // GRID_W
    k = col % GRID_W
    dc = jnp.clip(k - q, -(WIN_W - 1), WIN_W - 1) + (WIN_W - 1)
    start = jnp.clip(q - WIN_W // 2, 0, GRID_W - WIN_W)
    in_win = jnp.logical_and(k >= start, k < start + WIN_W)
    val = jnp.zeros(o_ref.shape, F32)
    for d in range(2 * WIN_W - 1):
        val = jnp.where(dc == d, rpb_ref[:, d:d + 1], val)
    o_ref[...] = jnp.where(in_win, val, NEG_BIG)


def _bias_table(rpb):
    depth, nh, ndr, ndc = rpb.shape
    rows = depth * nh * ndr
    toe = pl.pallas_call(
        _bias_kernel,
        out_shape=jax.ShapeDtypeStruct((rows, GRID_W * GRID_W), F32),
        compiler_params=_params((), 16 * rows * GRID_W * GRID_W),
        name="attention_bias_table",
    )(rpb.reshape(rows, ndc))
    toe = toe.reshape(depth, nh, ndr, GRID_W, GRID_W)
    tab = jnp.stack([toe[:, :, WIN_H - 1 - v:2 * WIN_H - 1 - v] for v in range(WIN_H)], axis=2)
    return tab.transpose(0, 1, 2, 4, 3, 5).reshape(depth, nh, WIN_H, GRID_W, WIN_H * GRID_W)


def _attn_prep_kernel(q_ref, k_ref, v_ref, cos_ref, sin_ref, qr_ref, qp_ref, kr_ref, vb_ref, *, nh):
    cos = cos_ref[...]
    sin = sin_ref[...]
    lane = lax.broadcasted_iota(jnp.int32, cos.shape, 1)
    low = (lane % (HEAD_DIM // 2)) < (HEAD_DIM // 4)

    def rope(x):
        swapped = jnp.where(low, pltpu.roll(x, HEAD_DIM - HEAD_DIM // 4, 1), pltpu.roll(x, HEAD_DIM // 4, 1))
        return x * cos + swapped * sin

    for h in range(nh):
        sl = slice(h * HEAD_DIM, (h + 1) * HEAD_DIM)
        q = q_ref[:, sl]
        qr_ref[:, sl] = rope(q).astype(BF16)
        qp_ref[:, sl] = q.astype(BF16)
        kr_ref[:, sl] = rope(k_ref[:, sl]).astype(BF16)
    vb_ref[...] = v_ref[...].astype(BF16)


def _attn_prep(p, cos, sin, bw):
    m = p.shape[0]
    tm = _pick(m, 256, SUBLANES)
    nh = bw // HEAD_DIM
    out = jax.ShapeDtypeStruct((m, bw), BF16)
    blk = lambda c: pl.BlockSpec((tm, bw), lambda i: (i, c))
    tab = pl.BlockSpec((tm, HEAD_DIM), lambda i: (i, 0))
    o_spec = pl.BlockSpec((tm, bw), lambda i: (i, 0))
    return pl.pallas_call(
        functools.partial(_attn_prep_kernel, nh=nh),
        grid=(m // tm,),
        in_specs=[blk(4), blk(5), blk(6), tab, tab],
        out_specs=[o_spec, o_spec, o_spec, o_spec],
        out_shape=[out, out, out, out],
        compiler_params=_params(("arbitrary",), 2 * tm * bw * (3 * 4 + 4 * 2) + 6 * tm * bw * 4),
        name="attention_rope_prep",
    )(p, p, p, cos, sin)


ATTN_UNROLL = 4


def _softmax_pv(scores, values):
    mx = None
    for s in scores:
        smax = jnp.max(s, axis=-1, keepdims=True)
        mx = smax if mx is None else jnp.maximum(mx, smax)
    den = None
    acc = None
    for s, v in zip(scores, values):
        e = jnp.exp(s - mx)
        esum = jnp.sum(e, axis=-1, keepdims=True)
        den = esum if den is None else den + esum
        pv = jnp.dot(e.astype(BF16), v, preferred_element_type=F32)
        acc = pv if acc is None else acc + pv
    return acc / den


def _dot_nt(a, b):
    return lax.dot_general(a, b, (((1,), (1,)), ((), ())), preferred_element_type=F32)


def _attn_kernel(qr_ref, qp_ref, kr_ref, v_ref, qc_ref, kc_ref, vc_ref, bias_ref, o_ref, *, rows, t_lat, with_ctx):
    scale = HEAD_DIM ** -0.5
    kc = kc_ref[...]
    vc = vc_ref[...]
    nwin = WIN_H * GRID_W

    def body(r, carry):
        ks = jnp.clip(r - WIN_H // 2, 0, rows - WIN_H)
        q0 = pl.multiple_of(r * GRID_W, GRID_W)
        k0 = pl.multiple_of(ks * GRID_W, GRID_W)
        s_lat = _dot_nt(qr_ref[pl.ds(q0, GRID_W), :], kr_ref[pl.ds(k0, nwin), :]) * scale
        b = bias_ref[r - ks]
        s_lat = jnp.where(b > 0.5 * NEG_BIG, s_lat + b, NEG_BIG)
        s_ctx = _dot_nt(qp_ref[pl.ds(q0, GRID_W), :], kc) * scale
        o = _softmax_pv((s_lat, s_ctx), (v_ref[pl.ds(k0, nwin), :], vc))
        o_ref[pl.ds(q0, GRID_W), :] = o.astype(o_ref.dtype)
        return carry

    lax.fori_loop(0, rows, body, 0, unroll=ATTN_UNROLL)
    if with_ctx:
        s = _dot_nt(qc_ref[...], kc) * scale
        o_ref[t_lat:, :] = _softmax_pv((s,), (vc,)).astype(o_ref.dtype)


def _attention(qr, qp, kr, vb, bias, layer, bw, t_lat, with_ctx):
    m = qr.shape[0]
    n_ctx = m - t_lat
    nh = bw // HEAD_DIM
    rows = t_lat // GRID_W
    assert rows >= WIN_H and t_lat % n_ctx == 0
    m_out = m if with_ctx else t_lat
    lat = pl.BlockSpec((t_lat, HEAD_DIM), lambda h: (0, h))
    ctx = pl.BlockSpec((n_ctx, HEAD_DIM), lambda h: (t_lat // n_ctx, h))
    need = 2 * (5 * t_lat * HEAD_DIM * 2 + WIN_H * GRID_W * WIN_H * GRID_W * 4) + 2**22
    return pl.pallas_call(
        functools.partial(_attn_kernel, rows=rows, t_lat=t_lat, with_ctx=with_ctx),
        grid=(nh,),
        in_specs=[lat, lat, lat, lat, ctx, ctx, ctx,
                  pl.BlockSpec((None, None, WIN_H, GRID_W, WIN_H * GRID_W), lambda h: (layer, h, 0, 0, 0))],
        out_specs=pl.BlockSpec((m_out, HEAD_DIM), lambda h: (0, h)),
        out_shape=jax.ShapeDtypeStruct((m_out, bw), BF16),
        compiler_params=_params(("arbitrary",), need),
        name="neighbourhood_attention",
    )(qr, qp, kr, vb, qp, kr, vb, bias)


def _scan_kernel(*refs, nh, reverse, readout):
    if readout:
        qd_ref, z_ref, i_ref, lb_ref, g_ref, other_ref, ng_ref, o_ref, st_ref = refs
    else:
        qd_ref, z_ref, i_ref, lb_ref, o_ref, st_ref = refs
    c = CHUNK

    @pl.when(pl.program_id(0) == 0)
    def _():
        st_ref[...] = jnp.zeros_like(st_ref)

    lb = lb_ref[...]
    f = lb + (1.0 - lb) * _sigmoid(z_ref[...])
    l = jnp.log(jnp.maximum(f, F_MIN))
    kg = 1.0 - f
    q = _silu(qd_ref[...])
    v = i_ref[...].astype(BF16)

    ri = lax.broadcasted_iota(jnp.int32, (c, 1), 0)
    ti = lax.broadcasted_iota(jnp.int32, (c, c), 0)
    si = lax.broadcasted_iota(jnp.int32, (c, c), 1)

    q_lv = [q.astype(BF16)]
    k_lv = [kg.astype(BF16)]
    masks = [ti == si]
    cf, tot = l, l
    w = 1
    while w < c:
        odd = (ri & w) != 0
        if reverse:
            qe = jnp.where(odd, 0.0, q * jnp.exp(tot - cf + l))
            ke = jnp.where(odd, kg * jnp.exp(cf - l), 0.0)
        else:
            qe = jnp.where(odd, q * jnp.exp(cf), 0.0)
            ke = jnp.where(odd, 0.0, kg * jnp.exp(tot - cf))
        q_lv.append(qe.astype(BF16))
        k_lv.append(ke.astype(BF16))
        masks.append((ti // (2 * w)) == (si // (2 * w)))
        prev_tot = pltpu.roll(tot, w, 0)
        next_tot = pltpu.roll(tot, c - w, 0)
        cf = cf + jnp.where(odd, prev_tot, 0.0)
        tot = tot + jnp.where(odd, prev_tot, next_tot)
        w *= 2
    to_state = (tot - cf + l) if reverse else cf
    to_end = (cf - l) if reverse else (tot - cf)
    q_in = (q * jnp.exp(to_state)).astype(BF16)
    k_out = (kg * jnp.exp(to_end)).astype(BF16)
    carry_decay = jnp.exp(tot[0:1, :])

    for h in range(nh):
        sl = slice(h * HEAD_DIM, (h + 1) * HEAD_DIM)
        att = None
        for qe, ke, mk in zip(q_lv, k_lv, masks):
            a = jnp.where(mk, _dot_nt(qe[:, sl], ke[:, sl]), 0.0)
            att = a if att is None else att + a
        st = st_ref[h]
        o = _dot_nt(q_in[:, sl], st.astype(BF16)) + jnp.dot(att.astype(BF16), v[:, sl], preferred_element_type=F32)
        st_ref[h] = st * carry_decay[:, sl] + lax.dot_general(
            v[:, sl], k_out[:, sl], (((0,), (0,)), ((), ())), preferred_element_type=F32)
        if readout:
            o = o + other_ref[:, sl]
            o = o * lax.rsqrt(jnp.mean(o * o, axis=-1, keepdims=True) + EPS) * ng_ref[...]
            o_ref[:, sl] = (o * _silu(g_ref[:, sl])).astype(o_ref.dtype)
        else:
            o_ref[:, sl] = o


def _scan(p, lb, bw, t_lat, reverse, other=None, norm_g=None):
    m = p.shape[0]
    nh = bw // HEAD_DIM
    n_lat = t_lat // CHUNK
    n_all = m // CHUNK
    n_ctx = n_all - n_lat
    readout = other is not None
    if reverse:
        row_blk = lambda j: jnp.where(j < n_ctx, n_all - 1 - j, n_lat - 1 - (j - n_ctx))
    else:
        row_blk = lambda j: jnp.where(j < n_ctx, n_lat + j, j - n_ctx)
    blk = lambda c: pl.BlockSpec((CHUNK, bw), lambda j: (row_blk(j), c))
    qd_col, zf_col, zb_col, i_col, g_col = 7, 8, 9, 10, 11
    in_specs = [blk(qd_col), blk(zb_col if reverse else zf_col), blk(i_col), pl.BlockSpec((1, bw), lambda j: (0, 0))]
    args = [p, p, p, lb.reshape(1, bw)]
    if readout:
        in_specs += [blk(g_col), blk(0), pl.BlockSpec((1, HEAD_DIM), lambda j: (0, 0))]
        args += [p, other, norm_g.reshape(1, HEAD_DIM)]
    return pl.pallas_call(
        functools.partial(_scan_kernel, nh=nh, reverse=reverse, readout=readout),
        grid=(n_all,),
        in_specs=in_specs,
        out_specs=pl.BlockSpec((CHUNK, bw), lambda j: (row_blk(j), 0)),
        out_shape=jax.ShapeDtypeStruct((m, bw), BF16 if readout else F32),
        scratch_shapes=[pltpu.VMEM((nh, HEAD_DIM, HEAD_DIM), F32)],
        compiler_params=_params(("arbitrary",), 48 * CHUNK * bw * 4),
        name="hgrn2_scan_bwd" if reverse else "hgrn2_scan_fwd",
    )(*args)


def kernel(x, c, ctx, c_ctx, w_ada, b_ada, norm1_g, norm2_g, w_in, a_norm_g, a_norm_b, a_ws, a_bs, b_conv_w,
           b_conv_b, b_norm_g, b_norm_b, c_rpb, d_lb_logits, d_norm_g, w_branch, w_gate, b_gate, w_out, w_ffn1,
           w_ffn3, w_ffn2, final_g):
    batch, t_lat, d = x.shape
    n_ctx = ctx.shape[1]
    depth = w_ada.shape[0]
    bw = d // N_BRANCH
    assert batch == 1 and t_lat % GRID_W == 0 and t_lat % CHUNK == 0 and n_ctx % CHUNK == 0

    xa = jnp.concatenate([x[0], ctx[0]], axis=0)
    cond = jnp.zeros((SUBLANES, d), F32).at[0].set(c[0]).at[1].set(c_ctx)
    mod = _modulation(cond, w_ada, b_ada)

    prob = jax.nn.softmax(d_lb_logits.astype(F32), axis=1)
    lbound = jnp.cumsum(prob, axis=1) - prob[:, :1]

    cos, sin = _rope_tables(t_lat, n_ctx)
    bias = _bias_table(c_rpb)
    w_gate_b, w_branch_b, w2_b = (w.astype(BF16) for w in (w_gate, w_branch, w_ffn2))

    for l in range(depth):
        last = l == depth - 1
        m = t_lat if last else t_lat + n_ctx
        h = _norm_mod(xa, norm1_g[l], mod, l, 0, 1, t_lat)
        p = _ws_matmul(h, w_in, l, F32)
        ya = _mix_a(p, a_ws[l], a_bs[l], a_norm_g[l], a_norm_b[l], bw)
        yb = _mix_b(p, b_conv_w[l], b_conv_b[l], b_norm_g[l], b_norm_b[l], bw, t_lat)
        qr, qp, kr, vb = _attn_prep(p, cos, sin, bw)
        yc = _attention(qr, qp, kr, vb, bias, l, bw, t_lat, with_ctx=not last)
        o_bwd = _scan(p, lbound[1, l], bw, t_lat, reverse=True)
        yd = _scan(p, lbound[0, l], bw, t_lat, reverse=False, other=o_bwd, norm_g=d_norm_g[l])
        merged = _gated_merge(h, (ya, yb, yc, yd), w_gate_b, b_gate, w_branch_b, l, m)
        x1 = _ws_matmul_residual(merged, w_out, l, xa, mod, 2, t_lat, m)
        hf = _norm_mod(x1, norm2_g[l], mod, l, 3, 4, t_lat)
        act = _ws_swiglu_up(hf, w_ffn1, w_ffn3, l, m)
        xa = _matmul_residual(act, w2_b, l, x1, mod, 5, t_lat, m)

    return _final_norm(xa[:t_lat], final_g)[None]
```

```python
import functools
import math

import jax
import jax.numpy as jnp
from jax import lax
from jax.experimental import pallas as pl
from jax.experimental.pallas import tpu as pltpu

F32 = jnp.float32
BF16 = jnp.bfloat16

HEAD_DIM = 128
CHUNK = 128
GRID_W = 64
WIN_H = 8
WIN_W = 16
CONV_K = 31
N_BRANCH = 4
ROPE_THETA = 10000.0
EPS = 1e-6
F_MIN = 1e-6
NEG_BIG = -1e30
LOG2_E = 1.4426950408889634

V7X_VMEM_BYTES = 64 * 2**20
VMEM_CAP = V7X_VMEM_BYTES - 6 * 2**20
SUBLANES = 8
LANES = 128
CONV_HALO = 16


def _pick(n, target, quantum):
    best = None
    for t in range(quantum, min(n, target) + 1, quantum):
        if n % t == 0:
            best = t
    assert best is not None, (n, target, quantum)
    return best


def _params(sem, need_bytes):
    limit = int(min(VMEM_CAP, max(need_bytes * 5 // 4, 16 * 2**20)))
    return pltpu.CompilerParams(dimension_semantics=sem, vmem_limit_bytes=limit)


def _sigmoid(x):
    return jax.nn.sigmoid(x)


def _silu(x):
    return x * jax.nn.sigmoid(x)


def _gelu(x):
    return 0.5 * x * (1.0 + lax.erf(x * 0.7071067811865476))


def _row_select(rows_is_ctx, p_ref):
    return jnp.where(rows_is_ctx, p_ref[1:2, :], p_ref[0:1, :])


def _mod_kernel(cond_ref, w_ref, b_ref, o_ref):
    s = _silu(cond_ref[...]).astype(BF16)
    o_ref[...] = jnp.dot(s, w_ref[...].astype(BF16), preferred_element_type=F32) + b_ref[...]


def _modulation(cond, w_ada, b_ada):
    depth, d, n = w_ada.shape
    tn = _pick(n, 512, LANES)
    need = 2 * d * tn * 4 + d * tn * 2
    return pl.pallas_call(
        _mod_kernel,
        grid=(depth, n // tn),
        in_specs=[
            pl.BlockSpec((SUBLANES, d), lambda l, j: (0, 0)),
            pl.BlockSpec((None, d, tn), lambda l, j: (l, 0, j)),
            pl.BlockSpec((None, 1, tn), lambda l, j: (l, 0, j)),
        ],
        out_specs=pl.BlockSpec((None, SUBLANES, tn), lambda l, j: (l, 0, j)),
        out_shape=jax.ShapeDtypeStruct((depth, SUBLANES, n), F32),
        compiler_params=_params(("arbitrary", "arbitrary"), need),
        name="adaln_modulation",
    )(cond, w_ada, b_ada.reshape(depth, 1, n))


NORM_ROWS = 8


def _row_rsqrt_stats(x_ref, rs_ref, tm):
    def stats(r, carry):
        r0 = pl.multiple_of(r * NORM_ROWS, NORM_ROWS)
        x = x_ref[pl.ds(r0, NORM_ROWS), :]
        rs_ref[pl.ds(r0, NORM_ROWS), :] = lax.rsqrt(jnp.mean(x * x, axis=-1, keepdims=True) + EPS)
        return carry

    lax.fori_loop(0, tm // NORM_ROWS, stats, 0, unroll=4)


def _norm_mod_kernel(x_ref, g_ref, sh_ref, sc_ref, o_ref, rs_ref, gs_ref, *, t_lat, tm):
    _row_rsqrt_stats(x_ref, rs_ref, tm)
    which = (pl.program_id(0) * tm >= t_lat).astype(jnp.int32)
    shape = (NORM_ROWS, x_ref.shape[1])
    gs_ref[0:NORM_ROWS, :] = jnp.broadcast_to(g_ref[...] * (1.0 + sc_ref[pl.ds(which, 1), :]), shape)
    gs_ref[NORM_ROWS:, :] = jnp.broadcast_to(sh_ref[pl.ds(which, 1), :], shape)

    def scale(r, carry):
        r0 = pl.multiple_of(r * NORM_ROWS, NORM_ROWS)
        y = x_ref[pl.ds(r0, NORM_ROWS), :] * rs_ref[pl.ds(r0, NORM_ROWS), :]
        o_ref[pl.ds(r0, NORM_ROWS), :] = (y * gs_ref[0:NORM_ROWS, :] + gs_ref[NORM_ROWS:, :]).astype(o_ref.dtype)
        return carry

    lax.fori_loop(0, tm // NORM_ROWS, scale, 0, unroll=8)


def _norm_mod(x, g, mod, layer, shift_blk, scale_blk, t_lat):
    m, d = x.shape
    tm = _pick(math.gcd(t_lat, m), 512, NORM_ROWS)
    return pl.pallas_call(
        functools.partial(_norm_mod_kernel, t_lat=t_lat, tm=tm),
        grid=(m // tm,),
        in_specs=[
            pl.BlockSpec((tm, d), lambda i: (i, 0)),
            pl.BlockSpec((1, d), lambda i: (0, 0)),
            pl.BlockSpec((None, SUBLANES, d), lambda i: (layer, 0, shift_blk)),
            pl.BlockSpec((None, SUBLANES, d), lambda i: (layer, 0, scale_blk)),
        ],
        out_specs=pl.BlockSpec((tm, d), lambda i: (i, 0)),
        out_shape=jax.ShapeDtypeStruct((m, d), BF16),
        scratch_shapes=[pltpu.VMEM((tm, 1), F32), pltpu.VMEM((2 * NORM_ROWS, d), F32)],
        compiler_params=_params(("arbitrary",), 2 * tm * d * 6 + tm * LANES * 4 + 4 * SUBLANES * d * 4),
        name="rmsnorm_modulate",
    )(x, g.reshape(1, d), mod, mod)


def _final_norm_kernel(x_ref, g_ref, o_ref, rs_ref, *, tm):
    _row_rsqrt_stats(x_ref, rs_ref, tm)

    def scale(r, carry):
        r0 = pl.multiple_of(r * NORM_ROWS, NORM_ROWS)
        o_ref[pl.ds(r0, NORM_ROWS), :] = x_ref[pl.ds(r0, NORM_ROWS), :] * rs_ref[pl.ds(r0, NORM_ROWS), :] * g_ref[...]
        return carry

    lax.fori_loop(0, tm // NORM_ROWS, scale, 0, unroll=8)


def _final_norm(x, g):
    m, d = x.shape
    tm = _pick(m, 512, NORM_ROWS)
    return pl.pallas_call(
        functools.partial(_final_norm_kernel, tm=tm),
        grid=(m // tm,),
        in_specs=[pl.BlockSpec((tm, d), lambda i: (i, 0)), pl.BlockSpec((NORM_ROWS, d), lambda i: (0, 0))],
        out_specs=pl.BlockSpec((tm, d), lambda i: (i, 0)),
        out_shape=jax.ShapeDtypeStruct((m, d), F32),
        scratch_shapes=[pltpu.VMEM((tm, 1), F32)],
        compiler_params=_params(("arbitrary",), 2 * tm * d * 8 + tm * LANES * 4),
        name="final_rmsnorm",
    )(x, jnp.broadcast_to(g.reshape(1, d), (NORM_ROWS, d)))


def _ws_mm_kernel(a_ref, w_ref, o_ref, wb):
    @pl.when(pl.program_id(1) == 0)
    def _():
        wb[...] = w_ref[...].astype(BF16)

    o_ref[...] = jnp.dot(a_ref[...], wb[...], preferred_element_type=F32).astype(o_ref.dtype)


def _ws_matmul(a, w, layer, out_dtype):
    m, k = a.shape
    n = w.shape[2]
    tm = _pick(m, 1408, LANES)
    tn = _pick(n, 512, LANES)
    osz = jnp.dtype(out_dtype).itemsize
    need = 2 * (tm * k * 2 + k * tn * 4 + tm * tn * osz) + k * tn * 2 + tm * tn * 4
    return pl.pallas_call(
        _ws_mm_kernel,
        grid=(n // tn, m // tm),
        in_specs=[pl.BlockSpec((tm, k), lambda j, i: (i, 0)), pl.BlockSpec((None, k, tn), lambda j, i: (layer, 0, j))],
        out_specs=pl.BlockSpec((tm, tn), lambda j, i: (i, j)),
        out_shape=jax.ShapeDtypeStruct((m, n), out_dtype),
        scratch_shapes=[pltpu.VMEM((k, tn), BF16)],
        compiler_params=_params(("arbitrary", "arbitrary"), need),
        name="input_projection",
    )(a, w)


def _ws_resid_kernel(a_ref, w_ref, x_ref, gate_ref, o_ref, wb, *, t_lat, tm):
    @pl.when(pl.program_id(1) == 0)
    def _():
        wb[...] = w_ref[...].astype(BF16)

    rows = pl.program_id(1) * tm + lax.broadcasted_iota(jnp.int32, (tm, 1), 0)
    gate = _row_select(rows >= t_lat, gate_ref)
    o_ref[...] = x_ref[...] + gate * jnp.dot(a_ref[...], wb[...], preferred_element_type=F32)


def _ws_matmul_residual(a, w, layer, x, mod, gate_blk, t_lat, m):
    k = a.shape[1]
    n = w.shape[2]
    tm = _pick(m, 1408, LANES)
    tn = _pick(n, 512, LANES)
    nb = n // tn
    need = 2 * (tm * k * 2 + k * tn * 4 + 2 * tm * tn * 4) + k * tn * 2 + tm * tn * 4
    return pl.pallas_call(
        functools.partial(_ws_resid_kernel, t_lat=t_lat, tm=tm),
        grid=(nb, m // tm),
        in_specs=[
            pl.BlockSpec((tm, k), lambda j, i: (i, 0)),
            pl.BlockSpec((None, k, tn), lambda j, i: (layer, 0, j)),
            pl.BlockSpec((tm, tn), lambda j, i: (i, j)),
            pl.BlockSpec((None, SUBLANES, tn), lambda j, i: (layer, 0, gate_blk * nb + j)),
        ],
        out_specs=pl.BlockSpec((tm, tn), lambda j, i: (i, j)),
        out_shape=jax.ShapeDtypeStruct((m, n), F32),
        scratch_shapes=[pltpu.VMEM((k, tn), BF16)],
        compiler_params=_params(("arbitrary", "arbitrary"), need),
        name="output_projection",
    )(a, w, x, mod)


def _ws_swiglu_kernel(a_ref, w1_ref, w3_ref, o_ref, w1b, w3b):
    @pl.when(pl.program_id(1) == 0)
    def _():
        w1b[...] = w1_ref[...].astype(BF16)
        w3b[...] = w3_ref[...].astype(BF16)

    a = a_ref[...]
    u = jnp.dot(a, w1b[...], preferred_element_type=F32)
    g = jnp.dot(a, w3b[...], preferred_element_type=F32)
    o_ref[...] = (_silu(u) * g).astype(o_ref.dtype)


def _ws_swiglu_up(a, w1, w3, layer, m):
    k = a.shape[1]
    n = w1.shape[2]
    tm = _pick(m, 1408, LANES)
    tn = _pick(n, 256, LANES)
    need = 2 * (tm * k * 2 + 2 * k * tn * 4 + tm * tn * 2) + 2 * k * tn * 2 + 3 * tm * tn * 4
    w_spec = pl.BlockSpec((None, k, tn), lambda j, i: (layer, 0, j))
    return pl.pallas_call(
        _ws_swiglu_kernel,
        grid=(n // tn, m // tm),
        in_specs=[pl.BlockSpec((tm, k), lambda j, i: (i, 0)), w_spec, w_spec],
        out_specs=pl.BlockSpec((tm, tn), lambda j, i: (i, j)),
        out_shape=jax.ShapeDtypeStruct((m, n), BF16),
        scratch_shapes=[pltpu.VMEM((k, tn), BF16), pltpu.VMEM((k, tn), BF16)],
        compiler_params=_params(("arbitrary", "arbitrary"), need),
        name="swiglu_up",
    )(a, w1, w3)


def _mm_resid_kernel(a_ref, w_ref, x_ref, gate_ref, o_ref, *, t_lat, tm):
    rows = pl.program_id(0) * tm + lax.broadcasted_iota(jnp.int32, (tm, 1), 0)
    gate = _row_select(rows >= t_lat, gate_ref)
    o_ref[...] = x_ref[...] + gate * jnp.dot(a_ref[...], w_ref[...], preferred_element_type=F32)


def _matmul_residual(a, w, layer, x, mod, gate_blk, t_lat, m):
    k = a.shape[1]
    n = w.shape[2]
    tm = _pick(m, 768, 256)
    tn = _pick(n, 256 if tm > 512 else 512, LANES)
    need = 2 * (tm * k * 2 + k * tn * 2 + 2 * tm * tn * 4) + tm * tn * 4
    nb = n // tn
    return pl.pallas_call(
        functools.partial(_mm_resid_kernel, t_lat=t_lat, tm=tm),
        grid=(m // tm, nb),
        in_specs=[
            pl.BlockSpec((tm, k), lambda i, j: (i, 0)),
            pl.BlockSpec((None, k, tn), lambda i, j: (layer, 0, j)),
            pl.BlockSpec((tm, tn), lambda i, j: (i, j)),
            pl.BlockSpec((None, SUBLANES, tn), lambda i, j: (layer, 0, gate_blk * nb + j)),
        ],
        out_specs=pl.BlockSpec((tm, tn), lambda i, j: (i, j)),
        out_shape=jax.ShapeDtypeStruct((m, n), F32),
        compiler_params=_params(("arbitrary", "arbitrary"), need),
        name="ffn_down",
    )(a, w, x, mod)


def _merge_kernel(h_ref, ya_ref, yb_ref, yc_ref, yd_ref, wg0, wg1, wg2, wg3, bg0, bg1, bg2, bg3, wb_ref, o_ref):
    h = h_ref[...]
    acc = None
    for i, (y_ref, wg, bg) in enumerate(((ya_ref, wg0, bg0), (yb_ref, wg1, bg1), (yc_ref, wg2, bg2), (yd_ref, wg3, bg3))):
        gate = _sigmoid(jnp.dot(h, wg[...], preferred_element_type=F32) + bg[...])
        term = gate * jnp.dot(y_ref[...], wb_ref[i], preferred_element_type=F32)
        acc = term if acc is None else acc + term
    o_ref[...] = acc.astype(o_ref.dtype)


def _gated_merge(h, ys, w_gate, b_gate, w_branch, layer, m):
    k = h.shape[1]
    depth, nbr, bw, n = w_branch.shape
    tm = _pick(m, 768, 256)
    tn = _pick(n, 256, LANES)
    nb = n // tn
    need = 2 * (tm * k * 2 + nbr * tm * bw * 2 + nbr * k * tn * 2 + nbr * bw * tn * 2 + tm * tn * 2) + 4 * tm * tn * 4
    y_spec = pl.BlockSpec((tm, bw), lambda i, j: (i, 0))
    wg_specs = [pl.BlockSpec((None, k, tn), functools.partial(lambda i, j, b: (layer, 0, b * nb + j), b=b))
                for b in range(nbr)]
    bg_specs = [pl.BlockSpec((None, 1, tn), functools.partial(lambda i, j, b: (layer, 0, b * nb + j), b=b))
                for b in range(nbr)]
    bg = b_gate.reshape(depth, 1, -1)
    return pl.pallas_call(
        _merge_kernel,
        grid=(m // tm, nb),
        in_specs=[pl.BlockSpec((tm, k), lambda i, j: (i, 0)), y_spec, y_spec, y_spec, y_spec]
        + wg_specs + bg_specs + [pl.BlockSpec((None, nbr, bw, tn), lambda i, j: (layer, 0, 0, j))],
        out_specs=pl.BlockSpec((tm, tn), lambda i, j: (i, j)),
        out_shape=jax.ShapeDtypeStruct((m, n), BF16),
        compiler_params=_params(("arbitrary", "arbitrary"), need),
        name="gated_merge",
    )(h, *ys, w_gate, w_gate, w_gate, w_gate, bg, bg, bg, bg, w_branch)


def _mix_a_kernel(pu_ref, pv_ref, ws_ref, bias_ref, ng_ref, nb_ref, o_ref, *, nh):
    zu = _gelu(pu_ref[...])
    zv = _gelu(pv_ref[...])
    mu = jnp.mean(zv, axis=-1, keepdims=True)
    zc = zv - mu
    var = jnp.mean(zc * zc, axis=-1, keepdims=True)
    v = (zc * lax.rsqrt(var + EPS) * ng_ref[...] + nb_ref[...]).astype(BF16)
    for g in range(nh):
        sl = slice(g * HEAD_DIM, (g + 1) * HEAD_DIM)
        s = jnp.dot(ws_ref[g], v[:, sl], preferred_element_type=F32) + bias_ref[:, sl]
        o_ref[:, sl] = (zu[:, sl] * s).astype(o_ref.dtype)


def _mix_a(p, ws, bs, ng, nb, bw):
    m = p.shape[0]
    nh = bw // HEAD_DIM
    bias = jnp.repeat(bs.T, HEAD_DIM, axis=1)
    return pl.pallas_call(
        functools.partial(_mix_a_kernel, nh=nh),
        grid=(m // CHUNK,),
        in_specs=[
            pl.BlockSpec((CHUNK, bw), lambda i: (i, 0)),
            pl.BlockSpec((CHUNK, bw), lambda i: (i, 1)),
            pl.BlockSpec((nh, CHUNK, CHUNK), lambda i: (0, 0, 0)),
            pl.BlockSpec((CHUNK, bw), lambda i: (0, 0)),
            pl.BlockSpec((1, bw), lambda i: (0, 0)),
            pl.BlockSpec((1, bw), lambda i: (0, 0)),
        ],
        out_specs=pl.BlockSpec((CHUNK, bw), lambda i: (i, 0)),
        out_shape=jax.ShapeDtypeStruct((m, bw), BF16),
        compiler_params=_params(("arbitrary",), 16 * CHUNK * bw * 4),
        name="mixer_gmlp",
    )(p, p, ws.astype(BF16), bias, ng.reshape(1, bw), nb.reshape(1, bw))


CONV_ROWS = 16


def _mix_b_kernel(ap_ref, ac_ref, an_ref, gp_ref, gc_ref, gn_ref, cw_ref, cb_ref, ng_ref, nb_ref, o_ref, ysh,
                  *, tiles_lat, tiles_tot, tmb):
    i = pl.program_id(0)
    first = jnp.logical_or(i == 0, i == tiles_lat)
    last = jnp.logical_or(i == tiles_lat - 1, i == tiles_tot - 1)
    yp = ap_ref[...] * _sigmoid(gp_ref[...])
    yn = an_ref[...] * _sigmoid(gn_ref[...])
    ysh[0, 0:CONV_HALO, :] = jnp.where(first, 0.0, yp)
    ysh[0, CONV_HALO:CONV_HALO + tmb, :] = ac_ref[...] * _sigmoid(gc_ref[...])
    ysh[0, CONV_HALO + tmb:, :] = jnp.where(last, 0.0, yn)
    nrow = tmb + 2 * CONV_HALO
    for c0 in range(0, ysh.shape[2], LANES):
        base = ysh[0, :, c0:c0 + LANES]
        for b in range(1, SUBLANES):
            ysh[b, :, c0:c0 + LANES] = pltpu.roll(base, nrow - b, 0)
    off = CONV_HALO - CONV_K // 2

    def row_block(rb, carry):
        r0 = pl.multiple_of(rb * CONV_ROWS, CONV_ROWS)
        acc = None
        for j in range(CONV_K):
            b = (off + j) % SUBLANES
            term = cw_ref[j * CONV_ROWS:(j + 1) * CONV_ROWS, :] * ysh[b, pl.ds(r0 + (off + j - b), CONV_ROWS), :]
            acc = term if acc is None else acc + term
        y = acc + cb_ref[...]
        mu = jnp.mean(y, axis=-1, keepdims=True)
        yc = y - mu
        var = jnp.mean(yc * yc, axis=-1, keepdims=True)
        z = yc * lax.rsqrt(var + EPS) * ng_ref[...] + nb_ref[...]
        o_ref[pl.ds(r0, CONV_ROWS), :] = _silu(z).astype(o_ref.dtype)
        return carry

    lax.fori_loop(0, tmb // CONV_ROWS, row_block, 0, unroll=2)


def _mix_b(p, cw, cb, ng, nb, bw, t_lat):
    m = p.shape[0]
    tmb = _pick(math.gcd(t_lat, m - t_lat), 256, CONV_HALO)
    hb = tmb // CONV_HALO
    nhalo = m // CONV_HALO
    a_col, g_col = 2, 3
    prev = lambda c: (lambda i: (jnp.maximum(i * hb - 1, 0), c))
    cur = lambda c: (lambda i: (i, c))
    nxt = lambda c: (lambda i: (jnp.minimum((i + 1) * hb, nhalo - 1), c))
    vec = pl.BlockSpec((CONV_ROWS, bw), lambda i: (0, 0))
    rep = lambda a: jnp.broadcast_to(a.reshape(-1, 1, bw), (a.size // bw, CONV_ROWS, bw)).reshape(-1, bw)
    return pl.pallas_call(
        functools.partial(_mix_b_kernel, tiles_lat=t_lat // tmb, tiles_tot=m // tmb, tmb=tmb),
        grid=(m // tmb,),
        in_specs=[
            pl.BlockSpec((CONV_HALO, bw), prev(a_col)),
            pl.BlockSpec((tmb, bw), cur(a_col)),
            pl.BlockSpec((CONV_HALO, bw), nxt(a_col)),
            pl.BlockSpec((CONV_HALO, bw), prev(g_col)),
            pl.BlockSpec((tmb, bw), cur(g_col)),
            pl.BlockSpec((CONV_HALO, bw), nxt(g_col)),
            pl.BlockSpec((CONV_K * CONV_ROWS, bw), lambda i: (0, 0)),
            vec, vec, vec,
        ],
        out_specs=pl.BlockSpec((tmb, bw), lambda i: (i, 0)),
        out_shape=jax.ShapeDtypeStruct((m, bw), BF16),
        scratch_shapes=[pltpu.VMEM((SUBLANES, tmb + 2 * CONV_HALO, bw), F32)],
        compiler_params=_params(("arbitrary",), 20 * tmb * bw * 4),
        name="mixer_conformer_conv",
    )(p, p, p, p, p, p, rep(cw), rep(cb), rep(ng), rep(nb))


def _rope_tables(t_lat, n_ctx):
    m = HEAD_DIM // 4
    freqs = ROPE_THETA ** (-jnp.arange(m, dtype=F32) / m)
    r = t_lat // GRID_W
    ang_r = jnp.repeat(jnp.arange(r, dtype=F32)[:, None] * freqs[None, :], GRID_W, axis=0)
    ang_c = jnp.tile(jnp.arange(GRID_W, dtype=F32)[:, None] * freqs[None, :], (r, 1))
    cos = jnp.concatenate([jnp.cos(ang_r), jnp.cos(ang_r), jnp.cos(ang_c), jnp.cos(ang_c)], axis=-1)
    sin = jnp.concatenate([-jnp.sin(ang_r), jnp.sin(ang_r), -jnp.sin(ang_c), jnp.sin(ang_c)], axis=-1)
    cos = jnp.concatenate([cos, jnp.ones((n_ctx, HEAD_DIM), F32)], axis=0)
    sin = jnp.concatenate([sin, jnp.zeros((n_ctx, HEAD_DIM), F32)], axis=0)
    return cos, sin


def _bias_kernel(rpb_ref, o_ref):
    n = GRID_W * GRID_W
    col = lax.broadcasted_iota(jnp.int32, (1, n), 1)
    q = col // GRID_W
    k = col % GRID_W
    dc = jnp.clip(k - q, -(WIN_W - 1), WIN_W - 1) + (WIN_W - 1)
    start = jnp.clip(q - WIN_W // 2, 0, GRID_W - WIN_W)
    in_win = jnp.logical_and(k >= start, k < start + WIN_W)
    val = jnp.zeros(o_ref.shape, F32)
    for d in range(2 * WIN_W - 1):
        val = jnp.where(dc == d, rpb_ref[:, d:d + 1], val)
    o_ref[...] = jnp.where(in_win, val, NEG_BIG)


ATTN_QROWS = 4
ATTN_KROWS = ATTN_QROWS + WIN_H
ATTN_SHIFT = WIN_H // 2


def _bias_table(rpb):
    depth, nh, ndr, ndc = rpb.shape
    rows = depth * nh * ndr
    toe = pl.pallas_call(
        _bias_kernel,
        out_shape=jax.ShapeDtypeStruct((rows, GRID_W * GRID_W), F32),
        compiler_params=_params((), 16 * rows * GRID_W * GRID_W),
        name="attention_bias_table",
    )(rpb.reshape(rows, ndc))
    toe = toe.reshape(depth, nh, ndr, GRID_W, GRID_W)
    masked = jnp.full((depth, nh, GRID_W, GRID_W), NEG_BIG, F32)

    def variant(block_off, window_start):
        out = []
        for qr in range(ATTN_QROWS):
            cols = []
            for j in range(ATTN_KROWS):
                visible = window_start(qr) <= j < window_start(qr) + WIN_H
                cols.append(toe[:, :, j - qr - block_off + WIN_H - 1] if visible else masked)
            out.append(jnp.concatenate(cols, axis=-1))
        return jnp.concatenate(out, axis=-2)

    return jnp.stack([variant(0, lambda qr: 0),
                      variant(ATTN_SHIFT, lambda qr: qr),
                      variant(ATTN_KROWS - ATTN_QROWS, lambda qr: ATTN_KROWS - WIN_H)], axis=2)


def _attn_prep_kernel(q_ref, k_ref, v_ref, cos_ref, sin_ref, qr_ref, qp_ref, kr_ref, vb_ref, *, nh):
    cos = cos_ref[...]
    sin = sin_ref[...]
    lane = lax.broadcasted_iota(jnp.int32, cos.shape, 1)
    low = (lane % (HEAD_DIM // 2)) < (HEAD_DIM // 4)

    def rope(x):
        swapped = jnp.where(low, pltpu.roll(x, HEAD_DIM - HEAD_DIM // 4, 1), pltpu.roll(x, HEAD_DIM // 4, 1))
        return x * cos + swapped * sin

    for h in range(nh):
        sl = slice(h * HEAD_DIM, (h + 1) * HEAD_DIM)
        q = q_ref[:, sl]
        qr_ref[:, sl] = rope(q).astype(BF16)
        qp_ref[:, sl] = q.astype(BF16)
        kr_ref[:, sl] = rope(k_ref[:, sl]).astype(BF16)
    vb_ref[...] = v_ref[...].astype(BF16)


def _attn_prep(p, cos, sin, bw):
    m = p.shape[0]
    tm = _pick(m, 256, SUBLANES)
    nh = bw // HEAD_DIM
    out = jax.ShapeDtypeStruct((m, bw), BF16)
    blk = lambda c: pl.BlockSpec((tm, bw), lambda i: (i, c))
    tab = pl.BlockSpec((tm, HEAD_DIM), lambda i: (i, 0))
    o_spec = pl.BlockSpec((tm, bw), lambda i: (i, 0))
    return pl.pallas_call(
        functools.partial(_attn_prep_kernel, nh=nh),
        grid=(m // tm,),
        in_specs=[blk(4), blk(5), blk(6), tab, tab],
        out_specs=[o_spec, o_spec, o_spec, o_spec],
        out_shape=[out, out, out, out],
        compiler_params=_params(("arbitrary",), 2 * tm * bw * (3 * 4 + 4 * 2) + 6 * tm * bw * 4),
        name="attention_rope_prep",
    )(p, p, p, cos, sin)


ATTN_UNROLL = 2


def _softmax_pv(scores, values):
    mx = None
    for s in scores:
        smax = jnp.max(s, axis=-1, keepdims=True)
        mx = smax if mx is None else jnp.maximum(mx, smax)
    den = None
    acc = None
    for s, v in zip(scores, values):
        e = jnp.exp(s - mx)
        esum = jnp.sum(e, axis=-1, keepdims=True)
        den = esum if den is None else den + esum
        pv = jnp.dot(e.astype(BF16), v, preferred_element_type=F32)
        acc = pv if acc is None else acc + pv
    return acc / den


def _dot_nt(a, b):
    return lax.dot_general(a, b, (((1,), (1,)), ((), ())), preferred_element_type=F32)


def _attn_kernel(qr_ref, qp_ref, kr_ref, v_ref, qc_ref, kc_ref, vc_ref, bias_ref, o_ref, *, rows, t_lat, with_ctx):
    scale = HEAD_DIM ** -0.5
    kc = kc_ref[...]
    vc = vc_ref[...]
    nq = ATTN_QROWS * GRID_W
    nk = ATTN_KROWS * GRID_W

    def body(blk, carry):
        r0 = blk * ATTN_QROWS
        s0 = jnp.clip(r0 - ATTN_SHIFT, 0, rows - ATTN_KROWS)
        q0 = pl.multiple_of(r0 * GRID_W, nq)
        k0 = pl.multiple_of(s0 * GRID_W, GRID_W)
        s_lat = _dot_nt(qr_ref[pl.ds(q0, nq), :], kr_ref[pl.ds(k0, nk), :]) * scale
        b = bias_ref[(r0 - s0) // ATTN_SHIFT]
        s_lat = jnp.where(b > 0.5 * NEG_BIG, s_lat + b, NEG_BIG)
        s_ctx = _dot_nt(qp_ref[pl.ds(q0, nq), :], kc) * scale
        o = _softmax_pv((s_lat, s_ctx), (v_ref[pl.ds(k0, nk), :], vc))
        o_ref[pl.ds(q0, nq), :] = o.astype(o_ref.dtype)
        return carry

    lax.fori_loop(0, rows // ATTN_QROWS, body, 0, unroll=ATTN_UNROLL)
    if with_ctx:
        s = _dot_nt(qc_ref[...], kc) * scale
        o_ref[t_lat:, :] = _softmax_pv((s,), (vc,)).astype(o_ref.dtype)


def _attention(qr, qp, kr, vb, bias, layer, bw, t_lat, with_ctx):
    m = qr.shape[0]
    n_ctx = m - t_lat
    nh = bw // HEAD_DIM
    rows = t_lat // GRID_W
    assert rows >= ATTN_KROWS and rows % ATTN_QROWS == 0 and t_lat % n_ctx == 0
    m_out = m if with_ctx else t_lat
    lat = pl.BlockSpec((t_lat, HEAD_DIM), lambda h: (0, h))
    ctx = pl.BlockSpec((n_ctx, HEAD_DIM), lambda h: (t_lat // n_ctx, h))
    nq, nk = ATTN_QROWS * GRID_W, ATTN_KROWS * GRID_W
    need = 2 * (5 * t_lat * HEAD_DIM * 2 + 3 * nq * nk * 4) + 16 * nq * (nk + n_ctx) * 4
    return pl.pallas_call(
        functools.partial(_attn_kernel, rows=rows, t_lat=t_lat, with_ctx=with_ctx),
        grid=(nh,),
        in_specs=[lat, lat, lat, lat, ctx, ctx, ctx,
                  pl.BlockSpec((None, None, 3, nq, nk), lambda h: (layer, h, 0, 0, 0))],
        out_specs=pl.BlockSpec((m_out, HEAD_DIM), lambda h: (0, h)),
        out_shape=jax.ShapeDtypeStruct((m_out, bw), BF16),
        compiler_params=_params(("arbitrary",), need),
        name="neighbourhood_attention",
    )(qr, qp, kr, vb, qp, kr, vb, bias)


def _scan_kernel(*refs, nh, reverse, readout):
    if readout:
        qd_ref, z_ref, i_ref, lb_ref, g_ref, other_ref, ng_ref, o_ref, st_ref = refs
    else:
        qd_ref, z_ref, i_ref, lb_ref, o_ref, st_ref = refs
    c = CHUNK

    @pl.when(pl.program_id(0) == 0)
    def _():
        st_ref[...] = jnp.zeros_like(st_ref)

    ri = lax.broadcasted_iota(jnp.int32, (c, 1), 0)
    ti = lax.broadcasted_iota(jnp.int32, (c, c), 0)
    si = lax.broadcasted_iota(jnp.int32, (c, c), 1)
    diag = ti == si
    widths = [1 << k for k in range(c.bit_length() - 1)]
    masks = [jnp.logical_and(((ti ^ si) >> (w.bit_length() - 1)) == 1,
                             ((ti & w) == 0) if reverse else ((ti & w) != 0)) for w in widths]

    for h in range(nh):
        sl = slice(h * HEAD_DIM, (h + 1) * HEAD_DIM)
        lb = lb_ref[:, sl]
        f = lb + (1.0 - lb) * _sigmoid(z_ref[:, sl])
        l = jnp.log(jnp.maximum(f, F_MIN)) * LOG2_E
        kg = 1.0 - f
        q = _silu(qd_ref[:, sl])
        v = i_ref[:, sl].astype(BF16)

        att = jnp.where(diag, _dot_nt(q.astype(BF16), kg.astype(BF16)), 0.0)
        cf, tot = l, l
        for w, mk in zip(widths, masks):
            if w < SUBLANES:
                odd = (ri & w) != 0
                if reverse:
                    x = jnp.where(odd, kg, q) * jnp.exp2(jnp.where(odd, cf - l, tot - cf + l))
                else:
                    x = jnp.where(odd, q, kg) * jnp.exp2(jnp.where(odd, cf, tot - cf))
                prev_tot = pltpu.roll(tot, w, 0)
                next_tot = pltpu.roll(tot, c - w, 0)
                cf = cf + jnp.where(odd, prev_tot, 0.0)
                tot = tot + jnp.where(odd, prev_tot, next_tot)
            else:
                xs, cfs, tots = [], [], []
                for k in range(c // w):
                    rows = slice(k * w, (k + 1) * w)
                    other = slice((k ^ 1) * w, ((k ^ 1) + 1) * w)
                    is_query = (k % 2 == 0) == reverse
                    if reverse:
                        arg = (tot[rows] - cf[rows] + l[rows]) if is_query else (cf[rows] - l[rows])
                    else:
                        arg = cf[rows] if is_query else (tot[rows] - cf[rows])
                    xs.append((q[rows] if is_query else kg[rows]) * jnp.exp2(arg))
                    cfs.append(cf[rows] + tot[other] if k % 2 else cf[rows])
                    tots.append(tot[rows] + tot[other])
                x = jnp.concatenate(xs, axis=0)
                cf = jnp.concatenate(cfs, axis=0)
                tot = jnp.concatenate(tots, axis=0)
            xb = x.astype(BF16)
            att = jnp.where(mk, _dot_nt(xb, xb), att)
        to_state = (tot - cf + l) if reverse else cf
        to_end = (cf - l) if reverse else (tot - cf)
        q_in = (q * jnp.exp2(to_state)).astype(BF16)
        k_out = (kg * jnp.exp2(to_end)).astype(BF16)

        st = st_ref[h]
        o = _dot_nt(q_in, st.astype(BF16)) + jnp.dot(att.astype(BF16), v, preferred_element_type=F32)
        st_ref[h] = st * jnp.exp2(tot[0:1, :]) + lax.dot_general(
            v, k_out, (((0,), (0,)), ((), ())), preferred_element_type=F32)
        if readout:
            o = o + other_ref[:, sl]
            o = o * lax.rsqrt(jnp.mean(o * o, axis=-1, keepdims=True) + EPS) * ng_ref[...]
            o_ref[:, sl] = (o * _silu(g_ref[:, sl])).astype(o_ref.dtype)
        else:
            o_ref[:, sl] = o


def _scan(p, lb, bw, t_lat, reverse, other=None, norm_g=None):
    m = p.shape[0]
    nh = bw // HEAD_DIM
    n_lat = t_lat // CHUNK
    n_all = m // CHUNK
    n_ctx = n_all - n_lat
    readout = other is not None
    if reverse:
        row_blk = lambda j: jnp.where(j < n_ctx, n_all - 1 - j, n_lat - 1 - (j - n_ctx))
    else:
        row_blk = lambda j: jnp.where(j < n_ctx, n_lat + j, j - n_ctx)
    blk = lambda c: pl.BlockSpec((CHUNK, bw), lambda j: (row_blk(j), c))
    qd_col, zf_col, zb_col, i_col, g_col = 7, 8, 9, 10, 11
    in_specs = [blk(qd_col), blk(zb_col if reverse else zf_col), blk(i_col), pl.BlockSpec((1, bw), lambda j: (0, 0))]
    args = [p, p, p, lb.reshape(1, bw)]
    if readout:
        in_specs += [blk(g_col), blk(0), pl.BlockSpec((1, HEAD_DIM), lambda j: (0, 0))]
        args += [p, other, norm_g.reshape(1, HEAD_DIM)]
    return pl.pallas_call(
        functools.partial(_scan_kernel, nh=nh, reverse=reverse, readout=readout),
        grid=(n_all,),
        in_specs=in_specs,
        out_specs=pl.BlockSpec((CHUNK, bw), lambda j: (row_blk(j), 0)),
        out_shape=jax.ShapeDtypeStruct((m, bw), BF16 if readout else F32),
        scratch_shapes=[pltpu.VMEM((nh, HEAD_DIM, HEAD_DIM), F32)],
        compiler_params=_params(("arbitrary",), 48 * CHUNK * bw * 4),
        name="hgrn2_scan_bwd" if reverse else "hgrn2_scan_fwd",
    )(*args)


def kernel(x, c, ctx, c_ctx, w_ada, b_ada, norm1_g, norm2_g, w_in, a_norm_g, a_norm_b, a_ws, a_bs, b_conv_w,
           b_conv_b, b_norm_g, b_norm_b, c_rpb, d_lb_logits, d_norm_g, w_branch, w_gate, b_gate, w_out, w_ffn1,
           w_ffn3, w_ffn2, final_g):
    batch, t_lat, d = x.shape
    n_ctx = ctx.shape[1]
    depth = w_ada.shape[0]
    bw = d // N_BRANCH
    assert batch == 1 and t_lat % GRID_W == 0 and t_lat % CHUNK == 0 and n_ctx % CHUNK == 0

    xa = jnp.concatenate([x[0], ctx[0]], axis=0)
    cond = jnp.zeros((SUBLANES, d), F32).at[0].set(c[0]).at[1].set(c_ctx)
    mod = _modulation(cond, w_ada, b_ada)

    prob = jax.nn.softmax(d_lb_logits.astype(F32), axis=1)
    lbound = jnp.cumsum(prob, axis=1) - prob[:, :1]

    cos, sin = _rope_tables(t_lat, n_ctx)
    bias = _bias_table(c_rpb)
    w_gate_b, w_branch_b, w2_b = (w.astype(BF16) for w in (w_gate, w_branch, w_ffn2))

    for l in range(depth):
        last = l == depth - 1
        m = t_lat if last else t_lat + n_ctx
        h = _norm_mod(xa, norm1_g[l], mod, l, 0, 1, t_lat)
        p = _ws_matmul(h, w_in, l, F32)
        ya = _mix_a(p, a_ws[l], a_bs[l], a_norm_g[l], a_norm_b[l], bw)
        yb = _mix_b(p, b_conv_w[l], b_conv_b[l], b_norm_g[l], b_norm_b[l], bw, t_lat)
        qr, qp, kr, vb = _attn_prep(p, cos, sin, bw)
        yc = _attention(qr, qp, kr, vb, bias, l, bw, t_lat, with_ctx=not last)
        o_bwd = _scan(p, lbound[1, l], bw, t_lat, reverse=True)
        yd = _scan(p, lbound[0, l], bw, t_lat, reverse=False, other=o_bwd, norm_g=d_norm_g[l])
        merged = _gated_merge(h, (ya, yb, yc, yd), w_gate_b, b_gate, w_branch_b, l, m)
        x1 = _ws_matmul_residual(merged, w_out, l, xa, mod, 2, t_lat, m)
        hf = _norm_mod(x1, norm2_g[l], mod, l, 3, 4, t_lat)
        act = _ws_swiglu_up(hf, w_ffn1, w_ffn3, l, m)
        xa = _matmul_residual(act, w2_b, l, x1, mod, 5, t_lat, m)

    return _final_norm(xa[:t_lat], final_g)[None]
```

```python
import functools
import math

import jax
import jax.numpy as jnp
from jax import lax
from jax.experimental import pallas as pl
from jax.experimental.pallas import tpu as pltpu

F32 = jnp.float32
BF16 = jnp.bfloat16

HEAD_DIM = 128
CHUNK = 128
GRID_W = 64
WIN_H = 8
WIN_W = 16
CONV_K = 31
N_BRANCH = 4
ROPE_THETA = 10000.0
EPS = 1e-6
F_MIN = 1e-6
NEG_BIG = -1e30
LOG2_E = 1.4426950408889634

V7X_VMEM_BYTES = 64 * 2**20
VMEM_CAP = V7X_VMEM_BYTES - 6 * 2**20
SUBLANES = 8
LANES = 128
CONV_HALO = 16


def _pick(n, target, quantum):
    best = None
    for t in range(quantum, min(n, target) + 1, quantum):
        if n % t == 0:
            best = t
    assert best is not None, (n, target, quantum)
    return best


def _params(sem, need_bytes):
    limit = int(min(VMEM_CAP, max(need_bytes * 5 // 4, 16 * 2**20)))
    return pltpu.CompilerParams(dimension_semantics=sem, vmem_limit_bytes=limit)


def _sigmoid(x):
    return jax.nn.sigmoid(x)


def _silu(x):
    return x * jax.nn.sigmoid(x)


def _gelu(x):
    return 0.5 * x * (1.0 + lax.erf(x * 0.7071067811865476))


def _row_select(rows_is_ctx, p_ref):
    return jnp.where(rows_is_ctx, p_ref[1:2, :], p_ref[0:1, :])


def _mod_kernel(cond_ref, w_ref, b_ref, o_ref):
    s = _silu(cond_ref[...]).astype(BF16)
    o_ref[...] = jnp.dot(s, w_ref[...].astype(BF16), preferred_element_type=F32) + b_ref[...]


def _modulation(cond, w_ada, b_ada):
    depth, d, n = w_ada.shape
    tn = _pick(n, 512, LANES)
    need = 2 * d * tn * 4 + d * tn * 2
    return pl.pallas_call(
        _mod_kernel,
        grid=(depth, n // tn),
        in_specs=[
            pl.BlockSpec((SUBLANES, d), lambda l, j: (0, 0)),
            pl.BlockSpec((None, d, tn), lambda l, j: (l, 0, j)),
            pl.BlockSpec((None, 1, tn), lambda l, j: (l, 0, j)),
        ],
        out_specs=pl.BlockSpec((None, SUBLANES, tn), lambda l, j: (l, 0, j)),
        out_shape=jax.ShapeDtypeStruct((depth, SUBLANES, n), F32),
        compiler_params=_params(("arbitrary", "arbitrary"), need),
        name="adaln_modulation",
    )(cond, w_ada, b_ada.reshape(depth, 1, n))


NORM_ROWS = 8


def _row_rsqrt_stats(x_ref, rs_ref, tm):
    def stats(r, carry):
        r0 = pl.multiple_of(r * NORM_ROWS, NORM_ROWS)
        x = x_ref[pl.ds(r0, NORM_ROWS), :]
        rs_ref[pl.ds(r0, NORM_ROWS), :] = lax.rsqrt(jnp.mean(x * x, axis=-1, keepdims=True) + EPS)
        return carry

    lax.fori_loop(0, tm // NORM_ROWS, stats, 0, unroll=4)


def _norm_mod_kernel(x_ref, g_ref, sh_ref, sc_ref, o_ref, rs_ref, gs_ref, *, t_lat, tm):
    _row_rsqrt_stats(x_ref, rs_ref, tm)
    which = (pl.program_id(0) * tm >= t_lat).astype(jnp.int32)
    shape = (NORM_ROWS, x_ref.shape[1])
    gs_ref[0:NORM_ROWS, :] = jnp.broadcast_to(g_ref[...] * (1.0 + sc_ref[pl.ds(which, 1), :]), shape)
    gs_ref[NORM_ROWS:, :] = jnp.broadcast_to(sh_ref[pl.ds(which, 1), :], shape)

    def scale(r, carry):
        r0 = pl.multiple_of(r * NORM_ROWS, NORM_ROWS)
        y = x_ref[pl.ds(r0, NORM_ROWS), :] * rs_ref[pl.ds(r0, NORM_ROWS), :]
        o_ref[pl.ds(r0, NORM_ROWS), :] = (y * gs_ref[0:NORM_ROWS, :] + gs_ref[NORM_ROWS:, :]).astype(o_ref.dtype)
        return carry

    lax.fori_loop(0, tm // NORM_ROWS, scale, 0, unroll=8)


def _norm_mod(x, g, mod, layer, shift_blk, scale_blk, t_lat):
    m, d = x.shape
    tm = _pick(math.gcd(t_lat, m), 512, NORM_ROWS)
    return pl.pallas_call(
        functools.partial(_norm_mod_kernel, t_lat=t_lat, tm=tm),
        grid=(m // tm,),
        in_specs=[
            pl.BlockSpec((tm, d), lambda i: (i, 0)),
            pl.BlockSpec((1, d), lambda i: (0, 0)),
            pl.BlockSpec((None, SUBLANES, d), lambda i: (layer, 0, shift_blk)),
            pl.BlockSpec((None, SUBLANES, d), lambda i: (layer, 0, scale_blk)),
        ],
        out_specs=pl.BlockSpec((tm, d), lambda i: (i, 0)),
        out_shape=jax.ShapeDtypeStruct((m, d), BF16),
        scratch_shapes=[pltpu.VMEM((tm, 1), F32), pltpu.VMEM((2 * NORM_ROWS, d), F32)],
        compiler_params=_params(("arbitrary",), 2 * tm * d * 6 + tm * LANES * 4 + 4 * SUBLANES * d * 4),
        name="rmsnorm_modulate",
    )(x, g.reshape(1, d), mod, mod)


def _final_norm_kernel(x_ref, g_ref, o_ref, rs_ref, *, tm):
    _row_rsqrt_stats(x_ref, rs_ref, tm)

    def scale(r, carry):
        r0 = pl.multiple_of(r * NORM_ROWS, NORM_ROWS)
        o_ref[pl.ds(r0, NORM_ROWS), :] = x_ref[pl.ds(r0, NORM_ROWS), :] * rs_ref[pl.ds(r0, NORM_ROWS), :] * g_ref[...]
        return carry

    lax.fori_loop(0, tm // NORM_ROWS, scale, 0, unroll=8)


def _final_norm(x, g):
    m, d = x.shape
    tm = _pick(m, 512, NORM_ROWS)
    return pl.pallas_call(
        functools.partial(_final_norm_kernel, tm=tm),
        grid=(m // tm,),
        in_specs=[pl.BlockSpec((tm, d), lambda i: (i, 0)), pl.BlockSpec((NORM_ROWS, d), lambda i: (0, 0))],
        out_specs=pl.BlockSpec((tm, d), lambda i: (i, 0)),
        out_shape=jax.ShapeDtypeStruct((m, d), F32),
        scratch_shapes=[pltpu.VMEM((tm, 1), F32)],
        compiler_params=_params(("arbitrary",), 2 * tm * d * 8 + tm * LANES * 4),
        name="final_rmsnorm",
    )(x, jnp.broadcast_to(g.reshape(1, d), (NORM_ROWS, d)))


def _ws_mm_kernel(a_ref, w_ref, o_ref, wb):
    @pl.when(pl.program_id(1) == 0)
    def _():
        wb[...] = w_ref[...].astype(BF16)

    o_ref[...] = jnp.dot(a_ref[...], wb[...], preferred_element_type=F32).astype(o_ref.dtype)


def _ws_matmul(a, w, layer, out_dtype):
    m, k = a.shape
    n = w.shape[2]
    tm = _pick(m, 1408, LANES)
    tn = _pick(n, 512, LANES)
    osz = jnp.dtype(out_dtype).itemsize
    need = 2 * (tm * k * 2 + k * tn * 4 + tm * tn * osz) + k * tn * 2 + tm * tn * 4
    return pl.pallas_call(
        _ws_mm_kernel,
        grid=(n // tn, m // tm),
        in_specs=[pl.BlockSpec((tm, k), lambda j, i: (i, 0)), pl.BlockSpec((None, k, tn), lambda j, i: (layer, 0, j))],
        out_specs=pl.BlockSpec((tm, tn), lambda j, i: (i, j)),
        out_shape=jax.ShapeDtypeStruct((m, n), out_dtype),
        scratch_shapes=[pltpu.VMEM((k, tn), BF16)],
        compiler_params=_params(("arbitrary", "arbitrary"), need),
        name="input_projection",
    )(a, w)


def _ws_resid_kernel(a_ref, w_ref, x_ref, gate_ref, o_ref, wb, *, t_lat, tm):
    @pl.when(pl.program_id(1) == 0)
    def _():
        wb[...] = w_ref[...].astype(BF16)

    rows = pl.program_id(1) * tm + lax.broadcasted_iota(jnp.int32, (tm, 1), 0)
    gate = _row_select(rows >= t_lat, gate_ref)
    o_ref[...] = x_ref[...] + gate * jnp.dot(a_ref[...], wb[...], preferred_element_type=F32)


def _ws_matmul_residual(a, w, layer, x, mod, gate_blk, t_lat, m):
    k = a.shape[1]
    n = w.shape[2]
    tm = _pick(m, 1408, LANES)
    tn = _pick(n, 512, LANES)
    nb = n // tn
    need = 2 * (tm * k * 2 + k * tn * 4 + 2 * tm * tn * 4) + k * tn * 2 + tm * tn * 4
    return pl.pallas_call(
        functools.partial(_ws_resid_kernel, t_lat=t_lat, tm=tm),
        grid=(nb, m // tm),
        in_specs=[
            pl.BlockSpec((tm, k), lambda j, i: (i, 0)),
            pl.BlockSpec((None, k, tn), lambda j, i: (layer, 0, j)),
            pl.BlockSpec((tm, tn), lambda j, i: (i, j)),
            pl.BlockSpec((None, SUBLANES, tn), lambda j, i: (layer, 0, gate_blk * nb + j)),
        ],
        out_specs=pl.BlockSpec((tm, tn), lambda j, i: (i, j)),
        out_shape=jax.ShapeDtypeStruct((m, n), F32),
        scratch_shapes=[pltpu.VMEM((k, tn), BF16)],
        compiler_params=_params(("arbitrary", "arbitrary"), need),
        name="output_projection",
    )(a, w, x, mod)


def _ws_swiglu_kernel(a_ref, w1_ref, w3_ref, o_ref, w1b, w3b):
    @pl.when(pl.program_id(1) == 0)
    def _():
        w1b[...] = w1_ref[...].astype(BF16)
        w3b[...] = w3_ref[...].astype(BF16)

    a = a_ref[...]
    u = jnp.dot(a, w1b[...], preferred_element_type=F32)
    g = jnp.dot(a, w3b[...], preferred_element_type=F32)
    o_ref[...] = (_silu(u) * g).astype(o_ref.dtype)


def _ws_swiglu_up(a, w1, w3, layer, m):
    k = a.shape[1]
    n = w1.shape[2]
    tm = _pick(m, 1408, LANES)
    tn = _pick(n, 256, LANES)
    need = 2 * (tm * k * 2 + 2 * k * tn * 4 + tm * tn * 2) + 2 * k * tn * 2 + 3 * tm * tn * 4
    w_spec = pl.BlockSpec((None, k, tn), lambda j, i: (layer, 0, j))
    return pl.pallas_call(
        _ws_swiglu_kernel,
        grid=(n // tn, m // tm),
        in_specs=[pl.BlockSpec((tm, k), lambda j, i: (i, 0)), w_spec, w_spec],
        out_specs=pl.BlockSpec((tm, tn), lambda j, i: (i, j)),
        out_shape=jax.ShapeDtypeStruct((m, n), BF16),
        scratch_shapes=[pltpu.VMEM((k, tn), BF16), pltpu.VMEM((k, tn), BF16)],
        compiler_params=_params(("arbitrary", "arbitrary"), need),
        name="swiglu_up",
    )(a, w1, w3)


def _mm_resid_kernel(a_ref, w_ref, x_ref, gate_ref, o_ref, *, t_lat, tm):
    rows = pl.program_id(0) * tm + lax.broadcasted_iota(jnp.int32, (tm, 1), 0)
    gate = _row_select(rows >= t_lat, gate_ref)
    o_ref[...] = x_ref[...] + gate * jnp.dot(a_ref[...], w_ref[...], preferred_element_type=F32)


def _matmul_residual(a, w, layer, x, mod, gate_blk, t_lat, m):
    k = a.shape[1]
    n = w.shape[2]
    tm = _pick(m, 768, 256)
    tn = _pick(n, 256 if tm > 512 else 512, LANES)
    need = 2 * (tm * k * 2 + k * tn * 2 + 2 * tm * tn * 4) + tm * tn * 4
    nb = n // tn
    return pl.pallas_call(
        functools.partial(_mm_resid_kernel, t_lat=t_lat, tm=tm),
        grid=(m // tm, nb),
        in_specs=[
            pl.BlockSpec((tm, k), lambda i, j: (i, 0)),
            pl.BlockSpec((None, k, tn), lambda i, j: (layer, 0, j)),
            pl.BlockSpec((tm, tn), lambda i, j: (i, j)),
            pl.BlockSpec((None, SUBLANES, tn), lambda i, j: (layer, 0, gate_blk * nb + j)),
        ],
        out_specs=pl.BlockSpec((tm, tn), lambda i, j: (i, j)),
        out_shape=jax.ShapeDtypeStruct((m, n), F32),
        compiler_params=_params(("arbitrary", "arbitrary"), need),
        name="ffn_down",
    )(a, w, x, mod)


def _merge_kernel(h_ref, ya_ref, yb_ref, yc_ref, yd_ref, wg0, wg1, wg2, wg3, bg0, bg1, bg2, bg3, wb_ref, o_ref):
    h = h_ref[...]
    acc = None
    for i, (y_ref, wg, bg) in enumerate(((ya_ref, wg0, bg0), (yb_ref, wg1, bg1), (yc_ref, wg2, bg2), (yd_ref, wg3, bg3))):
        gate = _sigmoid(jnp.dot(h, wg[...], preferred_element_type=F32) + bg[...])
        term = gate * jnp.dot(y_ref[...], wb_ref[i], preferred_element_type=F32)
        acc = term if acc is None else acc + term
    o_ref[...] = acc.astype(o_ref.dtype)


def _gated_merge(h, ys, w_gate, b_gate, w_branch, layer, m):
    k = h.shape[1]
    depth, nbr, bw, n = w_branch.shape
    tm = _pick(m, 768, 256)
    tn = _pick(n, 256, LANES)
    nb = n // tn
    need = 2 * (tm * k * 2 + nbr * tm * bw * 2 + nbr * k * tn * 2 + nbr * bw * tn * 2 + tm * tn * 2) + 4 * tm * tn * 4
    y_spec = pl.BlockSpec((tm, bw), lambda i, j: (i, 0))
    wg_specs = [pl.BlockSpec((None, k, tn), functools.partial(lambda i, j, b: (layer, 0, b * nb + j), b=b))
                for b in range(nbr)]
    bg_specs = [pl.BlockSpec((None, 1, tn), functools.partial(lambda i, j, b: (layer, 0, b * nb + j), b=b))
                for b in range(nbr)]
    bg = b_gate.reshape(depth, 1, -1)
    return pl.pallas_call(
        _merge_kernel,
        grid=(m // tm, nb),
        in_specs=[pl.BlockSpec((tm, k), lambda i, j: (i, 0)), y_spec, y_spec, y_spec, y_spec]
        + wg_specs + bg_specs + [pl.BlockSpec((None, nbr, bw, tn), lambda i, j: (layer, 0, 0, j))],
        out_specs=pl.BlockSpec((tm, tn), lambda i, j: (i, j)),
        out_shape=jax.ShapeDtypeStruct((m, n), BF16),
        compiler_params=_params(("arbitrary", "arbitrary"), need),
        name="gated_merge",
    )(h, *ys, w_gate, w_gate, w_gate, w_gate, bg, bg, bg, bg, w_branch)


def _mix_a_kernel(pu_ref, pv_ref, ws_ref, bias_ref, ng_ref, nb_ref, o_ref, *, nh):
    zu = _gelu(pu_ref[...])
    zv = _gelu(pv_ref[...])
    mu = jnp.mean(zv, axis=-1, keepdims=True)
    zc = zv - mu
    var = jnp.mean(zc * zc, axis=-1, keepdims=True)
    v = (zc * lax.rsqrt(var + EPS) * ng_ref[...] + nb_ref[...]).astype(BF16)
    for g in range(nh):
        sl = slice(g * HEAD_DIM, (g + 1) * HEAD_DIM)
        s = jnp.dot(ws_ref[g], v[:, sl], preferred_element_type=F32) + bias_ref[:, sl]
        o_ref[:, sl] = (zu[:, sl] * s).astype(o_ref.dtype)


def _mix_a(p, ws, bs, ng, nb, bw):
    m = p.shape[0]
    nh = bw // HEAD_DIM
    bias = jnp.repeat(bs.T, HEAD_DIM, axis=1)
    return pl.pallas_call(
        functools.partial(_mix_a_kernel, nh=nh),
        grid=(m // CHUNK,),
        in_specs=[
            pl.BlockSpec((CHUNK, bw), lambda i: (i, 0)),
            pl.BlockSpec((CHUNK, bw), lambda i: (i, 1)),
            pl.BlockSpec((nh, CHUNK, CHUNK), lambda i: (0, 0, 0)),
            pl.BlockSpec((CHUNK, bw), lambda i: (0, 0)),
            pl.BlockSpec((1, bw), lambda i: (0, 0)),
            pl.BlockSpec((1, bw), lambda i: (0, 0)),
        ],
        out_specs=pl.BlockSpec((CHUNK, bw), lambda i: (i, 0)),
        out_shape=jax.ShapeDtypeStruct((m, bw), BF16),
        compiler_params=_params(("arbitrary",), 16 * CHUNK * bw * 4),
        name="mixer_gmlp",
    )(p, p, ws.astype(BF16), bias, ng.reshape(1, bw), nb.reshape(1, bw))


CONV_ROWS = 32


def _mix_b_kernel(ap_ref, ac_ref, an_ref, gp_ref, gc_ref, gn_ref, cw_ref, cb_ref, ng_ref, nb_ref, o_ref, ysh, yconv,
                  *, tiles_lat, tiles_tot, tmb):
    i = pl.program_id(0)
    first = jnp.logical_or(i == 0, i == tiles_lat)
    last = jnp.logical_or(i == tiles_lat - 1, i == tiles_tot - 1)
    yp = ap_ref[...] * _sigmoid(gp_ref[...])
    yn = an_ref[...] * _sigmoid(gn_ref[...])
    ysh[0, 0:CONV_HALO, :] = jnp.where(first, 0.0, yp)
    ysh[0, CONV_HALO:CONV_HALO + tmb, :] = ac_ref[...] * _sigmoid(gc_ref[...])
    ysh[0, CONV_HALO + tmb:, :] = jnp.where(last, 0.0, yn)
    nrow = tmb + 2 * CONV_HALO
    for c0 in range(0, ysh.shape[2], LANES):
        base = ysh[0, :, c0:c0 + LANES]
        for b in range(1, SUBLANES):
            ysh[b, :, c0:c0 + LANES] = pltpu.roll(base, nrow - b, 0)
    off = CONV_HALO - CONV_K // 2

    bw = ysh.shape[2]
    groups = CONV_ROWS // SUBLANES

    def conv_block(rb, carry):
        r0 = pl.multiple_of(rb * CONV_ROWS, CONV_ROWS)
        acc = None
        for j in range(CONV_K):
            b = (off + j) % SUBLANES
            taps = ysh[b, pl.ds(r0 + (off + j - b), CONV_ROWS), :].reshape(groups, SUBLANES, bw)
            term = taps * cw_ref[j * SUBLANES:(j + 1) * SUBLANES, :][None]
            acc = term if acc is None else acc + term
        yconv[pl.ds(r0, CONV_ROWS), :] = acc.reshape(CONV_ROWS, bw)
        return carry

    lax.fori_loop(0, tmb // CONV_ROWS, conv_block, 0)

    def norm_group(rg, carry):
        r0 = pl.multiple_of(rg * SUBLANES, SUBLANES)
        y = yconv[pl.ds(r0, SUBLANES), :] + cb_ref[...]
        mu = jnp.mean(y, axis=-1, keepdims=True)
        yc = y - mu
        var = jnp.mean(yc * yc, axis=-1, keepdims=True)
        z = yc * lax.rsqrt(var + EPS) * ng_ref[...] + nb_ref[...]
        o_ref[pl.ds(r0, SUBLANES), :] = _silu(z).astype(o_ref.dtype)
        return carry

    lax.fori_loop(0, tmb // SUBLANES, norm_group, 0, unroll=8)


def _mix_b(p, cw, cb, ng, nb, bw, t_lat):
    m = p.shape[0]
    tmb = _pick(math.gcd(t_lat, m - t_lat), 256, CONV_HALO)
    hb = tmb // CONV_HALO
    nhalo = m // CONV_HALO
    a_col, g_col = 2, 3
    prev = lambda c: (lambda i: (jnp.maximum(i * hb - 1, 0), c))
    cur = lambda c: (lambda i: (i, c))
    nxt = lambda c: (lambda i: (jnp.minimum((i + 1) * hb, nhalo - 1), c))
    vec = pl.BlockSpec((SUBLANES, bw), lambda i: (0, 0))
    rep = lambda a: jnp.broadcast_to(a.reshape(-1, 1, bw), (a.size // bw, SUBLANES, bw)).reshape(-1, bw)
    return pl.pallas_call(
        functools.partial(_mix_b_kernel, tiles_lat=t_lat // tmb, tiles_tot=m // tmb, tmb=tmb),
        grid=(m // tmb,),
        in_specs=[
            pl.BlockSpec((CONV_HALO, bw), prev(a_col)),
            pl.BlockSpec((tmb, bw), cur(a_col)),
            pl.BlockSpec((CONV_HALO, bw), nxt(a_col)),
            pl.BlockSpec((CONV_HALO, bw), prev(g_col)),
            pl.BlockSpec((tmb, bw), cur(g_col)),
            pl.BlockSpec((CONV_HALO, bw), nxt(g_col)),
            pl.BlockSpec((CONV_K * SUBLANES, bw), lambda i: (0, 0)),
            vec, vec, vec,
        ],
        out_specs=pl.BlockSpec((tmb, bw), lambda i: (i, 0)),
        out_shape=jax.ShapeDtypeStruct((m, bw), BF16),
        scratch_shapes=[pltpu.VMEM((SUBLANES, tmb + 2 * CONV_HALO, bw), F32), pltpu.VMEM((tmb, bw), F32)],
        compiler_params=_params(("arbitrary",), 20 * tmb * bw * 4),
        name="mixer_conformer_conv",
    )(p, p, p, p, p, p, rep(cw), rep(cb), rep(ng), rep(nb))


def _rope_tables(t_lat):
    m = HEAD_DIM // 4
    freqs = ROPE_THETA ** (-jnp.arange(m, dtype=F32) / m)
    r = t_lat // GRID_W
    ang_r = jnp.repeat(jnp.arange(r, dtype=F32)[:, None] * freqs[None, :], GRID_W, axis=0)
    ang_c = jnp.tile(jnp.arange(GRID_W, dtype=F32)[:, None] * freqs[None, :], (r, 1))
    cos = jnp.concatenate([jnp.cos(ang_r), jnp.cos(ang_r), jnp.cos(ang_c), jnp.cos(ang_c)], axis=-1)
    sin = jnp.concatenate([-jnp.sin(ang_r), jnp.sin(ang_r), -jnp.sin(ang_c), jnp.sin(ang_c)], axis=-1)
    return cos, sin


def _rope(x, cos, sin):
    lane = lax.broadcasted_iota(jnp.int32, x.shape, 1)
    low = (lane % (HEAD_DIM // 2)) < (HEAD_DIM // 4)
    swapped = jnp.where(low, pltpu.roll(x, HEAD_DIM - HEAD_DIM // 4, 1), pltpu.roll(x, HEAD_DIM // 4, 1))
    return x * cos + swapped * sin


def _bias_kernel(rpb_ref, o_ref):
    n = GRID_W * GRID_W
    col = lax.broadcasted_iota(jnp.int32, (1, n), 1)
    q = col // GRID_W
    k = col % GRID_W
    dc = jnp.clip(k - q, -(WIN_W - 1), WIN_W - 1) + (WIN_W - 1)
    start = jnp.clip(q - WIN_W // 2, 0, GRID_W - WIN_W)
    in_win = jnp.logical_and(k >= start, k < start + WIN_W)
    val = jnp.zeros(o_ref.shape, F32)
    for d in range(2 * WIN_W - 1):
        val = jnp.where(dc == d, rpb_ref[:, d:d + 1], val)
    o_ref[...] = jnp.where(in_win, val, NEG_BIG)


ATTN_QROWS = 4
ATTN_KROWS = ATTN_QROWS + WIN_H
ATTN_SHIFT = WIN_H // 2


def _bias_table(rpb):
    depth, nh, ndr, ndc = rpb.shape
    rows = depth * nh * ndr
    toe = pl.pallas_call(
        _bias_kernel,
        out_shape=jax.ShapeDtypeStruct((rows, GRID_W * GRID_W), F32),
        compiler_params=_params((), 16 * rows * GRID_W * GRID_W),
        name="attention_bias_table",
    )(rpb.reshape(rows, ndc))
    toe = toe.reshape(depth, nh, ndr, GRID_W, GRID_W)
    masked = jnp.full((depth, nh, GRID_W, GRID_W), NEG_BIG, F32)

    def variant(block_off, window_start):
        out = []
        for qr in range(ATTN_QROWS):
            cols = []
            for j in range(ATTN_KROWS):
                visible = window_start(qr) <= j < window_start(qr) + WIN_H
                cols.append(toe[:, :, j - qr - block_off + WIN_H - 1] if visible else masked)
            out.append(jnp.concatenate(cols, axis=-1))
        return jnp.concatenate(out, axis=-2)

    return jnp.stack([variant(0, lambda qr: 0),
                      variant(ATTN_SHIFT, lambda qr: qr),
                      variant(ATTN_KROWS - ATTN_QROWS, lambda qr: ATTN_KROWS - WIN_H)], axis=2)


ATTN_UNROLL = 2
ATTN_PREP_ROWS = 256


def _softmax_pv(scores, values):
    mx = None
    for s in scores:
        smax = jnp.max(s, axis=-1, keepdims=True)
        mx = smax if mx is None else jnp.maximum(mx, smax)
    den = None
    acc = None
    for s, v in zip(scores, values):
        e = jnp.exp(s - mx)
        esum = jnp.sum(e, axis=-1, keepdims=True)
        den = esum if den is None else den + esum
        pv = jnp.dot(e.astype(BF16), v, preferred_element_type=F32)
        acc = pv if acc is None else acc + pv
    return acc / den


def _dot_nt(a, b):
    return lax.dot_general(a, b, (((1,), (1,)), ((), ())), preferred_element_type=F32)


def _attn_kernel(q_ref, k_ref, v_ref, qc_ref, kc_ref, vc_ref, cos_ref, sin_ref, bias_ref, o_ref, kr_ref, vb_ref,
                 *, rows, t_lat, with_ctx):
    scale = HEAD_DIM ** -0.5
    kc = kc_ref[...].astype(BF16)
    vc = vc_ref[...].astype(BF16)
    nq = ATTN_QROWS * GRID_W
    nk = ATTN_KROWS * GRID_W

    def prep(c, carry):
        rs = pl.ds(pl.multiple_of(c * ATTN_PREP_ROWS, ATTN_PREP_ROWS), ATTN_PREP_ROWS)
        kr_ref[rs, :] = _rope(k_ref[rs, :], cos_ref[rs, :], sin_ref[rs, :]).astype(BF16)
        vb_ref[rs, :] = v_ref[rs, :].astype(BF16)
        return carry

    lax.fori_loop(0, t_lat // ATTN_PREP_ROWS, prep, 0)

    def body(blk, carry):
        r0 = blk * ATTN_QROWS
        s0 = jnp.clip(r0 - ATTN_SHIFT, 0, rows - ATTN_KROWS)
        qs = pl.ds(pl.multiple_of(r0 * GRID_W, nq), nq)
        ks = pl.ds(pl.multiple_of(s0 * GRID_W, GRID_W), nk)
        q = q_ref[qs, :]
        s_lat = _dot_nt(_rope(q, cos_ref[qs, :], sin_ref[qs, :]).astype(BF16), kr_ref[ks, :]) * scale
        b = bias_ref[(r0 - s0) // ATTN_SHIFT]
        s_lat = jnp.where(b > 0.5 * NEG_BIG, s_lat + b, NEG_BIG)
        s_ctx = _dot_nt(q.astype(BF16), kc) * scale
        o = _softmax_pv((s_lat, s_ctx), (vb_ref[ks, :], vc))
        o_ref[qs, :] = o.astype(o_ref.dtype)
        return carry

    lax.fori_loop(0, rows // ATTN_QROWS, body, 0, unroll=ATTN_UNROLL)
    if with_ctx:
        s = _dot_nt(qc_ref[...].astype(BF16), kc) * scale
        o_ref[t_lat:, :] = _softmax_pv((s,), (vc,)).astype(o_ref.dtype)


def _attention(p, cos, sin, bias, layer, bw, t_lat, with_ctx):
    m = p.shape[0]
    n_ctx = m - t_lat
    nh = bw // HEAD_DIM
    rows = t_lat // GRID_W
    assert rows >= ATTN_KROWS and rows % ATTN_QROWS == 0 and t_lat % n_ctx == 0 and t_lat % ATTN_PREP_ROWS == 0
    m_out = m if with_ctx else t_lat
    q_col, k_col, v_col = 4 * nh, 5 * nh, 6 * nh
    lat = lambda c: pl.BlockSpec((t_lat, HEAD_DIM), lambda h: (0, c + h))
    ctx = lambda c: pl.BlockSpec((n_ctx, HEAD_DIM), lambda h: (t_lat // n_ctx, c + h))
    table = pl.BlockSpec((t_lat, HEAD_DIM), lambda h: (0, 0), pipeline_mode=pl.Buffered(1))
    nq, nk = ATTN_QROWS * GRID_W, ATTN_KROWS * GRID_W
    need = (2 * 3 * t_lat * HEAD_DIM * 4 + 2 * t_lat * HEAD_DIM * 4 + 2 * t_lat * HEAD_DIM * 2
            + 2 * m_out * HEAD_DIM * 2 + 2 * 3 * nq * nk * 4 + 8 * nq * (nk + n_ctx) * 4)
    return pl.pallas_call(
        functools.partial(_attn_kernel, rows=rows, t_lat=t_lat, with_ctx=with_ctx),
        grid=(nh,),
        in_specs=[lat(q_col), lat(k_col), lat(v_col), ctx(q_col), ctx(k_col), ctx(v_col), table, table,
                  pl.BlockSpec((None, None, 3, nq, nk), lambda h: (layer, h, 0, 0, 0))],
        out_specs=pl.BlockSpec((m_out, HEAD_DIM), lambda h: (0, h)),
        out_shape=jax.ShapeDtypeStruct((m_out, bw), BF16),
        scratch_shapes=[pltpu.VMEM((t_lat, HEAD_DIM), BF16), pltpu.VMEM((t_lat, HEAD_DIM), BF16)],
        compiler_params=_params(("arbitrary",), need),
        name="neighbourhood_attention",
    )(p, p, p, p, p, p, cos, sin, bias)


def _scan_kernel(*refs, nh, reverse, readout):
    if readout:
        qd_ref, z_ref, i_ref, lb_ref, g_ref, other_ref, ng_ref, o_ref, st_ref = refs
    else:
        qd_ref, z_ref, i_ref, lb_ref, o_ref, st_ref = refs
    c = CHUNK

    @pl.when(pl.program_id(0) == 0)
    def _():
        st_ref[...] = jnp.zeros_like(st_ref)

    ri = lax.broadcasted_iota(jnp.int32, (c, 1), 0)
    ti = lax.broadcasted_iota(jnp.int32, (c, c), 0)
    si = lax.broadcasted_iota(jnp.int32, (c, c), 1)
    diag = ti == si
    widths = [1 << k for k in range(c.bit_length() - 1)]
    masks = [jnp.logical_and(((ti ^ si) >> (w.bit_length() - 1)) == 1,
                             ((ti & w) == 0) if reverse else ((ti & w) != 0)) for w in widths]

    for h in range(nh):
        sl = slice(h * HEAD_DIM, (h + 1) * HEAD_DIM)
        lb = lb_ref[:, sl]
        f = lb + (1.0 - lb) * _sigmoid(z_ref[:, sl])
        l = jnp.log(jnp.maximum(f, F_MIN)) * LOG2_E
        kg = 1.0 - f
        q = _silu(qd_ref[:, sl])
        v = i_ref[:, sl].astype(BF16)

        att = jnp.where(diag, _dot_nt(q.astype(BF16), kg.astype(BF16)), 0.0)
        cf, tot = l, l
        for w, mk in zip(widths, masks):
            if w < SUBLANES:
                odd = (ri & w) != 0
                if reverse:
                    x = jnp.where(odd, kg, q) * jnp.exp2(jnp.where(odd, cf - l, tot - cf + l))
                else:
                    x = jnp.where(odd, q, kg) * jnp.exp2(jnp.where(odd, cf, tot - cf))
                prev_tot = pltpu.roll(tot, w, 0)
                next_tot = pltpu.roll(tot, c - w, 0)
                cf = cf + jnp.where(odd, prev_tot, 0.0)
                tot = tot + jnp.where(odd, prev_tot, next_tot)
            else:
                xs, cfs, tots = [], [], []
                for k in range(c // w):
                    rows = slice(k * w, (k + 1) * w)
                    other = slice((k ^ 1) * w, ((k ^ 1) + 1) * w)
                    is_query = (k % 2 == 0) == reverse
                    if reverse:
                        arg = (tot[rows] - cf[rows] + l[rows]) if is_query else (cf[rows] - l[rows])
                    else:
                        arg = cf[rows] if is_query else (tot[rows] - cf[rows])
                    xs.append((q[rows] if is_query else kg[rows]) * jnp.exp2(arg))
                    cfs.append(cf[rows] + tot[other] if k % 2 else cf[rows])
                    tots.append(tot[rows] + tot[other])
                x = jnp.concatenate(xs, axis=0)
                cf = jnp.concatenate(cfs, axis=0)
                tot = jnp.concatenate(tots, axis=0)
            xb = x.astype(BF16)
            att = jnp.where(mk, _dot_nt(xb, xb), att)
        to_state = (tot - cf + l) if reverse else cf
        to_end = (cf - l) if reverse else (tot - cf)
        q_in = (q * jnp.exp2(to_state)).astype(BF16)
        k_out = (kg * jnp.exp2(to_end)).astype(BF16)

        st = st_ref[h]
        o = _dot_nt(q_in, st.astype(BF16)) + jnp.dot(att.astype(BF16), v, preferred_element_type=F32)
        st_ref[h] = st * jnp.exp2(tot[0:1, :]) + lax.dot_general(
            v, k_out, (((0,), (0,)), ((), ())), preferred_element_type=F32)
        if readout:
            o = o + other_ref[:, sl]
            o = o * lax.rsqrt(jnp.mean(o * o, axis=-1, keepdims=True) + EPS) * ng_ref[...]
            o_ref[:, sl] = (o * _silu(g_ref[:, sl])).astype(o_ref.dtype)
        else:
            o_ref[:, sl] = o


def _scan(p, lb, bw, t_lat, reverse, other=None, norm_g=None):
    m = p.shape[0]
    nh = bw // HEAD_DIM
    n_lat = t_lat // CHUNK
    n_all = m // CHUNK
    n_ctx = n_all - n_lat
    readout = other is not None
    if reverse:
        row_blk = lambda j: jnp.where(j < n_ctx, n_all - 1 - j, n_lat - 1 - (j - n_ctx))
    else:
        row_blk = lambda j: jnp.where(j < n_ctx, n_lat + j, j - n_ctx)
    blk = lambda c: pl.BlockSpec((CHUNK, bw), lambda j: (row_blk(j), c))
    qd_col, zf_col, zb_col, i_col, g_col = 7, 8, 9, 10, 11
    in_specs = [blk(qd_col), blk(zb_col if reverse else zf_col), blk(i_col), pl.BlockSpec((1, bw), lambda j: (0, 0))]
    args = [p, p, p, lb.reshape(1, bw)]
    if readout:
        in_specs += [blk(g_col), blk(0), pl.BlockSpec((1, HEAD_DIM), lambda j: (0, 0))]
        args += [p, other, norm_g.reshape(1, HEAD_DIM)]
    return pl.pallas_call(
        functools.partial(_scan_kernel, nh=nh, reverse=reverse, readout=readout),
        grid=(n_all,),
        in_specs=in_specs,
        out_specs=pl.BlockSpec((CHUNK, bw), lambda j: (row_blk(j), 0)),
        out_shape=jax.ShapeDtypeStruct((m, bw), BF16 if readout else F32),
        scratch_shapes=[pltpu.VMEM((nh, HEAD_DIM, HEAD_DIM), F32)],
        compiler_params=_params(("arbitrary",), 48 * CHUNK * bw * 4),
        name="hgrn2_scan_bwd" if reverse else "hgrn2_scan_fwd",
    )(*args)


def kernel(x, c, ctx, c_ctx, w_ada, b_ada, norm1_g, norm2_g, w_in, a_norm_g, a_norm_b, a_ws, a_bs, b_conv_w,
           b_conv_b, b_norm_g, b_norm_b, c_rpb, d_lb_logits, d_norm_g, w_branch, w_gate, b_gate, w_out, w_ffn1,
           w_ffn3, w_ffn2, final_g):
    batch, t_lat, d = x.shape
    n_ctx = ctx.shape[1]
    depth = w_ada.shape[0]
    bw = d // N_BRANCH
    assert batch == 1 and t_lat % GRID_W == 0 and t_lat % CHUNK == 0 and n_ctx % CHUNK == 0

    xa = jnp.concatenate([x[0], ctx[0]], axis=0)
    cond = jnp.zeros((SUBLANES, d), F32).at[0].set(c[0]).at[1].set(c_ctx)
    mod = _modulation(cond, w_ada, b_ada)

    prob = jax.nn.softmax(d_lb_logits.astype(F32), axis=1)
    lbound = jnp.cumsum(prob, axis=1) - prob[:, :1]

    cos, sin = _rope_tables(t_lat)
    bias = _bias_table(c_rpb)
    w_gate_b, w_branch_b, w2_b = (w.astype(BF16) for w in (w_gate, w_branch, w_ffn2))

    for l in range(depth):
        last = l == depth - 1
        m = t_lat if last else t_lat + n_ctx
        h = _norm_mod(xa, norm1_g[l], mod, l, 0, 1, t_lat)
        p = _ws_matmul(h, w_in, l, F32)
        ya = _mix_a(p, a_ws[l], a_bs[l], a_norm_g[l], a_norm_b[l], bw)
        yb = _mix_b(p, b_conv_w[l], b_conv_b[l], b_norm_g[l], b_norm_b[l], bw, t_lat)
        yc = _attention(p, cos, sin, bias, l, bw, t_lat, with_ctx=not last)
        o_bwd = _scan(p, lbound[1, l], bw, t_lat, reverse=True)
        yd = _scan(p, lbound[0, l], bw, t_lat, reverse=False, other=o_bwd, norm_g=d_norm_g[l])
        merged = _gated_merge(h, (ya, yb, yc, yd), w_gate_b, b_gate, w_branch_b, l, m)
        x1 = _ws_matmul_residual(merged, w_out, l, xa, mod, 2, t_lat, m)
        hf = _norm_mod(x1, norm2_g[l], mod, l, 3, 4, t_lat)
        act = _ws_swiglu_up(hf, w_ffn1, w_ffn3, l, m)
        xa = _matmul_residual(act, w2_b, l, x1, mod, 5, t_lat, m)

    return _final_norm(xa[:t_lat], final_g)[None]
```

```python
import functools
import math

import jax
import jax.numpy as jnp
from jax import lax
from jax.experimental import pallas as pl
from jax.experimental.pallas import tpu as pltpu

F32 = jnp.float32
BF16 = jnp.bfloat16

HEAD_DIM = 128
CHUNK = 128
GRID_W = 64
WIN_H = 8
WIN_W = 16
CONV_K = 31
N_BRANCH = 4
ROPE_THETA = 10000.0
EPS = 1e-6
F_MIN = 1e-6
NEG_BIG = -1e30
LOG2_E = 1.4426950408889634

V7X_VMEM_BYTES = 64 * 2**20
VMEM_CAP = V7X_VMEM_BYTES - 6 * 2**20
SUBLANES = 8
LANES = 128
CONV_HALO = 16


def _pick(n, target, quantum):
    best = None
    for t in range(quantum, min(n, target) + 1, quantum):
        if n % t == 0:
            best = t
    assert best is not None, (n, target, quantum)
    return best


def _params(sem, need_bytes):
    limit = int(min(VMEM_CAP, max(need_bytes * 5 // 4, 16 * 2**20)))
    return pltpu.CompilerParams(dimension_semantics=sem, vmem_limit_bytes=limit)


def _sigmoid(x):
    return jax.nn.sigmoid(x)


def _silu(x):
    return x * jax.nn.sigmoid(x)


def _gelu(x):
    return 0.5 * x * (1.0 + lax.erf(x * 0.7071067811865476))


def _row_select(rows_is_ctx, p_ref):
    return jnp.where(rows_is_ctx, p_ref[1:2, :], p_ref[0:1, :])


def _mod_kernel(cond_ref, w_ref, b_ref, o_ref):
    s = _silu(cond_ref[...]).astype(BF16)
    o_ref[...] = jnp.dot(s, w_ref[...].astype(BF16), preferred_element_type=F32) + b_ref[...]


def _modulation(cond, w_ada, b_ada):
    depth, d, n = w_ada.shape
    tn = _pick(n, 512, LANES)
    need = 2 * d * tn * 4 + d * tn * 2
    return pl.pallas_call(
        _mod_kernel,
        grid=(depth, n // tn),
        in_specs=[
            pl.BlockSpec((SUBLANES, d), lambda l, j: (0, 0)),
            pl.BlockSpec((None, d, tn), lambda l, j: (l, 0, j)),
            pl.BlockSpec((None, 1, tn), lambda l, j: (l, 0, j)),
        ],
        out_specs=pl.BlockSpec((None, SUBLANES, tn), lambda l, j: (l, 0, j)),
        out_shape=jax.ShapeDtypeStruct((depth, SUBLANES, n), F32),
        compiler_params=_params(("arbitrary", "arbitrary"), need),
        name="adaln_modulation",
    )(cond, w_ada, b_ada.reshape(depth, 1, n))


NORM_ROWS = 8


def _row_rsqrt_stats(x_ref, rs_ref, tm):
    def stats(r, carry):
        r0 = pl.multiple_of(r * NORM_ROWS, NORM_ROWS)
        x = x_ref[pl.ds(r0, NORM_ROWS), :]
        rs_ref[pl.ds(r0, NORM_ROWS), :] = lax.rsqrt(jnp.mean(x * x, axis=-1, keepdims=True) + EPS)
        return carry

    lax.fori_loop(0, tm // NORM_ROWS, stats, 0, unroll=4)


def _norm_mod_kernel(x_ref, g_ref, sh_ref, sc_ref, o_ref, rs_ref, gs_ref, *, t_lat, tm):
    _row_rsqrt_stats(x_ref, rs_ref, tm)
    which = (pl.program_id(0) * tm >= t_lat).astype(jnp.int32)
    shape = (NORM_ROWS, x_ref.shape[1])
    gs_ref[0:NORM_ROWS, :] = jnp.broadcast_to(g_ref[...] * (1.0 + sc_ref[pl.ds(which, 1), :]), shape)
    gs_ref[NORM_ROWS:, :] = jnp.broadcast_to(sh_ref[pl.ds(which, 1), :], shape)

    def scale(r, carry):
        r0 = pl.multiple_of(r * NORM_ROWS, NORM_ROWS)
        y = x_ref[pl.ds(r0, NORM_ROWS), :] * rs_ref[pl.ds(r0, NORM_ROWS), :]
        o_ref[pl.ds(r0, NORM_ROWS), :] = (y * gs_ref[0:NORM_ROWS, :] + gs_ref[NORM_ROWS:, :]).astype(o_ref.dtype)
        return carry

    lax.fori_loop(0, tm // NORM_ROWS, scale, 0, unroll=8)


def _norm_mod(x, g, mod, layer, shift_blk, scale_blk, t_lat):
    m, d = x.shape
    tm = _pick(math.gcd(t_lat, m), 512, NORM_ROWS)
    return pl.pallas_call(
        functools.partial(_norm_mod_kernel, t_lat=t_lat, tm=tm),
        grid=(m // tm,),
        in_specs=[
            pl.BlockSpec((tm, d), lambda i: (i, 0)),
            pl.BlockSpec((1, d), lambda i: (0, 0)),
            pl.BlockSpec((None, SUBLANES, d), lambda i: (layer, 0, shift_blk)),
            pl.BlockSpec((None, SUBLANES, d), lambda i: (layer, 0, scale_blk)),
        ],
        out_specs=pl.BlockSpec((tm, d), lambda i: (i, 0)),
        out_shape=jax.ShapeDtypeStruct((m, d), BF16),
        scratch_shapes=[pltpu.VMEM((tm, 1), F32), pltpu.VMEM((2 * NORM_ROWS, d), F32)],
        compiler_params=_params(("arbitrary",), 2 * tm * d * 6 + tm * LANES * 4 + 4 * SUBLANES * d * 4),
        name="rmsnorm_modulate",
    )(x, g.reshape(1, d), mod, mod)


def _final_norm_kernel(x_ref, g_ref, o_ref, rs_ref, *, tm):
    _row_rsqrt_stats(x_ref, rs_ref, tm)

    def scale(r, carry):
        r0 = pl.multiple_of(r * NORM_ROWS, NORM_ROWS)
        o_ref[pl.ds(r0, NORM_ROWS), :] = x_ref[pl.ds(r0, NORM_ROWS), :] * rs_ref[pl.ds(r0, NORM_ROWS), :] * g_ref[...]
        return carry

    lax.fori_loop(0, tm // NORM_ROWS, scale, 0, unroll=8)


def _final_norm(x, g):
    m, d = x.shape
    tm = _pick(m, 512, NORM_ROWS)
    return pl.pallas_call(
        functools.partial(_final_norm_kernel, tm=tm),
        grid=(m // tm,),
        in_specs=[pl.BlockSpec((tm, d), lambda i: (i, 0)), pl.BlockSpec((NORM_ROWS, d), lambda i: (0, 0))],
        out_specs=pl.BlockSpec((tm, d), lambda i: (i, 0)),
        out_shape=jax.ShapeDtypeStruct((m, d), F32),
        scratch_shapes=[pltpu.VMEM((tm, 1), F32)],
        compiler_params=_params(("arbitrary",), 2 * tm * d * 8 + tm * LANES * 4),
        name="final_rmsnorm",
    )(x, jnp.broadcast_to(g.reshape(1, d), (NORM_ROWS, d)))


def _ws_mm_kernel(a_ref, w_ref, o_ref, wb):
    @pl.when(pl.program_id(1) == 0)
    def _():
        wb[...] = w_ref[...].astype(BF16)

    o_ref[...] = jnp.dot(a_ref[...], wb[...], preferred_element_type=F32).astype(o_ref.dtype)


def _ws_matmul(a, w, layer, col0, n, out_dtype):
    m, k = a.shape
    tm = _pick(m, 1408, LANES)
    tn = _pick(math.gcd(n, col0) if col0 else n, 512, LANES)
    jb = col0 // tn
    osz = jnp.dtype(out_dtype).itemsize
    need = 2 * (tm * k * 2 + k * tn * 4 + tm * tn * osz) + k * tn * 2 + tm * tn * 4
    return pl.pallas_call(
        _ws_mm_kernel,
        grid=(n // tn, m // tm),
        in_specs=[pl.BlockSpec((tm, k), lambda j, i: (i, 0)),
                  pl.BlockSpec((None, k, tn), lambda j, i: (layer, 0, jb + j))],
        out_specs=pl.BlockSpec((tm, tn), lambda j, i: (i, j)),
        out_shape=jax.ShapeDtypeStruct((m, n), out_dtype),
        scratch_shapes=[pltpu.VMEM((k, tn), BF16)],
        compiler_params=_params(("arbitrary", "arbitrary"), need),
        name="input_projection",
    )(a, w)


def _ws_resid_kernel(a_ref, w_ref, x_ref, gate_ref, o_ref, wb, *, t_lat, tm):
    @pl.when(pl.program_id(1) == 0)
    def _():
        wb[...] = w_ref[...].astype(BF16)

    rows = pl.program_id(1) * tm + lax.broadcasted_iota(jnp.int32, (tm, 1), 0)
    gate = _row_select(rows >= t_lat, gate_ref)
    o_ref[...] = x_ref[...] + gate * jnp.dot(a_ref[...], wb[...], preferred_element_type=F32)


def _ws_matmul_residual(a, w, layer, x, mod, gate_blk, t_lat, m):
    k = a.shape[1]
    n = w.shape[2]
    tm = _pick(m, 1408, LANES)
    tn = _pick(n, 512, LANES)
    nb = n // tn
    need = 2 * (tm * k * 2 + k * tn * 4 + 2 * tm * tn * 4) + k * tn * 2 + tm * tn * 4
    return pl.pallas_call(
        functools.partial(_ws_resid_kernel, t_lat=t_lat, tm=tm),
        grid=(nb, m // tm),
        in_specs=[
            pl.BlockSpec((tm, k), lambda j, i: (i, 0)),
            pl.BlockSpec((None, k, tn), lambda j, i: (layer, 0, j)),
            pl.BlockSpec((tm, tn), lambda j, i: (i, j)),
            pl.BlockSpec((None, SUBLANES, tn), lambda j, i: (layer, 0, gate_blk * nb + j)),
        ],
        out_specs=pl.BlockSpec((tm, tn), lambda j, i: (i, j)),
        out_shape=jax.ShapeDtypeStruct((m, n), F32),
        scratch_shapes=[pltpu.VMEM((k, tn), BF16)],
        compiler_params=_params(("arbitrary", "arbitrary"), need),
        name="output_projection",
    )(a, w, x, mod)


def _ws_swiglu_kernel(a_ref, w1_ref, w3_ref, o_ref, w1b, w3b):
    @pl.when(pl.program_id(1) == 0)
    def _():
        w1b[...] = w1_ref[...].astype(BF16)
        w3b[...] = w3_ref[...].astype(BF16)

    a = a_ref[...]
    u = jnp.dot(a, w1b[...], preferred_element_type=F32)
    g = jnp.dot(a, w3b[...], preferred_element_type=F32)
    o_ref[...] = (_silu(u) * g).astype(o_ref.dtype)


def _ws_swiglu_up(a, w1, w3, layer, m):
    k = a.shape[1]
    n = w1.shape[2]
    tm = _pick(m, 1408, LANES)
    tn = _pick(n, 256, LANES)
    need = 2 * (tm * k * 2 + 2 * k * tn * 4 + tm * tn * 2) + 2 * k * tn * 2 + 3 * tm * tn * 4
    w_spec = pl.BlockSpec((None, k, tn), lambda j, i: (layer, 0, j))
    return pl.pallas_call(
        _ws_swiglu_kernel,
        grid=(n // tn, m // tm),
        in_specs=[pl.BlockSpec((tm, k), lambda j, i: (i, 0)), w_spec, w_spec],
        out_specs=pl.BlockSpec((tm, tn), lambda j, i: (i, j)),
        out_shape=jax.ShapeDtypeStruct((m, n), BF16),
        scratch_shapes=[pltpu.VMEM((k, tn), BF16), pltpu.VMEM((k, tn), BF16)],
        compiler_params=_params(("arbitrary", "arbitrary"), need),
        name="swiglu_up",
    )(a, w1, w3)


def _mm_resid_kernel(a_ref, w_ref, x_ref, gate_ref, o_ref, *, t_lat, tm):
    rows = pl.program_id(0) * tm + lax.broadcasted_iota(jnp.int32, (tm, 1), 0)
    gate = _row_select(rows >= t_lat, gate_ref)
    o_ref[...] = x_ref[...] + gate * jnp.dot(a_ref[...], w_ref[...], preferred_element_type=F32)


def _matmul_residual(a, w, layer, x, mod, gate_blk, t_lat, m):
    k = a.shape[1]
    n = w.shape[2]
    tm = _pick(m, 768, 256)
    tn = _pick(n, 256 if tm > 512 else 512, LANES)
    need = 2 * (tm * k * 2 + k * tn * 2 + 2 * tm * tn * 4) + tm * tn * 4
    nb = n // tn
    return pl.pallas_call(
        functools.partial(_mm_resid_kernel, t_lat=t_lat, tm=tm),
        grid=(m // tm, nb),
        in_specs=[
            pl.BlockSpec((tm, k), lambda i, j: (i, 0)),
            pl.BlockSpec((None, k, tn), lambda i, j: (layer, 0, j)),
            pl.BlockSpec((tm, tn), lambda i, j: (i, j)),
            pl.BlockSpec((None, SUBLANES, tn), lambda i, j: (layer, 0, gate_blk * nb + j)),
        ],
        out_specs=pl.BlockSpec((tm, tn), lambda i, j: (i, j)),
        out_shape=jax.ShapeDtypeStruct((m, n), F32),
        compiler_params=_params(("arbitrary", "arbitrary"), need),
        name="ffn_down",
    )(a, w, x, mod)


def _merge_kernel(h_ref, ya_ref, yb_ref, yc_ref, yd_ref, wg0, wg1, wg2, wg3, bg0, bg1, bg2, bg3, wb_ref, o_ref):
    h = h_ref[...]
    acc = None
    for i, (y_ref, wg, bg) in enumerate(((ya_ref, wg0, bg0), (yb_ref, wg1, bg1), (yc_ref, wg2, bg2), (yd_ref, wg3, bg3))):
        gate = _sigmoid(jnp.dot(h, wg[...], preferred_element_type=F32) + bg[...])
        term = gate * jnp.dot(y_ref[...], wb_ref[i], preferred_element_type=F32)
        acc = term if acc is None else acc + term
    o_ref[...] = acc.astype(o_ref.dtype)


def _gated_merge(h, ys, w_gate, b_gate, w_branch, layer, m):
    k = h.shape[1]
    depth, nbr, bw, n = w_branch.shape
    tm = _pick(m, 768, 256)
    tn = _pick(n, 256, LANES)
    nb = n // tn
    need = 2 * (tm * k * 2 + nbr * tm * bw * 2 + nbr * k * tn * 2 + nbr * bw * tn * 2 + tm * tn * 2) + 4 * tm * tn * 4
    y_spec = pl.BlockSpec((tm, bw), lambda i, j: (i, 0))
    wg_specs = [pl.BlockSpec((None, k, tn), functools.partial(lambda i, j, b: (layer, 0, b * nb + j), b=b))
                for b in range(nbr)]
    bg_specs = [pl.BlockSpec((None, 1, tn), functools.partial(lambda i, j, b: (layer, 0, b * nb + j), b=b))
                for b in range(nbr)]
    bg = b_gate.reshape(depth, 1, -1)
    return pl.pallas_call(
        _merge_kernel,
        grid=(m // tm, nb),
        in_specs=[pl.BlockSpec((tm, k), lambda i, j: (i, 0)), y_spec, y_spec, y_spec, y_spec]
        + wg_specs + bg_specs + [pl.BlockSpec((None, nbr, bw, tn), lambda i, j: (layer, 0, 0, j))],
        out_specs=pl.BlockSpec((tm, tn), lambda i, j: (i, j)),
        out_shape=jax.ShapeDtypeStruct((m, n), BF16),
        compiler_params=_params(("arbitrary", "arbitrary"), need),
        name="gated_merge",
    )(h, *ys, w_gate, w_gate, w_gate, w_gate, bg, bg, bg, bg, w_branch)


def _mix_a_kernel(pu_ref, pv_ref, ws_ref, bias_ref, ng_ref, nb_ref, o_ref, *, nh):
    zu = _gelu(pu_ref[...].astype(F32))
    zv = _gelu(pv_ref[...].astype(F32))
    mu = jnp.mean(zv, axis=-1, keepdims=True)
    zc = zv - mu
    var = jnp.mean(zc * zc, axis=-1, keepdims=True)
    v = (zc * lax.rsqrt(var + EPS) * ng_ref[...] + nb_ref[...]).astype(BF16)
    for g in range(nh):
        sl = slice(g * HEAD_DIM, (g + 1) * HEAD_DIM)
        s = jnp.dot(ws_ref[g], v[:, sl], preferred_element_type=F32) + bias_ref[:, sl]
        o_ref[:, sl] = (zu[:, sl] * s).astype(o_ref.dtype)


def _mix_a(p, ws, bs, ng, nb, bw):
    m = p.shape[0]
    nh = bw // HEAD_DIM
    bias = jnp.repeat(bs.T, HEAD_DIM, axis=1)
    return pl.pallas_call(
        functools.partial(_mix_a_kernel, nh=nh),
        grid=(m // CHUNK,),
        in_specs=[
            pl.BlockSpec((CHUNK, bw), lambda i: (i, 0)),
            pl.BlockSpec((CHUNK, bw), lambda i: (i, 1)),
            pl.BlockSpec((nh, CHUNK, CHUNK), lambda i: (0, 0, 0)),
            pl.BlockSpec((CHUNK, bw), lambda i: (0, 0)),
            pl.BlockSpec((1, bw), lambda i: (0, 0)),
            pl.BlockSpec((1, bw), lambda i: (0, 0)),
        ],
        out_specs=pl.BlockSpec((CHUNK, bw), lambda i: (i, 0)),
        out_shape=jax.ShapeDtypeStruct((m, bw), BF16),
        compiler_params=_params(("arbitrary",), 16 * CHUNK * bw * 4),
        name="mixer_gmlp",
    )(p, p, ws.astype(BF16), bias, ng.reshape(1, bw), nb.reshape(1, bw))


CONV_ROWS = 32


def _mix_b_kernel(ap_ref, ac_ref, an_ref, gp_ref, gc_ref, gn_ref, cw_ref, cb_ref, ng_ref, nb_ref, o_ref, ysh, yconv,
                  *, tiles_lat, tiles_tot, tmb):
    i = pl.program_id(0)
    first = jnp.logical_or(i == 0, i == tiles_lat)
    last = jnp.logical_or(i == tiles_lat - 1, i == tiles_tot - 1)
    glu = lambda a_ref, g_ref: a_ref[...].astype(F32) * _sigmoid(g_ref[...].astype(F32))
    yp = glu(ap_ref, gp_ref)
    yn = glu(an_ref, gn_ref)
    ysh[0, 0:CONV_HALO, :] = jnp.where(first, 0.0, yp)
    ysh[0, CONV_HALO:CONV_HALO + tmb, :] = glu(ac_ref, gc_ref)
    ysh[0, CONV_HALO + tmb:, :] = jnp.where(last, 0.0, yn)
    nrow = tmb + 2 * CONV_HALO
    for c0 in range(0, ysh.shape[2], LANES):
        base = ysh[0, :, c0:c0 + LANES]
        for b in range(1, SUBLANES):
            ysh[b, :, c0:c0 + LANES] = pltpu.roll(base, nrow - b, 0)
    off = CONV_HALO - CONV_K // 2

    bw = ysh.shape[2]
    groups = CONV_ROWS // SUBLANES

    def conv_block(rb, carry):
        r0 = pl.multiple_of(rb * CONV_ROWS, CONV_ROWS)
        acc = None
        for j in range(CONV_K):
            b = (off + j) % SUBLANES
            taps = ysh[b, pl.ds(r0 + (off + j - b), CONV_ROWS), :].reshape(groups, SUBLANES, bw)
            term = taps * cw_ref[j * SUBLANES:(j + 1) * SUBLANES, :][None]
            acc = term if acc is None else acc + term
        yconv[pl.ds(r0, CONV_ROWS), :] = acc.reshape(CONV_ROWS, bw)
        return carry

    lax.fori_loop(0, tmb // CONV_ROWS, conv_block, 0)

    def norm_group(rg, carry):
        r0 = pl.multiple_of(rg * SUBLANES, SUBLANES)
        y = yconv[pl.ds(r0, SUBLANES), :] + cb_ref[...]
        mu = jnp.mean(y, axis=-1, keepdims=True)
        yc = y - mu
        var = jnp.mean(yc * yc, axis=-1, keepdims=True)
        z = yc * lax.rsqrt(var + EPS) * ng_ref[...] + nb_ref[...]
        o_ref[pl.ds(r0, SUBLANES), :] = _silu(z).astype(o_ref.dtype)
        return carry

    lax.fori_loop(0, tmb // SUBLANES, norm_group, 0, unroll=8)


def _mix_b(p, cw, cb, ng, nb, bw, t_lat):
    m = p.shape[0]
    tmb = _pick(math.gcd(t_lat, m - t_lat), 256, CONV_HALO)
    hb = tmb // CONV_HALO
    nhalo = m // CONV_HALO
    a_col, g_col = 2, 3
    prev = lambda c: (lambda i: (jnp.maximum(i * hb - 1, 0), c))
    cur = lambda c: (lambda i: (i, c))
    nxt = lambda c: (lambda i: (jnp.minimum((i + 1) * hb, nhalo - 1), c))
    vec = pl.BlockSpec((SUBLANES, bw), lambda i: (0, 0))
    rep = lambda a: jnp.broadcast_to(a.reshape(-1, 1, bw), (a.size // bw, SUBLANES, bw)).reshape(-1, bw)
    return pl.pallas_call(
        functools.partial(_mix_b_kernel, tiles_lat=t_lat // tmb, tiles_tot=m // tmb, tmb=tmb),
        grid=(m // tmb,),
        in_specs=[
            pl.BlockSpec((CONV_HALO, bw), prev(a_col)),
            pl.BlockSpec((tmb, bw), cur(a_col)),
            pl.BlockSpec((CONV_HALO, bw), nxt(a_col)),
            pl.BlockSpec((CONV_HALO, bw), prev(g_col)),
            pl.BlockSpec((tmb, bw), cur(g_col)),
            pl.BlockSpec((CONV_HALO, bw), nxt(g_col)),
            pl.BlockSpec((CONV_K * SUBLANES, bw), lambda i: (0, 0)),
            vec, vec, vec,
        ],
        out_specs=pl.BlockSpec((tmb, bw), lambda i: (i, 0)),
        out_shape=jax.ShapeDtypeStruct((m, bw), BF16),
        scratch_shapes=[pltpu.VMEM((SUBLANES, tmb + 2 * CONV_HALO, bw), F32), pltpu.VMEM((tmb, bw), F32)],
        compiler_params=_params(("arbitrary",), 20 * tmb * bw * 4),
        name="mixer_conformer_conv",
    )(p, p, p, p, p, p, rep(cw), rep(cb), rep(ng), rep(nb))


def _rope_tables(t_lat):
    m = HEAD_DIM // 4
    freqs = ROPE_THETA ** (-jnp.arange(m, dtype=F32) / m)
    r = t_lat // GRID_W
    ang_r = jnp.repeat(jnp.arange(r, dtype=F32)[:, None] * freqs[None, :], GRID_W, axis=0)
    ang_c = jnp.tile(jnp.arange(GRID_W, dtype=F32)[:, None] * freqs[None, :], (r, 1))
    cos = jnp.concatenate([jnp.cos(ang_r), jnp.cos(ang_r), jnp.cos(ang_c), jnp.cos(ang_c)], axis=-1)
    sin = jnp.concatenate([-jnp.sin(ang_r), jnp.sin(ang_r), -jnp.sin(ang_c), jnp.sin(ang_c)], axis=-1)
    return cos, sin


def _rope(x, cos, sin):
    lane = lax.broadcasted_iota(jnp.int32, x.shape, 1)
    low = (lane % (HEAD_DIM // 2)) < (HEAD_DIM // 4)
    swapped = jnp.where(low, pltpu.roll(x, HEAD_DIM - HEAD_DIM // 4, 1), pltpu.roll(x, HEAD_DIM // 4, 1))
    return x * cos + swapped * sin


def _bias_kernel(rpb_ref, o_ref):
    n = GRID_W * GRID_W
    col = lax.broadcasted_iota(jnp.int32, (1, n), 1)
    q = col // GRID_W
    k = col % GRID_W
    dc = jnp.clip(k - q, -(WIN_W - 1), WIN_W - 1) + (WIN_W - 1)
    start = jnp.clip(q - WIN_W // 2, 0, GRID_W - WIN_W)
    in_win = jnp.logical_and(k >= start, k < start + WIN_W)
    val = jnp.zeros(o_ref.shape, F32)
    for d in range(2 * WIN_W - 1):
        val = jnp.where(dc == d, rpb_ref[:, d:d + 1], val)
    o_ref[...] = jnp.where(in_win, val * LOG2_E, NEG_BIG)


ATTN_QROWS = 4
ATTN_KROWS = ATTN_QROWS + WIN_H
ATTN_SHIFT = WIN_H // 2


def _bias_table(rpb):
    depth, nh, ndr, ndc = rpb.shape
    rows = depth * nh * ndr
    toe = pl.pallas_call(
        _bias_kernel,
        out_shape=jax.ShapeDtypeStruct((rows, GRID_W * GRID_W), F32),
        compiler_params=_params((), 16 * rows * GRID_W * GRID_W),
        name="attention_bias_table",
    )(rpb.reshape(rows, ndc))
    toe = toe.reshape(depth, nh, ndr, GRID_W, GRID_W)
    masked = jnp.full((depth, nh, GRID_W, GRID_W), NEG_BIG, F32)

    def variant(block_off, window_start):
        out = []
        for qr in range(ATTN_QROWS):
            cols = []
            for j in range(ATTN_KROWS):
                visible = window_start(qr) <= j < window_start(qr) + WIN_H
                cols.append(toe[:, :, j - qr - block_off + WIN_H - 1] if visible else masked)
            out.append(jnp.concatenate(cols, axis=-1))
        return jnp.concatenate(out, axis=-2)

    return jnp.stack([variant(0, lambda qr: 0),
                      variant(ATTN_SHIFT, lambda qr: qr),
                      variant(ATTN_KROWS - ATTN_QROWS, lambda qr: ATTN_KROWS - WIN_H)], axis=2)


ATTN_UNROLL = 2
ATTN_PREP_ROWS = 256


def _softmax_pv(scores, values):
    mx = None
    for s in scores:
        smax = jnp.max(s, axis=-1, keepdims=True)
        mx = smax if mx is None else jnp.maximum(mx, smax)
    den = None
    acc = None
    for s, v in zip(scores, values):
        e = jnp.exp2(s - mx)
        esum = jnp.sum(e, axis=-1, keepdims=True)
        den = esum if den is None else den + esum
        pv = jnp.dot(e.astype(BF16), v, preferred_element_type=F32)
        acc = pv if acc is None else acc + pv
    return acc / den


def _dot_nt(a, b):
    return lax.dot_general(a, b, (((1,), (1,)), ((), ())), preferred_element_type=F32)


def _attn_kernel(q_ref, k_ref, v_ref, qc_ref, kc_ref, vc_ref, cos_ref, sin_ref, bias_ref, o_ref, kr_ref,
                 *, rows, t_lat, with_ctx):
    scale = HEAD_DIM ** -0.5 * LOG2_E
    kc = kc_ref[...]
    vc = vc_ref[...]
    nq = ATTN_QROWS * GRID_W
    nk = ATTN_KROWS * GRID_W

    def prep(c, carry):
        rs = pl.ds(pl.multiple_of(c * ATTN_PREP_ROWS, ATTN_PREP_ROWS), ATTN_PREP_ROWS)
        kr_ref[rs, :] = _rope(k_ref[rs, :].astype(F32), cos_ref[rs, :], sin_ref[rs, :]).astype(BF16)
        return carry

    lax.fori_loop(0, t_lat // ATTN_PREP_ROWS, prep, 0)

    def body(blk, carry):
        r0 = blk * ATTN_QROWS
        s0 = jnp.clip(r0 - ATTN_SHIFT, 0, rows - ATTN_KROWS)
        qs = pl.ds(pl.multiple_of(r0 * GRID_W, nq), nq)
        ks = pl.ds(pl.multiple_of(s0 * GRID_W, GRID_W), nk)
        q = q_ref[qs, :]
        q_rot = _rope(q.astype(F32), cos_ref[qs, :], sin_ref[qs, :]).astype(BF16)
        s_lat = _dot_nt(q_rot, kr_ref[ks, :]) * scale
        b = bias_ref[(r0 - s0) // ATTN_SHIFT]
        s_lat = jnp.where(b > 0.5 * NEG_BIG, s_lat + b, NEG_BIG)
        s_ctx = _dot_nt(q, kc) * scale
        o = _softmax_pv((s_lat, s_ctx), (v_ref[ks, :], vc))
        o_ref[qs, :] = o.astype(o_ref.dtype)
        return carry

    lax.fori_loop(0, rows // ATTN_QROWS, body, 0, unroll=ATTN_UNROLL)
    if with_ctx:
        s = _dot_nt(qc_ref[...], kc) * scale
        o_ref[t_lat:, :] = _softmax_pv((s,), (vc,)).astype(o_ref.dtype)


def _attention(p, cos, sin, bias, layer, bw, t_lat, with_ctx):
    m = p.shape[0]
    n_ctx = m - t_lat
    nh = bw // HEAD_DIM
    rows = t_lat // GRID_W
    assert rows >= ATTN_KROWS and rows % ATTN_QROWS == 0 and t_lat % n_ctx == 0 and t_lat % ATTN_PREP_ROWS == 0
    m_out = m if with_ctx else t_lat
    q_col, k_col, v_col = 4 * nh, 5 * nh, 6 * nh
    lat = lambda c: pl.BlockSpec((t_lat, HEAD_DIM), lambda h: (0, c + h))
    ctx = lambda c: pl.BlockSpec((n_ctx, HEAD_DIM), lambda h: (t_lat // n_ctx, c + h))
    table = pl.BlockSpec((t_lat, HEAD_DIM), lambda h: (0, 0), pipeline_mode=pl.Buffered(1))
    nq, nk = ATTN_QROWS * GRID_W, ATTN_KROWS * GRID_W
    need = (2 * 3 * t_lat * HEAD_DIM * 2 + 2 * t_lat * HEAD_DIM * 4 + t_lat * HEAD_DIM * 2
            + 2 * m_out * HEAD_DIM * 2 + 2 * 3 * nq * nk * 4 + 8 * nq * (nk + n_ctx) * 4)
    return pl.pallas_call(
        functools.partial(_attn_kernel, rows=rows, t_lat=t_lat, with_ctx=with_ctx),
        grid=(nh,),
        in_specs=[lat(q_col), lat(k_col), lat(v_col), ctx(q_col), ctx(k_col), ctx(v_col), table, table,
                  pl.BlockSpec((None, None, 3, nq, nk), lambda h: (layer, h, 0, 0, 0))],
        out_specs=pl.BlockSpec((m_out, HEAD_DIM), lambda h: (0, h)),
        out_shape=jax.ShapeDtypeStruct((m_out, bw), BF16),
        scratch_shapes=[pltpu.VMEM((t_lat, HEAD_DIM), BF16)],
        compiler_params=_params(("arbitrary",), need),
        name="neighbourhood_attention",
    )(p, p, p, p, p, p, cos, sin, bias)


def _scan_kernel(*refs, nh, reverse, readout):
    if readout:
        qd_ref, z_ref, i_ref, lb_ref, g_ref, other_ref, ng_ref, o_ref, st_ref = refs
    else:
        qd_ref, z_ref, i_ref, lb_ref, o_ref, st_ref = refs
    c = CHUNK

    @pl.when(pl.program_id(0) == 0)
    def _():
        st_ref[...] = jnp.zeros_like(st_ref)

    ri = lax.broadcasted_iota(jnp.int32, (c, 1), 0)
    ti = lax.broadcasted_iota(jnp.int32, (c, c), 0)
    si = lax.broadcasted_iota(jnp.int32, (c, c), 1)
    diag = ti == si
    widths = [1 << k for k in range(c.bit_length() - 1)]
    masks = [jnp.logical_and(((ti ^ si) >> (w.bit_length() - 1)) == 1,
                             ((ti & w) == 0) if reverse else ((ti & w) != 0)) for w in widths]

    for h in range(nh):
        sl = slice(h * HEAD_DIM, (h + 1) * HEAD_DIM)
        lb = lb_ref[:, sl]
        f = lb + (1.0 - lb) * _sigmoid(z_ref[:, sl])
        l = jnp.log(jnp.maximum(f, F_MIN)) * LOG2_E
        kg = 1.0 - f
        q = _silu(qd_ref[:, sl])
        v = i_ref[:, sl].astype(BF16)

        att = jnp.where(diag, _dot_nt(q.astype(BF16), kg.astype(BF16)), 0.0)
        cf, tot = l, l
        for w, mk in zip(widths, masks):
            if w < SUBLANES:
                odd = (ri & w) != 0
                if reverse:
                    x = jnp.where(odd, kg, q) * jnp.exp2(jnp.where(odd, cf - l, tot - cf + l))
                else:
                    x = jnp.where(odd, q, kg) * jnp.exp2(jnp.where(odd, cf, tot - cf))
                prev_tot = pltpu.roll(tot, w, 0)
                next_tot = pltpu.roll(tot, c - w, 0)
                cf = cf + jnp.where(odd, prev_tot, 0.0)
                tot = tot + jnp.where(odd, prev_tot, next_tot)
            else:
                xs, cfs, tots = [], [], []
                for k in range(c // w):
                    rows = slice(k * w, (k + 1) * w)
                    other = slice((k ^ 1) * w, ((k ^ 1) + 1) * w)
                    is_query = (k % 2 == 0) == reverse
                    if reverse:
                        arg = (tot[rows] - cf[rows] + l[rows]) if is_query else (cf[rows] - l[rows])
                    else:
                        arg = cf[rows] if is_query else (tot[rows] - cf[rows])
                    xs.append((q[rows] if is_query else kg[rows]) * jnp.exp2(arg))
                    cfs.append(cf[rows] + tot[other] if k % 2 else cf[rows])
                    tots.append(tot[rows] + tot[other])
                x = jnp.concatenate(xs, axis=0)
                cf = jnp.concatenate(cfs, axis=0)
                tot = jnp.concatenate(tots, axis=0)
            xb = x.astype(BF16)
            att = jnp.where(mk, _dot_nt(xb, xb), att)
        to_state = (tot - cf + l) if reverse else cf
        to_end = (cf - l) if reverse else (tot - cf)
        q_in = (q * jnp.exp2(to_state)).astype(BF16)
        k_out = (kg * jnp.exp2(to_end)).astype(BF16)

        st = st_ref[h]
        o = _dot_nt(q_in, st.astype(BF16)) + jnp.dot(att.astype(BF16), v, preferred_element_type=F32)
        st_ref[h] = st * jnp.exp2(tot[0:1, :]) + lax.dot_general(
            v, k_out, (((0,), (0,)), ((), ())), preferred_element_type=F32)
        if readout:
            o = o + other_ref[:, sl]
            o = o * lax.rsqrt(jnp.mean(o * o, axis=-1, keepdims=True) + EPS) * ng_ref[...]
            o_ref[:, sl] = (o * _silu(g_ref[:, sl])).astype(o_ref.dtype)
        else:
            o_ref[:, sl] = o


def _scan(p, lb, bw, t_lat, reverse, other=None, norm_g=None):
    m = p.shape[0]
    nh = bw // HEAD_DIM
    n_lat = t_lat // CHUNK
    n_all = m // CHUNK
    n_ctx = n_all - n_lat
    readout = other is not None
    if reverse:
        row_blk = lambda j: jnp.where(j < n_ctx, n_all - 1 - j, n_lat - 1 - (j - n_ctx))
    else:
        row_blk = lambda j: jnp.where(j < n_ctx, n_lat + j, j - n_ctx)
    blk = lambda c: pl.BlockSpec((CHUNK, bw), lambda j: (row_blk(j), c))
    qd_col, zf_col, zb_col, i_col, g_col = 0, 1, 2, 3, 4
    in_specs = [blk(qd_col), blk(zb_col if reverse else zf_col), blk(i_col), pl.BlockSpec((1, bw), lambda j: (0, 0))]
    args = [p, p, p, lb.reshape(1, bw)]
    if readout:
        in_specs += [blk(g_col), blk(0), pl.BlockSpec((1, HEAD_DIM), lambda j: (0, 0))]
        args += [p, other, norm_g.reshape(1, HEAD_DIM)]
    return pl.pallas_call(
        functools.partial(_scan_kernel, nh=nh, reverse=reverse, readout=readout),
        grid=(n_all,),
        in_specs=in_specs,
        out_specs=pl.BlockSpec((CHUNK, bw), lambda j: (row_blk(j), 0)),
        out_shape=jax.ShapeDtypeStruct((m, bw), BF16 if readout else F32),
        scratch_shapes=[pltpu.VMEM((nh, HEAD_DIM, HEAD_DIM), F32)],
        compiler_params=_params(("arbitrary",), 48 * CHUNK * bw * 4),
        name="hgrn2_scan_bwd" if reverse else "hgrn2_scan_fwd",
    )(*args)


def kernel(x, c, ctx, c_ctx, w_ada, b_ada, norm1_g, norm2_g, w_in, a_norm_g, a_norm_b, a_ws, a_bs, b_conv_w,
           b_conv_b, b_norm_g, b_norm_b, c_rpb, d_lb_logits, d_norm_g, w_branch, w_gate, b_gate, w_out, w_ffn1,
           w_ffn3, w_ffn2, final_g):
    batch, t_lat, d = x.shape
    n_ctx = ctx.shape[1]
    depth = w_ada.shape[0]
    bw = d // N_BRANCH
    assert batch == 1 and t_lat % GRID_W == 0 and t_lat % CHUNK == 0 and n_ctx % CHUNK == 0

    xa = jnp.concatenate([x[0], ctx[0]], axis=0)
    cond = jnp.zeros((SUBLANES, d), F32).at[0].set(c[0]).at[1].set(c_ctx)
    mod = _modulation(cond, w_ada, b_ada)

    prob = jax.nn.softmax(d_lb_logits.astype(F32), axis=1)
    lbound = jnp.cumsum(prob, axis=1) - prob[:, :1]

    cos, sin = _rope_tables(t_lat)
    bias = _bias_table(c_rpb)
    w_gate_b, w_branch_b, w2_b = (w.astype(BF16) for w in (w_gate, w_branch, w_ffn2))

    for l in range(depth):
        last = l == depth - 1
        m = t_lat if last else t_lat + n_ctx
        h = _norm_mod(xa, norm1_g[l], mod, l, 0, 1, t_lat)
        p = _ws_matmul(h, w_in, l, 0, 7 * bw, BF16)
        pd = _ws_matmul(h, w_in, l, 7 * bw, 5 * bw, F32)
        ya = _mix_a(p, a_ws[l], a_bs[l], a_norm_g[l], a_norm_b[l], bw)
        yb = _mix_b(p, b_conv_w[l], b_conv_b[l], b_norm_g[l], b_norm_b[l], bw, t_lat)
        yc = _attention(p, cos, sin, bias, l, bw, t_lat, with_ctx=not last)
        o_bwd = _scan(pd, lbound[1, l], bw, t_lat, reverse=True)
        yd = _scan(pd, lbound[0, l], bw, t_lat, reverse=False, other=o_bwd, norm_g=d_norm_g[l])
        merged = _gated_merge(h, (ya, yb, yc, yd), w_gate_b, b_gate, w_branch_b, l, m)
        x1 = _ws_matmul_residual(merged, w_out, l, xa, mod, 2, t_lat, m)
        hf = _norm_mod(x1, norm2_g[l], mod, l, 3, 4, t_lat)
        act = _ws_swiglu_up(hf, w_ffn1, w_ffn3, l, m)
        xa = _matmul_residual(act, w2_b, l, x1, mod, 5, t_lat, m)

    return _final_norm(xa[:t_lat], final_g)[None]
```

```python
import functools
import math

import jax
import jax.numpy as jnp
from jax import lax
from jax.experimental import pallas as pl
from jax.experimental.pallas import tpu as pltpu

F32 = jnp.float32
BF16 = jnp.bfloat16

HEAD_DIM = 128
CHUNK = 128
GRID_W = 64
WIN_H = 8
WIN_W = 16
CONV_K = 31
N_BRANCH = 4
ROPE_THETA = 10000.0
EPS = 1e-6
F_MIN = 1e-6
NEG_BIG = -1e30
LOG2_E = 1.4426950408889634
SCAN_DIRECT = 4
assert (SCAN_DIRECT - 1) * -math.log2(F_MIN) < 120 <= (2 * SCAN_DIRECT - 1) * -math.log2(F_MIN)

V7X_VMEM_BYTES = 64 * 2**20
VMEM_CAP = V7X_VMEM_BYTES - 6 * 2**20
SUBLANES = 8
LANES = 128
CONV_HALO = 16


def _pick(n, target, quantum):
    best = None
    for t in range(quantum, min(n, target) + 1, quantum):
        if n % t == 0:
            best = t
    assert best is not None, (n, target, quantum)
    return best


def _params(sem, need_bytes):
    limit = int(min(VMEM_CAP, max(need_bytes * 5 // 4, 16 * 2**20)))
    return pltpu.CompilerParams(dimension_semantics=sem, vmem_limit_bytes=limit)


def _sigmoid(x):
    return jax.nn.sigmoid(x)


def _silu(x):
    return x * jax.nn.sigmoid(x)


def _gelu(x):
    return 0.5 * x * (1.0 + lax.erf(x * 0.7071067811865476))


def _row_select(rows_is_ctx, p_ref):
    return jnp.where(rows_is_ctx, p_ref[1:2, :], p_ref[0:1, :])


def _mod_kernel(cond_ref, w_ref, b_ref, o_ref):
    s = _silu(cond_ref[...]).astype(BF16)
    o_ref[...] = jnp.dot(s, w_ref[...].astype(BF16), preferred_element_type=F32) + b_ref[...]


def _modulation(cond, w_ada, b_ada):
    depth, d, n = w_ada.shape
    tn = _pick(n, 512, LANES)
    need = 2 * d * tn * 4 + d * tn * 2
    return pl.pallas_call(
        _mod_kernel,
        grid=(depth, n // tn),
        in_specs=[
            pl.BlockSpec((SUBLANES, d), lambda l, j: (0, 0)),
            pl.BlockSpec((None, d, tn), lambda l, j: (l, 0, j)),
            pl.BlockSpec((None, 1, tn), lambda l, j: (l, 0, j)),
        ],
        out_specs=pl.BlockSpec((None, SUBLANES, tn), lambda l, j: (l, 0, j)),
        out_shape=jax.ShapeDtypeStruct((depth, SUBLANES, n), F32),
        compiler_params=_params(("arbitrary", "arbitrary"), need),
        name="adaln_modulation",
    )(cond, w_ada, b_ada.reshape(depth, 1, n))


NORM_ROWS = 8


def _row_rsqrt_stats(x_ref, rs_ref, tm):
    def stats(r, carry):
        r0 = pl.multiple_of(r * NORM_ROWS, NORM_ROWS)
        x = x_ref[pl.ds(r0, NORM_ROWS), :]
        rs_ref[pl.ds(r0, NORM_ROWS), :] = lax.rsqrt(jnp.mean(x * x, axis=-1, keepdims=True) + EPS)
        return carry

    lax.fori_loop(0, tm // NORM_ROWS, stats, 0, unroll=4)


def _norm_mod_kernel(x_ref, g_ref, sh_ref, sc_ref, o_ref, rs_ref, gs_ref, *, t_lat, tm):
    _row_rsqrt_stats(x_ref, rs_ref, tm)
    which = (pl.program_id(0) * tm >= t_lat).astype(jnp.int32)
    shape = (NORM_ROWS, x_ref.shape[1])
    gs_ref[0:NORM_ROWS, :] = jnp.broadcast_to(g_ref[...] * (1.0 + sc_ref[pl.ds(which, 1), :]), shape)
    gs_ref[NORM_ROWS:, :] = jnp.broadcast_to(sh_ref[pl.ds(which, 1), :], shape)

    def scale(r, carry):
        r0 = pl.multiple_of(r * NORM_ROWS, NORM_ROWS)
        y = x_ref[pl.ds(r0, NORM_ROWS), :] * rs_ref[pl.ds(r0, NORM_ROWS), :]
        o_ref[pl.ds(r0, NORM_ROWS), :] = (y * gs_ref[0:NORM_ROWS, :] + gs_ref[NORM_ROWS:, :]).astype(o_ref.dtype)
        return carry

    lax.fori_loop(0, tm // NORM_ROWS, scale, 0, unroll=8)


def _norm_mod(x, g, mod, layer, shift_blk, scale_blk, t_lat):
    m, d = x.shape
    tm = _pick(math.gcd(t_lat, m), 512, NORM_ROWS)
    return pl.pallas_call(
        functools.partial(_norm_mod_kernel, t_lat=t_lat, tm=tm),
        grid=(m // tm,),
        in_specs=[
            pl.BlockSpec((tm, d), lambda i: (i, 0)),
            pl.BlockSpec((1, d), lambda i: (0, 0)),
            pl.BlockSpec((None, SUBLANES, d), lambda i: (layer, 0, shift_blk)),
            pl.BlockSpec((None, SUBLANES, d), lambda i: (layer, 0, scale_blk)),
        ],
        out_specs=pl.BlockSpec((tm, d), lambda i: (i, 0)),
        out_shape=jax.ShapeDtypeStruct((m, d), BF16),
        scratch_shapes=[pltpu.VMEM((tm, 1), F32), pltpu.VMEM((2 * NORM_ROWS, d), F32)],
        compiler_params=_params(("arbitrary",), 2 * tm * d * 6 + tm * LANES * 4 + 4 * SUBLANES * d * 4),
        name="rmsnorm_modulate",
    )(x, g.reshape(1, d), mod, mod)


def _final_norm_kernel(x_ref, g_ref, o_ref, rs_ref, *, tm):
    _row_rsqrt_stats(x_ref, rs_ref, tm)

    def scale(r, carry):
        r0 = pl.multiple_of(r * NORM_ROWS, NORM_ROWS)
        o_ref[pl.ds(r0, NORM_ROWS), :] = x_ref[pl.ds(r0, NORM_ROWS), :] * rs_ref[pl.ds(r0, NORM_ROWS), :] * g_ref[...]
        return carry

    lax.fori_loop(0, tm // NORM_ROWS, scale, 0, unroll=8)


def _final_norm(x, g):
    m, d = x.shape
    tm = _pick(m, 512, NORM_ROWS)
    return pl.pallas_call(
        functools.partial(_final_norm_kernel, tm=tm),
        grid=(m // tm,),
        in_specs=[pl.BlockSpec((tm, d), lambda i: (i, 0)), pl.BlockSpec((NORM_ROWS, d), lambda i: (0, 0))],
        out_specs=pl.BlockSpec((tm, d), lambda i: (i, 0)),
        out_shape=jax.ShapeDtypeStruct((m, d), F32),
        scratch_shapes=[pltpu.VMEM((tm, 1), F32)],
        compiler_params=_params(("arbitrary",), 2 * tm * d * 8 + tm * LANES * 4),
        name="final_rmsnorm",
    )(x, jnp.broadcast_to(g.reshape(1, d), (NORM_ROWS, d)))


def _ws_mm_kernel(a_ref, w_ref, o_ref, wb):
    @pl.when(pl.program_id(1) == 0)
    def _():
        wb[...] = w_ref[...].astype(BF16)

    o_ref[...] = jnp.dot(a_ref[...], wb[...], preferred_element_type=F32).astype(o_ref.dtype)


def _ws_matmul(a, w, layer, col0, n, out_dtype):
    m, k = a.shape
    tm = _pick(m, 1408, LANES)
    tn = _pick(math.gcd(n, col0) if col0 else n, 512, LANES)
    jb = col0 // tn
    osz = jnp.dtype(out_dtype).itemsize
    need = 2 * (tm * k * 2 + k * tn * 4 + tm * tn * osz) + k * tn * 2 + tm * tn * 4
    return pl.pallas_call(
        _ws_mm_kernel,
        grid=(n // tn, m // tm),
        in_specs=[pl.BlockSpec((tm, k), lambda j, i: (i, 0)),
                  pl.BlockSpec((None, k, tn), lambda j, i: (layer, 0, jb + j))],
        out_specs=pl.BlockSpec((tm, tn), lambda j, i: (i, j)),
        out_shape=jax.ShapeDtypeStruct((m, n), out_dtype),
        scratch_shapes=[pltpu.VMEM((k, tn), BF16)],
        compiler_params=_params(("arbitrary", "arbitrary"), need),
        name="input_projection",
    )(a, w)


def _ws_resid_kernel(a_ref, w_ref, x_ref, gate_ref, o_ref, wb, *, t_lat, tm):
    @pl.when(pl.program_id(1) == 0)
    def _():
        wb[...] = w_ref[...].astype(BF16)

    rows = pl.program_id(1) * tm + lax.broadcasted_iota(jnp.int32, (tm, 1), 0)
    gate = _row_select(rows >= t_lat, gate_ref)
    o_ref[...] = x_ref[...] + gate * jnp.dot(a_ref[...], wb[...], preferred_element_type=F32)


def _ws_matmul_residual(a, w, layer, x, mod, gate_blk, t_lat, m):
    k = a.shape[1]
    n = w.shape[2]
    tm = _pick(m, 1408, LANES)
    tn = _pick(n, 512, LANES)
    nb = n // tn
    need = 2 * (tm * k * 2 + k * tn * 4 + 2 * tm * tn * 4) + k * tn * 2 + tm * tn * 4
    return pl.pallas_call(
        functools.partial(_ws_resid_kernel, t_lat=t_lat, tm=tm),
        grid=(nb, m // tm),
        in_specs=[
            pl.BlockSpec((tm, k), lambda j, i: (i, 0)),
            pl.BlockSpec((None, k, tn), lambda j, i: (layer, 0, j)),
            pl.BlockSpec((tm, tn), lambda j, i: (i, j)),
            pl.BlockSpec((None, SUBLANES, tn), lambda j, i: (layer, 0, gate_blk * nb + j)),
        ],
        out_specs=pl.BlockSpec((tm, tn), lambda j, i: (i, j)),
        out_shape=jax.ShapeDtypeStruct((m, n), F32),
        scratch_shapes=[pltpu.VMEM((k, tn), BF16)],
        compiler_params=_params(("arbitrary", "arbitrary"), need),
        name="output_projection",
    )(a, w, x, mod)


def _ws_swiglu_kernel(a_ref, w1_ref, w3_ref, o_ref, w1b, w3b):
    @pl.when(pl.program_id(1) == 0)
    def _():
        w1b[...] = w1_ref[...].astype(BF16)
        w3b[...] = w3_ref[...].astype(BF16)

    a = a_ref[...]
    u = jnp.dot(a, w1b[...], preferred_element_type=F32)
    g = jnp.dot(a, w3b[...], preferred_element_type=F32)
    o_ref[...] = (_silu(u) * g).astype(o_ref.dtype)


def _ws_swiglu_up(a, w1, w3, layer, m):
    k = a.shape[1]
    n = w1.shape[2]
    tm = _pick(m, 1408, LANES)
    tn = _pick(n, 256, LANES)
    need = 2 * (tm * k * 2 + 2 * k * tn * 4 + tm * tn * 2) + 2 * k * tn * 2 + 3 * tm * tn * 4
    w_spec = pl.BlockSpec((None, k, tn), lambda j, i: (layer, 0, j))
    return pl.pallas_call(
        _ws_swiglu_kernel,
        grid=(n // tn, m // tm),
        in_specs=[pl.BlockSpec((tm, k), lambda j, i: (i, 0)), w_spec, w_spec],
        out_specs=pl.BlockSpec((tm, tn), lambda j, i: (i, j)),
        out_shape=jax.ShapeDtypeStruct((m, n), BF16),
        scratch_shapes=[pltpu.VMEM((k, tn), BF16), pltpu.VMEM((k, tn), BF16)],
        compiler_params=_params(("arbitrary", "arbitrary"), need),
        name="swiglu_up",
    )(a, w1, w3)


def _mm_resid_kernel(a_ref, w_ref, x_ref, gate_ref, o_ref, *, t_lat, tm):
    rows = pl.program_id(0) * tm + lax.broadcasted_iota(jnp.int32, (tm, 1), 0)
    gate = _row_select(rows >= t_lat, gate_ref)
    o_ref[...] = x_ref[...] + gate * jnp.dot(a_ref[...], w_ref[...], preferred_element_type=F32)


def _matmul_residual(a, w, layer, x, mod, gate_blk, t_lat, m):
    k = a.shape[1]
    n = w.shape[2]
    tm = _pick(m, 768, 256)
    tn = _pick(n, 256 if tm > 512 else 512, LANES)
    need = 2 * (tm * k * 2 + k * tn * 2 + 2 * tm * tn * 4) + tm * tn * 4
    nb = n // tn
    return pl.pallas_call(
        functools.partial(_mm_resid_kernel, t_lat=t_lat, tm=tm),
        grid=(m // tm, nb),
        in_specs=[
            pl.BlockSpec((tm, k), lambda i, j: (i, 0)),
            pl.BlockSpec((None, k, tn), lambda i, j: (layer, 0, j)),
            pl.BlockSpec((tm, tn), lambda i, j: (i, j)),
            pl.BlockSpec((None, SUBLANES, tn), lambda i, j: (layer, 0, gate_blk * nb + j)),
        ],
        out_specs=pl.BlockSpec((tm, tn), lambda i, j: (i, j)),
        out_shape=jax.ShapeDtypeStruct((m, n), F32),
        compiler_params=_params(("arbitrary", "arbitrary"), need),
        name="ffn_down",
    )(a, w, x, mod)


def _merge_kernel(h_ref, ya_ref, yb_ref, yc_ref, yd_ref, wg0, wg1, wg2, wg3, bg0, bg1, bg2, bg3, wb_ref, o_ref):
    h = h_ref[...]
    acc = None
    for i, (y_ref, wg, bg) in enumerate(((ya_ref, wg0, bg0), (yb_ref, wg1, bg1), (yc_ref, wg2, bg2), (yd_ref, wg3, bg3))):
        gate = _sigmoid(jnp.dot(h, wg[...], preferred_element_type=F32) + bg[...])
        term = gate * jnp.dot(y_ref[...], wb_ref[i], preferred_element_type=F32)
        acc = term if acc is None else acc + term
    o_ref[...] = acc.astype(o_ref.dtype)


def _gated_merge(h, ys, w_gate, b_gate, w_branch, layer, m):
    k = h.shape[1]
    depth, nbr, bw, n = w_branch.shape
    tm = _pick(m, 768, 256)
    tn = _pick(n, 256, LANES)
    nb = n // tn
    need = 2 * (tm * k * 2 + nbr * tm * bw * 2 + nbr * k * tn * 2 + nbr * bw * tn * 2 + tm * tn * 2) + 4 * tm * tn * 4
    y_spec = pl.BlockSpec((tm, bw), lambda i, j: (i, 0))
    wg_specs = [pl.BlockSpec((None, k, tn), functools.partial(lambda i, j, b: (layer, 0, b * nb + j), b=b))
                for b in range(nbr)]
    bg_specs = [pl.BlockSpec((None, 1, tn), functools.partial(lambda i, j, b: (layer, 0, b * nb + j), b=b))
                for b in range(nbr)]
    bg = b_gate.reshape(depth, 1, -1)
    return pl.pallas_call(
        _merge_kernel,
        grid=(m // tm, nb),
        in_specs=[pl.BlockSpec((tm, k), lambda i, j: (i, 0)), y_spec, y_spec, y_spec, y_spec]
        + wg_specs + bg_specs + [pl.BlockSpec((None, nbr, bw, tn), lambda i, j: (layer, 0, 0, j))],
        out_specs=pl.BlockSpec((tm, tn), lambda i, j: (i, j)),
        out_shape=jax.ShapeDtypeStruct((m, n), BF16),
        compiler_params=_params(("arbitrary", "arbitrary"), need),
        name="gated_merge",
    )(h, *ys, w_gate, w_gate, w_gate, w_gate, bg, bg, bg, bg, w_branch)


def _mix_a_kernel(pu_ref, pv_ref, ws_ref, bias_ref, ng_ref, nb_ref, o_ref, *, nh):
    zu = _gelu(pu_ref[...].astype(F32))
    zv = _gelu(pv_ref[...].astype(F32))
    mu = jnp.mean(zv, axis=-1, keepdims=True)
    zc = zv - mu
    var = jnp.mean(zc * zc, axis=-1, keepdims=True)
    v = (zc * lax.rsqrt(var + EPS) * ng_ref[...] + nb_ref[...]).astype(BF16)
    for g in range(nh):
        sl = slice(g * HEAD_DIM, (g + 1) * HEAD_DIM)
        s = jnp.dot(ws_ref[g], v[:, sl], preferred_element_type=F32) + bias_ref[:, sl]
        o_ref[:, sl] = (zu[:, sl] * s).astype(o_ref.dtype)


def _mix_a(p, ws, bs, ng, nb, bw):
    m = p.shape[0]
    nh = bw // HEAD_DIM
    bias = jnp.repeat(bs.T, HEAD_DIM, axis=1)
    return pl.pallas_call(
        functools.partial(_mix_a_kernel, nh=nh),
        grid=(m // CHUNK,),
        in_specs=[
            pl.BlockSpec((CHUNK, bw), lambda i: (i, 0)),
            pl.BlockSpec((CHUNK, bw), lambda i: (i, 1)),
            pl.BlockSpec((nh, CHUNK, CHUNK), lambda i: (0, 0, 0)),
            pl.BlockSpec((CHUNK, bw), lambda i: (0, 0)),
            pl.BlockSpec((1, bw), lambda i: (0, 0)),
            pl.BlockSpec((1, bw), lambda i: (0, 0)),
        ],
        out_specs=pl.BlockSpec((CHUNK, bw), lambda i: (i, 0)),
        out_shape=jax.ShapeDtypeStruct((m, bw), BF16),
        compiler_params=_params(("arbitrary",), 16 * CHUNK * bw * 4),
        name="mixer_gmlp",
    )(p, p, ws.astype(BF16), bias, ng.reshape(1, bw), nb.reshape(1, bw))


CONV_ROWS = 32


def _mix_b_kernel(ap_ref, ac_ref, an_ref, gp_ref, gc_ref, gn_ref, cw_ref, cb_ref, ng_ref, nb_ref, o_ref, ysh, yconv,
                  *, tiles_lat, tiles_tot, tmb):
    i = pl.program_id(0)
    first = jnp.logical_or(i == 0, i == tiles_lat)
    last = jnp.logical_or(i == tiles_lat - 1, i == tiles_tot - 1)
    glu = lambda a_ref, g_ref: a_ref[...].astype(F32) * _sigmoid(g_ref[...].astype(F32))
    yp = glu(ap_ref, gp_ref)
    yn = glu(an_ref, gn_ref)
    ysh[0, 0:CONV_HALO, :] = jnp.where(first, 0.0, yp)
    ysh[0, CONV_HALO:CONV_HALO + tmb, :] = glu(ac_ref, gc_ref)
    ysh[0, CONV_HALO + tmb:, :] = jnp.where(last, 0.0, yn)
    nrow = tmb + 2 * CONV_HALO
    for c0 in range(0, ysh.shape[2], LANES):
        base = ysh[0, :, c0:c0 + LANES]
        for b in range(1, SUBLANES):
            ysh[b, :, c0:c0 + LANES] = pltpu.roll(base, nrow - b, 0)
    off = CONV_HALO - CONV_K // 2

    bw = ysh.shape[2]
    groups = CONV_ROWS // SUBLANES

    def conv_block(rb, carry):
        r0 = pl.multiple_of(rb * CONV_ROWS, CONV_ROWS)
        acc = None
        for j in range(CONV_K):
            b = (off + j) % SUBLANES
            taps = ysh[b, pl.ds(r0 + (off + j - b), CONV_ROWS), :].reshape(groups, SUBLANES, bw)
            term = taps * cw_ref[j * SUBLANES:(j + 1) * SUBLANES, :][None]
            acc = term if acc is None else acc + term
        yconv[pl.ds(r0, CONV_ROWS), :] = acc.reshape(CONV_ROWS, bw)
        return carry

    lax.fori_loop(0, tmb // CONV_ROWS, conv_block, 0)

    def norm_group(rg, carry):
        r0 = pl.multiple_of(rg * SUBLANES, SUBLANES)
        y = yconv[pl.ds(r0, SUBLANES), :] + cb_ref[...]
        mu = jnp.mean(y, axis=-1, keepdims=True)
        yc = y - mu
        var = jnp.mean(yc * yc, axis=-1, keepdims=True)
        z = yc * lax.rsqrt(var + EPS) * ng_ref[...] + nb_ref[...]
        o_ref[pl.ds(r0, SUBLANES), :] = _silu(z).astype(o_ref.dtype)
        return carry

    lax.fori_loop(0, tmb // SUBLANES, norm_group, 0, unroll=8)


def _mix_b(p, cw, cb, ng, nb, bw, t_lat):
    m = p.shape[0]
    tmb = _pick(math.gcd(t_lat, m - t_lat), 256, CONV_HALO)
    hb = tmb // CONV_HALO
    nhalo = m // CONV_HALO
    a_col, g_col = 2, 3
    prev = lambda c: (lambda i: (jnp.maximum(i * hb - 1, 0), c))
    cur = lambda c: (lambda i: (i, c))
    nxt = lambda c: (lambda i: (jnp.minimum((i + 1) * hb, nhalo - 1), c))
    vec = pl.BlockSpec((SUBLANES, bw), lambda i: (0, 0))
    rep = lambda a: jnp.broadcast_to(a.reshape(-1, 1, bw), (a.size // bw, SUBLANES, bw)).reshape(-1, bw)
    return pl.pallas_call(
        functools.partial(_mix_b_kernel, tiles_lat=t_lat // tmb, tiles_tot=m // tmb, tmb=tmb),
        grid=(m // tmb,),
        in_specs=[
            pl.BlockSpec((CONV_HALO, bw), prev(a_col)),
            pl.BlockSpec((tmb, bw), cur(a_col)),
            pl.BlockSpec((CONV_HALO, bw), nxt(a_col)),
            pl.BlockSpec((CONV_HALO, bw), prev(g_col)),
            pl.BlockSpec((tmb, bw), cur(g_col)),
            pl.BlockSpec((CONV_HALO, bw), nxt(g_col)),
            pl.BlockSpec((CONV_K * SUBLANES, bw), lambda i: (0, 0)),
            vec, vec, vec,
        ],
        out_specs=pl.BlockSpec((tmb, bw), lambda i: (i, 0)),
        out_shape=jax.ShapeDtypeStruct((m, bw), BF16),
        scratch_shapes=[pltpu.VMEM((SUBLANES, tmb + 2 * CONV_HALO, bw), F32), pltpu.VMEM((tmb, bw), F32)],
        compiler_params=_params(("arbitrary",), 20 * tmb * bw * 4),
        name="mixer_conformer_conv",
    )(p, p, p, p, p, p, rep(cw), rep(cb), rep(ng), rep(nb))


def _rope_tables(t_lat):
    m = HEAD_DIM // 4
    freqs = ROPE_THETA ** (-jnp.arange(m, dtype=F32) / m)
    r = t_lat // GRID_W
    ang_r = jnp.repeat(jnp.arange(r, dtype=F32)[:, None] * freqs[None, :], GRID_W, axis=0)
    ang_c = jnp.tile(jnp.arange(GRID_W, dtype=F32)[:, None] * freqs[None, :], (r, 1))
    cos = jnp.concatenate([jnp.cos(ang_r), jnp.cos(ang_r), jnp.cos(ang_c), jnp.cos(ang_c)], axis=-1)
    sin = jnp.concatenate([-jnp.sin(ang_r), jnp.sin(ang_r), -jnp.sin(ang_c), jnp.sin(ang_c)], axis=-1)
    return cos, sin


def _rope(x, cos, sin):
    lane = lax.broadcasted_iota(jnp.int32, x.shape, 1)
    low = (lane % (HEAD_DIM // 2)) < (HEAD_DIM // 4)
    swapped = jnp.where(low, pltpu.roll(x, HEAD_DIM - HEAD_DIM // 4, 1), pltpu.roll(x, HEAD_DIM // 4, 1))
    return x * cos + swapped * sin


def _bias_kernel(rpb_ref, o_ref):
    n = GRID_W * GRID_W
    col = lax.broadcasted_iota(jnp.int32, (1, n), 1)
    q = col // GRID_W
    k = col % GRID_W
    dc = jnp.clip(k - q, -(WIN_W - 1), WIN_W - 1) + (WIN_W - 1)
    start = jnp.clip(q - WIN_W // 2, 0, GRID_W - WIN_W)
    in_win = jnp.logical_and(k >= start, k < start + WIN_W)
    val = jnp.zeros(o_ref.shape, F32)
    for d in range(2 * WIN_W - 1):
        val = jnp.where(dc == d, rpb_ref[:, d:d + 1], val)
    o_ref[...] = jnp.where(in_win, val * LOG2_E, NEG_BIG)


ATTN_QROWS = 4
ATTN_KROWS = ATTN_QROWS + WIN_H
ATTN_SHIFT = WIN_H // 2
ATTN_VARIANTS = WIN_H // ATTN_QROWS + 1


def _bias_table(rpb):
    depth, nh, ndr, ndc = rpb.shape
    rows = depth * nh * ndr
    toe = pl.pallas_call(
        _bias_kernel,
        out_shape=jax.ShapeDtypeStruct((rows, GRID_W * GRID_W), F32),
        compiler_params=_params((), 16 * rows * GRID_W * GRID_W),
        name="attention_bias_table",
    )(rpb.reshape(rows, ndc))
    toe = toe.reshape(depth, nh, ndr, GRID_W, GRID_W)
    masked = jnp.full((depth, nh, GRID_W, GRID_W), NEG_BIG, F32)

    def variant(block_off):
        out = []
        for qr in range(ATTN_QROWS):
            first = min(max(block_off + qr - ATTN_SHIFT, 0), ATTN_KROWS - WIN_H)
            cols = []
            for j in range(ATTN_KROWS):
                visible = first <= j < first + WIN_H
                cols.append(toe[:, :, j - qr - block_off + WIN_H - 1] if visible else masked)
            out.append(jnp.concatenate(cols, axis=-1))
        return jnp.concatenate(out, axis=-2)

    return jnp.stack([variant(off) for off in range(0, WIN_H + 1, ATTN_QROWS)], axis=2)


ATTN_UNROLL = 2
ATTN_PREP_ROWS = 256


def _softmax_pv(scores, values):
    mx = None
    for s in scores:
        smax = jnp.max(s, axis=-1, keepdims=True)
        mx = smax if mx is None else jnp.maximum(mx, smax)
    den = None
    acc = None
    for s, v in zip(scores, values):
        e = jnp.exp2(s - mx)
        esum = jnp.sum(e, axis=-1, keepdims=True)
        den = esum if den is None else den + esum
        pv = jnp.dot(e.astype(BF16), v, preferred_element_type=F32)
        acc = pv if acc is None else acc + pv
    return acc / den


def _dot_nt(a, b):
    return lax.dot_general(a, b, (((1,), (1,)), ((), ())), preferred_element_type=F32)


def _attn_kernel(q_ref, k_ref, v_ref, qc_ref, kc_ref, vc_ref, cos_ref, sin_ref, bias_ref, o_ref, kr_ref,
                 *, rows, t_lat, with_ctx):
    scale = HEAD_DIM ** -0.5 * LOG2_E
    kc = kc_ref[...]
    vc = vc_ref[...]
    nq = ATTN_QROWS * GRID_W
    nk = ATTN_KROWS * GRID_W

    def prep(c, carry):
        rs = pl.ds(pl.multiple_of(c * ATTN_PREP_ROWS, ATTN_PREP_ROWS), ATTN_PREP_ROWS)
        kr_ref[rs, :] = _rope(k_ref[rs, :].astype(F32), cos_ref[rs, :], sin_ref[rs, :]).astype(BF16)
        return carry

    lax.fori_loop(0, t_lat // ATTN_PREP_ROWS, prep, 0)

    def body(blk, carry):
        r0 = blk * ATTN_QROWS
        s0 = jnp.clip(r0 - ATTN_SHIFT, 0, rows - ATTN_KROWS)
        qs = pl.ds(pl.multiple_of(r0 * GRID_W, nq), nq)
        ks = pl.ds(pl.multiple_of(s0 * GRID_W, GRID_W), nk)
        q = q_ref[qs, :]
        q_rot = _rope(q.astype(F32), cos_ref[qs, :], sin_ref[qs, :]).astype(BF16)
        s_lat = _dot_nt(q_rot, kr_ref[ks, :]) * scale
        b = bias_ref[(r0 - s0) // ATTN_QROWS]
        s_lat = jnp.where(b > 0.5 * NEG_BIG, s_lat + b, NEG_BIG)
        s_ctx = _dot_nt(q, kc) * scale
        o = _softmax_pv((s_lat, s_ctx), (v_ref[ks, :], vc))
        o_ref[qs, :] = o.astype(o_ref.dtype)
        return carry

    lax.fori_loop(0, rows // ATTN_QROWS, body, 0, unroll=ATTN_UNROLL)
    if with_ctx:
        s = _dot_nt(qc_ref[...], kc) * scale
        o_ref[t_lat:, :] = _softmax_pv((s,), (vc,)).astype(o_ref.dtype)


def _attention(p, cos, sin, bias, layer, bw, t_lat, with_ctx):
    m = p.shape[0]
    n_ctx = m - t_lat
    nh = bw // HEAD_DIM
    rows = t_lat // GRID_W
    assert rows >= ATTN_KROWS and rows % ATTN_QROWS == 0 and t_lat % n_ctx == 0 and t_lat % ATTN_PREP_ROWS == 0
    m_out = m if with_ctx else t_lat
    q_col, k_col, v_col = 4 * nh, 5 * nh, 6 * nh
    lat = lambda c: pl.BlockSpec((t_lat, HEAD_DIM), lambda h: (0, c + h))
    ctx = lambda c: pl.BlockSpec((n_ctx, HEAD_DIM), lambda h: (t_lat // n_ctx, c + h))
    table = pl.BlockSpec((t_lat, HEAD_DIM), lambda h: (0, 0), pipeline_mode=pl.Buffered(1))
    nq, nk = ATTN_QROWS * GRID_W, ATTN_KROWS * GRID_W
    need = (2 * 3 * t_lat * HEAD_DIM * 2 + 2 * t_lat * HEAD_DIM * 4 + t_lat * HEAD_DIM * 2
            + 2 * m_out * HEAD_DIM * 2 + 2 * ATTN_VARIANTS * nq * nk * 4 + 8 * nq * (nk + n_ctx) * 4)
    return pl.pallas_call(
        functools.partial(_attn_kernel, rows=rows, t_lat=t_lat, with_ctx=with_ctx),
        grid=(nh,),
        in_specs=[lat(q_col), lat(k_col), lat(v_col), ctx(q_col), ctx(k_col), ctx(v_col), table, table,
                  pl.BlockSpec((None, None, ATTN_VARIANTS, nq, nk), lambda h: (layer, h, 0, 0, 0))],
        out_specs=pl.BlockSpec((m_out, HEAD_DIM), lambda h: (0, h)),
        out_shape=jax.ShapeDtypeStruct((m_out, bw), BF16),
        scratch_shapes=[pltpu.VMEM((t_lat, HEAD_DIM), BF16)],
        compiler_params=_params(("arbitrary",), need),
        name="neighbourhood_attention",
    )(p, p, p, p, p, p, cos, sin, bias)


def _scan_kernel(*refs, nh, reverse, readout):
    if readout:
        qd_ref, z_ref, i_ref, lb_ref, g_ref, other_ref, ng_ref, o_ref, st_ref = refs
    else:
        qd_ref, z_ref, i_ref, lb_ref, o_ref, st_ref = refs
    c = CHUNK

    @pl.when(pl.program_id(0) == 0)
    def _():
        st_ref[...] = jnp.zeros_like(st_ref)

    ri = lax.broadcasted_iota(jnp.int32, (c, 1), 0)
    ti = lax.broadcasted_iota(jnp.int32, (c, c), 0)
    si = lax.broadcasted_iota(jnp.int32, (c, c), 1)
    shift = SCAN_DIRECT.bit_length() - 1
    near = jnp.logical_and((ti >> shift) == (si >> shift), (si >= ti) if reverse else (si <= ti))
    widths = [1 << k for k in range(c.bit_length() - 1)]
    masks = [jnp.logical_and(((ti ^ si) >> (w.bit_length() - 1)) == 1,
                             ((ti & w) == 0) if reverse else ((ti & w) != 0)) for w in widths]

    for h in range(nh):
        sl = slice(h * HEAD_DIM, (h + 1) * HEAD_DIM)
        lb = lb_ref[:, sl]
        f = lb + (1.0 - lb) * _sigmoid(z_ref[:, sl])
        l = jnp.log(jnp.maximum(f, F_MIN)) * LOG2_E
        kg = 1.0 - f
        q = _silu(qd_ref[:, sl])
        v = i_ref[:, sl].astype(BF16)

        cf, tot = l, l
        att = None
        for w, mk in zip(widths, masks):
            if w == SCAN_DIRECT:
                e = (cf - l) if reverse else cf
                q_near = q * jnp.exp2(-e if reverse else e)
                k_near = kg * jnp.exp2(e if reverse else -e)
                att = jnp.where(near, _dot_nt(q_near.astype(BF16), k_near.astype(BF16)), 0.0)
            if w < SUBLANES:
                odd = (ri & w) != 0
                if w < SCAN_DIRECT:
                    x = None
                elif reverse:
                    x = jnp.where(odd, kg, q) * jnp.exp2(jnp.where(odd, cf - l, tot - cf + l))
                else:
                    x = jnp.where(odd, q, kg) * jnp.exp2(jnp.where(odd, cf, tot - cf))
                prev_tot = pltpu.roll(tot, w, 0)
                next_tot = pltpu.roll(tot, c - w, 0)
                cf = cf + jnp.where(odd, prev_tot, 0.0)
                tot = tot + jnp.where(odd, prev_tot, next_tot)
            else:
                xs, cfs, tots = [], [], []
                for k in range(c // w):
                    rows = slice(k * w, (k + 1) * w)
                    other = slice((k ^ 1) * w, ((k ^ 1) + 1) * w)
                    is_query = (k % 2 == 0) == reverse
                    if reverse:
                        arg = (tot[rows] - cf[rows] + l[rows]) if is_query else (cf[rows] - l[rows])
                    else:
                        arg = cf[rows] if is_query else (tot[rows] - cf[rows])
                    xs.append((q[rows] if is_query else kg[rows]) * jnp.exp2(arg))
                    cfs.append(cf[rows] + tot[other] if k % 2 else cf[rows])
                    tots.append(tot[rows] + tot[other])
                x = jnp.concatenate(xs, axis=0)
                cf = jnp.concatenate(cfs, axis=0)
                tot = jnp.concatenate(tots, axis=0)
            if x is not None:
                xb = x.astype(BF16)
                att = jnp.where(mk, _dot_nt(xb, xb), att)
        to_state = (tot - cf + l) if reverse else cf
        to_end = (cf - l) if reverse else (tot - cf)
        q_in = (q * jnp.exp2(to_state)).astype(BF16)
        k_out = (kg * jnp.exp2(to_end)).astype(BF16)

        st = st_ref[h]
        o = _dot_nt(q_in, st.astype(BF16)) + jnp.dot(att.astype(BF16), v, preferred_element_type=F32)
        st_ref[h] = st * jnp.exp2(tot[0:1, :]) + lax.dot_general(
            v, k_out, (((0,), (0,)), ((), ())), preferred_element_type=F32)
        if readout:
            o = o + other_ref[:, sl]
            o = o * lax.rsqrt(jnp.mean(o * o, axis=-1, keepdims=True) + EPS) * ng_ref[...]
            o_ref[:, sl] = (o * _silu(g_ref[:, sl])).astype(o_ref.dtype)
        else:
            o_ref[:, sl] = o


def _scan(p, lb, bw, t_lat, reverse, other=None, norm_g=None):
    m = p.shape[0]
    nh = bw // HEAD_DIM
    n_lat = t_lat // CHUNK
    n_all = m // CHUNK
    n_ctx = n_all - n_lat
    readout = other is not None
    if reverse:
        row_blk = lambda j: jnp.where(j < n_ctx, n_all - 1 - j, n_lat - 1 - (j - n_ctx))
    else:
        row_blk = lambda j: jnp.where(j < n_ctx, n_lat + j, j - n_ctx)
    blk = lambda c: pl.BlockSpec((CHUNK, bw), lambda j: (row_blk(j), c))
    qd_col, zf_col, zb_col, i_col, g_col = 0, 1, 2, 3, 4
    in_specs = [blk(qd_col), blk(zb_col if reverse else zf_col), blk(i_col), pl.BlockSpec((1, bw), lambda j: (0, 0))]
    args = [p, p, p, lb.reshape(1, bw)]
    if readout:
        in_specs += [blk(g_col), blk(0), pl.BlockSpec((1, HEAD_DIM), lambda j: (0, 0))]
        args += [p, other, norm_g.reshape(1, HEAD_DIM)]
    return pl.pallas_call(
        functools.partial(_scan_kernel, nh=nh, reverse=reverse, readout=readout),
        grid=(n_all,),
        in_specs=in_specs,
        out_specs=pl.BlockSpec((CHUNK, bw), lambda j: (row_blk(j), 0)),
        out_shape=jax.ShapeDtypeStruct((m, bw), BF16 if readout else F32),
        scratch_shapes=[pltpu.VMEM((nh, HEAD_DIM, HEAD_DIM), F32)],
        compiler_params=_params(("arbitrary",), 48 * CHUNK * bw * 4),
        name="hgrn2_scan_bwd" if reverse else "hgrn2_scan_fwd",
    )(*args)


def kernel(x, c, ctx, c_ctx, w_ada, b_ada, norm1_g, norm2_g, w_in, a_norm_g, a_norm_b, a_ws, a_bs, b_conv_w,
           b_conv_b, b_norm_g, b_norm_b, c_rpb, d_lb_logits, d_norm_g, w_branch, w_gate, b_gate, w_out, w_ffn1,
           w_ffn3, w_ffn2, final_g):
    batch, t_lat, d = x.shape
    n_ctx = ctx.shape[1]
    depth = w_ada.shape[0]
    bw = d // N_BRANCH
    assert batch == 1 and t_lat % GRID_W == 0 and t_lat % CHUNK == 0 and n_ctx % CHUNK == 0

    xa = jnp.concatenate([x[0], ctx[0]], axis=0)
    cond = jnp.zeros((SUBLANES, d), F32).at[0].set(c[0]).at[1].set(c_ctx)
    mod = _modulation(cond, w_ada, b_ada)

    prob = jax.nn.softmax(d_lb_logits.astype(F32), axis=1)
    lbound = jnp.cumsum(prob, axis=1) - prob[:, :1]

    cos, sin = _rope_tables(t_lat)
    bias = _bias_table(c_rpb)
    w_gate_b, w_branch_b, w2_b = (w.astype(BF16) for w in (w_gate, w_branch, w_ffn2))

    for l in range(depth):
        last = l == depth - 1
        m = t_lat if last else t_lat + n_ctx
        h = _norm_mod(xa, norm1_g[l], mod, l, 0, 1, t_lat)
        p = _ws_matmul(h, w_in, l, 0, 7 * bw, BF16)
        pd = _ws_matmul(h, w_in, l, 7 * bw, 5 * bw, F32)
        ya = _mix_a(p, a_ws[l], a_bs[l], a_norm_g[l], a_norm_b[l], bw)
        yb = _mix_b(p, b_conv_w[l], b_conv_b[l], b_norm_g[l], b_norm_b[l], bw, t_lat)
        yc = _attention(p, cos, sin, bias, l, bw, t_lat, with_ctx=not last)
        o_bwd = _scan(pd, lbound[1, l], bw, t_lat, reverse=True)
        yd = _scan(pd, lbound[0, l], bw, t_lat, reverse=False, other=o_bwd, norm_g=d_norm_g[l])
        merged = _gated_merge(h, (ya, yb, yc, yd), w_gate_b, b_gate, w_branch_b, l, m)
        x1 = _ws_matmul_residual(merged, w_out, l, xa, mod, 2, t_lat, m)
        hf = _norm_mod(x1, norm2_g[l], mod, l, 3, 4, t_lat)
        act = _ws_swiglu_up(hf, w_ffn1, w_ffn3, l, m)
        xa = _matmul_residual(act, w2_b, l, x1, mod, 5, t_lat, m)

    return _final_norm(xa[:t_lat], final_g)[None]
```

```python
import functools
import math

import jax
import jax.numpy as jnp
from jax import lax
from jax.experimental import pallas as pl
from jax.experimental.pallas import tpu as pltpu

F32 = jnp.float32
BF16 = jnp.bfloat16

HEAD_DIM = 128
CHUNK = 128
GRID_W = 64
WIN_H = 8
WIN_W = 16
CONV_K = 31
N_BRANCH = 4
ROPE_THETA = 10000.0
EPS = 1e-6
F_MIN = 1e-6
NEG_BIG = -1e30
LOG2_E = 1.4426950408889634
SCAN_DIRECT = 4
assert (SCAN_DIRECT - 1) * -math.log2(F_MIN) < 120 <= (2 * SCAN_DIRECT - 1) * -math.log2(F_MIN)

V7X_VMEM_BYTES = 64 * 2**20
VMEM_CAP = V7X_VMEM_BYTES - 6 * 2**20
SUBLANES = 8
LANES = 128
CONV_HALO = 16
MXU_COLS = 256

WS_ROWS, WS_COLS = 1408, 512
WS_PAIR_COLS = MXU_COLS
AS_ROWS = 768
AS_ROWS_WIDE, AS_COLS_WIDE = 512, 512
ROW_TILE = 512
CONV_TILE = 256


def _pick(n, target, quantum):
    best = None
    for t in range(quantum, min(n, target) + 1, quantum):
        if n % t == 0:
            best = t
    assert best is not None, (n, target, quantum)
    return best


def _params(sem, need_bytes):
    limit = int(min(VMEM_CAP, max(need_bytes * 5 // 4, 16 * 2**20)))
    return pltpu.CompilerParams(dimension_semantics=sem, vmem_limit_bytes=limit)


def _sigmoid(x):
    return jax.nn.sigmoid(x)


def _silu(x):
    return x * jax.nn.sigmoid(x)


def _gelu(x):
    return 0.5 * x * (1.0 + lax.erf(x * 0.7071067811865476))


def _row_select(rows_is_ctx, p_ref):
    return jnp.where(rows_is_ctx, p_ref[1:2, :], p_ref[0:1, :])


def _mod_kernel(cond_ref, w_ref, b_ref, o_ref):
    s = _silu(cond_ref[...]).astype(BF16)
    o_ref[...] = jnp.dot(s, w_ref[...].astype(BF16), preferred_element_type=F32) + b_ref[...]


def _modulation(cond, w_ada, b_ada):
    depth, d, n = w_ada.shape
    tn = _pick(n, WS_COLS, LANES)
    need = 2 * d * tn * 4 + d * tn * 2
    return pl.pallas_call(
        _mod_kernel,
        grid=(depth, n // tn),
        in_specs=[
            pl.BlockSpec((SUBLANES, d), lambda l, j: (0, 0)),
            pl.BlockSpec((None, d, tn), lambda l, j: (l, 0, j)),
            pl.BlockSpec((None, 1, tn), lambda l, j: (l, 0, j)),
        ],
        out_specs=pl.BlockSpec((None, SUBLANES, tn), lambda l, j: (l, 0, j)),
        out_shape=jax.ShapeDtypeStruct((depth, SUBLANES, n), F32),
        compiler_params=_params(("arbitrary", "arbitrary"), need),
        name="adaln_modulation",
    )(cond, w_ada, b_ada.reshape(depth, 1, n))


NORM_ROWS = 8


def _row_rsqrt_stats(x_ref, rs_ref, tm):
    def stats(r, carry):
        r0 = pl.multiple_of(r * NORM_ROWS, NORM_ROWS)
        x = x_ref[pl.ds(r0, NORM_ROWS), :]
        rs_ref[pl.ds(r0, NORM_ROWS), :] = lax.rsqrt(jnp.mean(x * x, axis=-1, keepdims=True) + EPS)
        return carry

    lax.fori_loop(0, tm // NORM_ROWS, stats, 0, unroll=4)


def _norm_mod_kernel(x_ref, g_ref, sh_ref, sc_ref, o_ref, rs_ref, gs_ref, *, t_lat, tm):
    _row_rsqrt_stats(x_ref, rs_ref, tm)
    which = (pl.program_id(0) * tm >= t_lat).astype(jnp.int32)
    shape = (NORM_ROWS, x_ref.shape[1])
    gs_ref[0:NORM_ROWS, :] = jnp.broadcast_to(g_ref[...] * (1.0 + sc_ref[pl.ds(which, 1), :]), shape)
    gs_ref[NORM_ROWS:, :] = jnp.broadcast_to(sh_ref[pl.ds(which, 1), :], shape)

    def scale(r, carry):
        r0 = pl.multiple_of(r * NORM_ROWS, NORM_ROWS)
        y = x_ref[pl.ds(r0, NORM_ROWS), :] * rs_ref[pl.ds(r0, NORM_ROWS), :]
        o_ref[pl.ds(r0, NORM_ROWS), :] = (y * gs_ref[0:NORM_ROWS, :] + gs_ref[NORM_ROWS:, :]).astype(o_ref.dtype)
        return carry

    lax.fori_loop(0, tm // NORM_ROWS, scale, 0, unroll=8)


def _norm_mod(x, g, mod, layer, shift_blk, scale_blk, t_lat):
    m, d = x.shape
    tm = _pick(math.gcd(t_lat, m), ROW_TILE, NORM_ROWS)
    return pl.pallas_call(
        functools.partial(_norm_mod_kernel, t_lat=t_lat, tm=tm),
        grid=(m // tm,),
        in_specs=[
            pl.BlockSpec((tm, d), lambda i: (i, 0)),
            pl.BlockSpec((1, d), lambda i: (0, 0)),
            pl.BlockSpec((None, SUBLANES, d), lambda i: (layer, 0, shift_blk)),
            pl.BlockSpec((None, SUBLANES, d), lambda i: (layer, 0, scale_blk)),
        ],
        out_specs=pl.BlockSpec((tm, d), lambda i: (i, 0)),
        out_shape=jax.ShapeDtypeStruct((m, d), BF16),
        scratch_shapes=[pltpu.VMEM((tm, 1), F32), pltpu.VMEM((2 * NORM_ROWS, d), F32)],
        compiler_params=_params(("arbitrary",), 2 * tm * d * 6 + tm * LANES * 4 + 4 * SUBLANES * d * 4),
        name="rmsnorm_modulate",
    )(x, g.reshape(1, d), mod, mod)


def _final_norm_kernel(x_ref, g_ref, o_ref, rs_ref, *, tm):
    _row_rsqrt_stats(x_ref, rs_ref, tm)

    def scale(r, carry):
        r0 = pl.multiple_of(r * NORM_ROWS, NORM_ROWS)
        o_ref[pl.ds(r0, NORM_ROWS), :] = x_ref[pl.ds(r0, NORM_ROWS), :] * rs_ref[pl.ds(r0, NORM_ROWS), :] * g_ref[...]
        return carry

    lax.fori_loop(0, tm // NORM_ROWS, scale, 0, unroll=8)


def _final_norm(x, g):
    m, d = x.shape
    tm = _pick(m, ROW_TILE, NORM_ROWS)
    return pl.pallas_call(
        functools.partial(_final_norm_kernel, tm=tm),
        grid=(m // tm,),
        in_specs=[pl.BlockSpec((tm, d), lambda i: (i, 0)), pl.BlockSpec((NORM_ROWS, d), lambda i: (0, 0))],
        out_specs=pl.BlockSpec((tm, d), lambda i: (i, 0)),
        out_shape=jax.ShapeDtypeStruct((m, d), F32),
        scratch_shapes=[pltpu.VMEM((tm, 1), F32)],
        compiler_params=_params(("arbitrary",), 2 * tm * d * 8 + tm * LANES * 4),
        name="final_rmsnorm",
    )(x, jnp.broadcast_to(g.reshape(1, d), (NORM_ROWS, d)))


def _ws_mm_kernel(a_ref, w_ref, o_ref, wb):
    @pl.when(pl.program_id(1) == 0)
    def _():
        wb[...] = w_ref[...].astype(BF16)

    o_ref[...] = jnp.dot(a_ref[...], wb[...], preferred_element_type=F32).astype(o_ref.dtype)


def _ws_matmul(a, w, layer, col0, n, out_dtype):
    m, k = a.shape
    tm = _pick(m, WS_ROWS, LANES)
    tn = _pick(math.gcd(n, col0) if col0 else n, WS_COLS, LANES)
    jb = col0 // tn
    osz = jnp.dtype(out_dtype).itemsize
    need = 2 * (tm * k * 2 + k * tn * 4 + tm * tn * osz) + k * tn * 2 + tm * tn * 4
    return pl.pallas_call(
        _ws_mm_kernel,
        grid=(n // tn, m // tm),
        in_specs=[pl.BlockSpec((tm, k), lambda j, i: (i, 0)),
                  pl.BlockSpec((None, k, tn), lambda j, i: (layer, 0, jb + j))],
        out_specs=pl.BlockSpec((tm, tn), lambda j, i: (i, j)),
        out_shape=jax.ShapeDtypeStruct((m, n), out_dtype),
        scratch_shapes=[pltpu.VMEM((k, tn), BF16)],
        compiler_params=_params(("arbitrary", "arbitrary"), need),
        name="input_projection",
    )(a, w)


def _ws_resid_kernel(a_ref, w_ref, x_ref, gate_ref, o_ref, wb, *, t_lat, tm):
    @pl.when(pl.program_id(1) == 0)
    def _():
        wb[...] = w_ref[...].astype(BF16)

    rows = pl.program_id(1) * tm + lax.broadcasted_iota(jnp.int32, (tm, 1), 0)
    gate = _row_select(rows >= t_lat, gate_ref)
    o_ref[...] = x_ref[...] + gate * jnp.dot(a_ref[...], wb[...], preferred_element_type=F32)


def _ws_matmul_residual(a, w, layer, x, mod, gate_blk, t_lat, m):
    k = a.shape[1]
    n = w.shape[2]
    tm = _pick(m, WS_ROWS, LANES)
    tn = _pick(n, WS_COLS, LANES)
    nb = n // tn
    need = 2 * (tm * k * 2 + k * tn * 4 + 2 * tm * tn * 4) + k * tn * 2 + tm * tn * 4
    return pl.pallas_call(
        functools.partial(_ws_resid_kernel, t_lat=t_lat, tm=tm),
        grid=(nb, m // tm),
        in_specs=[
            pl.BlockSpec((tm, k), lambda j, i: (i, 0)),
            pl.BlockSpec((None, k, tn), lambda j, i: (layer, 0, j)),
            pl.BlockSpec((tm, tn), lambda j, i: (i, j)),
            pl.BlockSpec((None, SUBLANES, tn), lambda j, i: (layer, 0, gate_blk * nb + j)),
        ],
        out_specs=pl.BlockSpec((tm, tn), lambda j, i: (i, j)),
        out_shape=jax.ShapeDtypeStruct((m, n), F32),
        scratch_shapes=[pltpu.VMEM((k, tn), BF16)],
        compiler_params=_params(("arbitrary", "arbitrary"), need),
        name="output_projection",
    )(a, w, x, mod)


def _ws_swiglu_kernel(a_ref, w1_ref, w3_ref, o_ref, w1b, w3b):
    @pl.when(pl.program_id(1) == 0)
    def _():
        w1b[...] = w1_ref[...].astype(BF16)
        w3b[...] = w3_ref[...].astype(BF16)

    a = a_ref[...]
    u = jnp.dot(a, w1b[...], preferred_element_type=F32)
    g = jnp.dot(a, w3b[...], preferred_element_type=F32)
    o_ref[...] = (_silu(u) * g).astype(o_ref.dtype)


def _ws_swiglu_up(a, w1, w3, layer, m):
    k = a.shape[1]
    n = w1.shape[2]
    tm = _pick(m, WS_ROWS, LANES)
    tn = _pick(n, WS_PAIR_COLS, LANES)
    need = 2 * (tm * k * 2 + 2 * k * tn * 4 + tm * tn * 2) + 2 * k * tn * 2 + 3 * tm * tn * 4
    w_spec = pl.BlockSpec((None, k, tn), lambda j, i: (layer, 0, j))
    return pl.pallas_call(
        _ws_swiglu_kernel,
        grid=(n // tn, m // tm),
        in_specs=[pl.BlockSpec((tm, k), lambda j, i: (i, 0)), w_spec, w_spec],
        out_specs=pl.BlockSpec((tm, tn), lambda j, i: (i, j)),
        out_shape=jax.ShapeDtypeStruct((m, n), BF16),
        scratch_shapes=[pltpu.VMEM((k, tn), BF16), pltpu.VMEM((k, tn), BF16)],
        compiler_params=_params(("arbitrary", "arbitrary"), need),
        name="swiglu_up",
    )(a, w1, w3)


def _mm_resid_kernel(a_ref, w_ref, x_ref, gate_ref, o_ref, *, t_lat, tm):
    rows = pl.program_id(0) * tm + lax.broadcasted_iota(jnp.int32, (tm, 1), 0)
    gate = _row_select(rows >= t_lat, gate_ref)
    o_ref[...] = x_ref[...] + gate * jnp.dot(a_ref[...], w_ref[...], preferred_element_type=F32)


def _matmul_residual(a, w, layer, x, mod, gate_blk, t_lat, m):
    k = a.shape[1]
    n = w.shape[2]
    tm = _pick(m, AS_ROWS, MXU_COLS)
    tn = _pick(n, MXU_COLS if tm > AS_ROWS_WIDE else AS_COLS_WIDE, LANES)
    need = 2 * (tm * k * 2 + k * tn * 2 + 2 * tm * tn * 4) + tm * tn * 4
    nb = n // tn
    return pl.pallas_call(
        functools.partial(_mm_resid_kernel, t_lat=t_lat, tm=tm),
        grid=(m // tm, nb),
        in_specs=[
            pl.BlockSpec((tm, k), lambda i, j: (i, 0)),
            pl.BlockSpec((None, k, tn), lambda i, j: (layer, 0, j)),
            pl.BlockSpec((tm, tn), lambda i, j: (i, j)),
            pl.BlockSpec((None, SUBLANES, tn), lambda i, j: (layer, 0, gate_blk * nb + j)),
        ],
        out_specs=pl.BlockSpec((tm, tn), lambda i, j: (i, j)),
        out_shape=jax.ShapeDtypeStruct((m, n), F32),
        compiler_params=_params(("arbitrary", "arbitrary"), need),
        name="ffn_down",
    )(a, w, x, mod)


def _merge_kernel(h_ref, ya_ref, yb_ref, yc_ref, yd_ref, wg0, wg1, wg2, wg3, bg0, bg1, bg2, bg3, wb_ref, o_ref):
    h = h_ref[...]
    acc = None
    for i, (y_ref, wg, bg) in enumerate(((ya_ref, wg0, bg0), (yb_ref, wg1, bg1), (yc_ref, wg2, bg2), (yd_ref, wg3, bg3))):
        gate = _sigmoid(jnp.dot(h, wg[...], preferred_element_type=F32) + bg[...])
        term = gate * jnp.dot(y_ref[...], wb_ref[i], preferred_element_type=F32)
        acc = term if acc is None else acc + term
    o_ref[...] = acc.astype(o_ref.dtype)


def _gated_merge(h, ys, w_gate, b_gate, w_branch, layer, m):
    k = h.shape[1]
    depth, nbr, bw, n = w_branch.shape
    tm = _pick(m, AS_ROWS, MXU_COLS)
    tn = _pick(n, MXU_COLS, LANES)
    nb = n // tn
    need = 2 * (tm * k * 2 + nbr * tm * bw * 2 + nbr * k * tn * 2 + nbr * bw * tn * 2 + tm * tn * 2) + 4 * tm * tn * 4
    y_spec = pl.BlockSpec((tm, bw), lambda i, j: (i, 0))
    wg_specs = [pl.BlockSpec((None, k, tn), functools.partial(lambda i, j, b: (layer, 0, b * nb + j), b=b))
                for b in range(nbr)]
    bg_specs = [pl.BlockSpec((None, 1, tn), functools.partial(lambda i, j, b: (layer, 0, b * nb + j), b=b))
                for b in range(nbr)]
    bg = b_gate.reshape(depth, 1, -1)
    return pl.pallas_call(
        _merge_kernel,
        grid=(m // tm, nb),
        in_specs=[pl.BlockSpec((tm, k), lambda i, j: (i, 0)), y_spec, y_spec, y_spec, y_spec]
        + wg_specs + bg_specs + [pl.BlockSpec((None, nbr, bw, tn), lambda i, j: (layer, 0, 0, j))],
        out_specs=pl.BlockSpec((tm, tn), lambda i, j: (i, j)),
        out_shape=jax.ShapeDtypeStruct((m, n), BF16),
        compiler_params=_params(("arbitrary", "arbitrary"), need),
        name="gated_merge",
    )(h, *ys, w_gate, w_gate, w_gate, w_gate, bg, bg, bg, bg, w_branch)


def _mix_a_kernel(pu_ref, pv_ref, ws_ref, bias_ref, ng_ref, nb_ref, o_ref, *, nh):
    zu = _gelu(pu_ref[...].astype(F32))
    zv = _gelu(pv_ref[...].astype(F32))
    mu = jnp.mean(zv, axis=-1, keepdims=True)
    zc = zv - mu
    var = jnp.mean(zc * zc, axis=-1, keepdims=True)
    v = (zc * lax.rsqrt(var + EPS) * ng_ref[...] + nb_ref[...]).astype(BF16)
    for g in range(nh):
        sl = slice(g * HEAD_DIM, (g + 1) * HEAD_DIM)
        s = jnp.dot(ws_ref[g], v[:, sl], preferred_element_type=F32) + bias_ref[:, sl]
        o_ref[:, sl] = (zu[:, sl] * s).astype(o_ref.dtype)


def _mix_a(p, ws, bs, ng, nb, bw):
    m = p.shape[0]
    nh = bw // HEAD_DIM
    bias = jnp.repeat(bs.T, HEAD_DIM, axis=1)
    return pl.pallas_call(
        functools.partial(_mix_a_kernel, nh=nh),
        grid=(m // CHUNK,),
        in_specs=[
            pl.BlockSpec((CHUNK, bw), lambda i: (i, 0)),
            pl.BlockSpec((CHUNK, bw), lambda i: (i, 1)),
            pl.BlockSpec((nh, CHUNK, CHUNK), lambda i: (0, 0, 0)),
            pl.BlockSpec((CHUNK, bw), lambda i: (0, 0)),
            pl.BlockSpec((1, bw), lambda i: (0, 0)),
            pl.BlockSpec((1, bw), lambda i: (0, 0)),
        ],
        out_specs=pl.BlockSpec((CHUNK, bw), lambda i: (i, 0)),
        out_shape=jax.ShapeDtypeStruct((m, bw), BF16),
        compiler_params=_params(("arbitrary",), 16 * CHUNK * bw * 4),
        name="mixer_gmlp",
    )(p, p, ws.astype(BF16), bias, ng.reshape(1, bw), nb.reshape(1, bw))


CONV_ROWS = 32


def _mix_b_kernel(ap_ref, ac_ref, an_ref, gp_ref, gc_ref, gn_ref, cw_ref, cb_ref, ng_ref, nb_ref, o_ref, ysh, yconv,
                  *, tiles_lat, tiles_tot, tmb):
    i = pl.program_id(0)
    first = jnp.logical_or(i == 0, i == tiles_lat)
    last = jnp.logical_or(i == tiles_lat - 1, i == tiles_tot - 1)
    glu = lambda a_ref, g_ref: a_ref[...].astype(F32) * _sigmoid(g_ref[...].astype(F32))
    yp = glu(ap_ref, gp_ref)
    yn = glu(an_ref, gn_ref)
    ysh[0, 0:CONV_HALO, :] = jnp.where(first, 0.0, yp)
    ysh[0, CONV_HALO:CONV_HALO + tmb, :] = glu(ac_ref, gc_ref)
    ysh[0, CONV_HALO + tmb:, :] = jnp.where(last, 0.0, yn)
    nrow = tmb + 2 * CONV_HALO
    for c0 in range(0, ysh.shape[2], LANES):
        base = ysh[0, :, c0:c0 + LANES]
        for b in range(1, SUBLANES):
            ysh[b, :, c0:c0 + LANES] = pltpu.roll(base, nrow - b, 0)
    off = CONV_HALO - CONV_K // 2

    bw = ysh.shape[2]
    groups = CONV_ROWS // SUBLANES

    def conv_block(rb, carry):
        r0 = pl.multiple_of(rb * CONV_ROWS, CONV_ROWS)
        acc = [None] * groups
        for j in range(CONV_K):
            b = (off + j) % SUBLANES
            wj = cw_ref[j * SUBLANES:(j + 1) * SUBLANES, :]
            for g in range(groups):
                term = ysh[b, pl.ds(r0 + (off + j - b + g * SUBLANES), SUBLANES), :] * wj
                acc[g] = term if acc[g] is None else acc[g] + term
        for g in range(groups):
            yconv[pl.ds(r0 + g * SUBLANES, SUBLANES), :] = acc[g]
        return carry

    lax.fori_loop(0, tmb // CONV_ROWS, conv_block, 0)

    def norm_group(rg, carry):
        r0 = pl.multiple_of(rg * SUBLANES, SUBLANES)
        y = yconv[pl.ds(r0, SUBLANES), :] + cb_ref[...]
        mu = jnp.mean(y, axis=-1, keepdims=True)
        yc = y - mu
        var = jnp.mean(yc * yc, axis=-1, keepdims=True)
        z = yc * lax.rsqrt(var + EPS) * ng_ref[...] + nb_ref[...]
        o_ref[pl.ds(r0, SUBLANES), :] = _silu(z).astype(o_ref.dtype)
        return carry

    lax.fori_loop(0, tmb // SUBLANES, norm_group, 0, unroll=8)


def _mix_b(p, cw, cb, ng, nb, bw, t_lat):
    m = p.shape[0]
    tmb = _pick(math.gcd(t_lat, m - t_lat), CONV_TILE, CONV_ROWS)
    hb = tmb // CONV_HALO
    nhalo = m // CONV_HALO
    a_col, g_col = 2, 3
    prev = lambda c: (lambda i: (jnp.maximum(i * hb - 1, 0), c))
    cur = lambda c: (lambda i: (i, c))
    nxt = lambda c: (lambda i: (jnp.minimum((i + 1) * hb, nhalo - 1), c))
    vec = pl.BlockSpec((SUBLANES, bw), lambda i: (0, 0))
    rep = lambda a: jnp.broadcast_to(a.reshape(-1, 1, bw), (a.size // bw, SUBLANES, bw)).reshape(-1, bw)
    return pl.pallas_call(
        functools.partial(_mix_b_kernel, tiles_lat=t_lat // tmb, tiles_tot=m // tmb, tmb=tmb),
        grid=(m // tmb,),
        in_specs=[
            pl.BlockSpec((CONV_HALO, bw), prev(a_col)),
            pl.BlockSpec((tmb, bw), cur(a_col)),
            pl.BlockSpec((CONV_HALO, bw), nxt(a_col)),
            pl.BlockSpec((CONV_HALO, bw), prev(g_col)),
            pl.BlockSpec((tmb, bw), cur(g_col)),
            pl.BlockSpec((CONV_HALO, bw), nxt(g_col)),
            pl.BlockSpec((CONV_K * SUBLANES, bw), lambda i: (0, 0)),
            vec, vec, vec,
        ],
        out_specs=pl.BlockSpec((tmb, bw), lambda i: (i, 0)),
        out_shape=jax.ShapeDtypeStruct((m, bw), BF16),
        scratch_shapes=[pltpu.VMEM((SUBLANES, tmb + 2 * CONV_HALO, bw), F32), pltpu.VMEM((tmb, bw), F32)],
        compiler_params=_params(("arbitrary",), 20 * tmb * bw * 4),
        name="mixer_conformer_conv",
    )(p, p, p, p, p, p, rep(cw), rep(cb), rep(ng), rep(nb))


def _rope_tables(t_lat):
    m = HEAD_DIM // 4
    freqs = ROPE_THETA ** (-jnp.arange(m, dtype=F32) / m)
    r = t_lat // GRID_W
    ang_r = jnp.repeat(jnp.arange(r, dtype=F32)[:, None] * freqs[None, :], GRID_W, axis=0)
    ang_c = jnp.tile(jnp.arange(GRID_W, dtype=F32)[:, None] * freqs[None, :], (r, 1))
    cos = jnp.concatenate([jnp.cos(ang_r), jnp.cos(ang_r), jnp.cos(ang_c), jnp.cos(ang_c)], axis=-1)
    sin = jnp.concatenate([-jnp.sin(ang_r), jnp.sin(ang_r), -jnp.sin(ang_c), jnp.sin(ang_c)], axis=-1)
    return cos, sin


def _rope(x, cos, sin):
    lane = lax.broadcasted_iota(jnp.int32, x.shape, 1)
    low = (lane % (HEAD_DIM // 2)) < (HEAD_DIM // 4)
    swapped = jnp.where(low, pltpu.roll(x, HEAD_DIM - HEAD_DIM // 4, 1), pltpu.roll(x, HEAD_DIM // 4, 1))
    return x * cos + swapped * sin


def _bias_kernel(rpb_ref, o_ref):
    n = GRID_W * GRID_W
    col = lax.broadcasted_iota(jnp.int32, (1, n), 1)
    q = col // GRID_W
    k = col % GRID_W
    dc = jnp.clip(k - q, -(WIN_W - 1), WIN_W - 1) + (WIN_W - 1)
    start = jnp.clip(q - WIN_W // 2, 0, GRID_W - WIN_W)
    in_win = jnp.logical_and(k >= start, k < start + WIN_W)
    val = jnp.zeros(o_ref.shape, F32)
    for d in range(2 * WIN_W - 1):
        val = jnp.where(dc == d, rpb_ref[:, d:d + 1], val)
    o_ref[...] = jnp.where(in_win, val * LOG2_E, NEG_BIG)


ATTN_QROWS = 4
ATTN_KROWS = ATTN_QROWS + WIN_H
ATTN_SHIFT = WIN_H // 2
ATTN_VARIANTS = WIN_H // ATTN_QROWS + 1


def _bias_table(rpb):
    depth, nh, ndr, ndc = rpb.shape
    rows = depth * nh * ndr
    toe = pl.pallas_call(
        _bias_kernel,
        out_shape=jax.ShapeDtypeStruct((rows, GRID_W * GRID_W), F32),
        compiler_params=_params((), 16 * rows * GRID_W * GRID_W),
        name="attention_bias_table",
    )(rpb.reshape(rows, ndc))
    toe = toe.reshape(depth, nh, ndr, GRID_W, GRID_W)
    masked = jnp.full((depth, nh, GRID_W, GRID_W), NEG_BIG, F32)

    def variant(block_off):
        out = []
        for qr in range(ATTN_QROWS):
            first = min(max(block_off + qr - ATTN_SHIFT, 0), ATTN_KROWS - WIN_H)
            cols = []
            for j in range(ATTN_KROWS):
                visible = first <= j < first + WIN_H
                cols.append(toe[:, :, j - qr - block_off + WIN_H - 1] if visible else masked)
            out.append(jnp.concatenate(cols, axis=-1))
        return jnp.concatenate(out, axis=-2)

    return jnp.stack([variant(off) for off in range(0, WIN_H + 1, ATTN_QROWS)], axis=2)


ATTN_UNROLL = 4
ATTN_PREP_ROWS = 256


def _softmax_pv(scores, values):
    mx = None
    for s in scores:
        smax = jnp.max(s, axis=-1, keepdims=True)
        mx = smax if mx is None else jnp.maximum(mx, smax)
    den = None
    acc = None
    for s, v in zip(scores, values):
        e = jnp.exp2(s - mx)
        esum = jnp.sum(e, axis=-1, keepdims=True)
        den = esum if den is None else den + esum
        pv = jnp.dot(e.astype(BF16), v, preferred_element_type=F32)
        acc = pv if acc is None else acc + pv
    return acc / den


def _dot_nt(a, b):
    return lax.dot_general(a, b, (((1,), (1,)), ((), ())), preferred_element_type=F32)


def _attn_kernel(q_ref, k_ref, v_ref, qc_ref, kc_ref, vc_ref, cos_ref, sin_ref, bias_ref, o_ref, kr_ref,
                 *, rows, t_lat, with_ctx):
    scale = HEAD_DIM ** -0.5 * LOG2_E
    kc = kc_ref[...]
    vc = vc_ref[...]
    nq = ATTN_QROWS * GRID_W
    nk = ATTN_KROWS * GRID_W

    def prep(c, carry):
        rs = pl.ds(pl.multiple_of(c * ATTN_PREP_ROWS, ATTN_PREP_ROWS), ATTN_PREP_ROWS)
        kr_ref[rs, :] = _rope(k_ref[rs, :].astype(F32), cos_ref[rs, :], sin_ref[rs, :]).astype(BF16)
        return carry

    lax.fori_loop(0, t_lat // ATTN_PREP_ROWS, prep, 0)

    def body(blk, carry):
        r0 = blk * ATTN_QROWS
        s0 = jnp.clip(r0 - ATTN_SHIFT, 0, rows - ATTN_KROWS)
        qs = pl.ds(pl.multiple_of(r0 * GRID_W, nq), nq)
        ks = pl.ds(pl.multiple_of(s0 * GRID_W, GRID_W), nk)
        q = q_ref[qs, :]
        q_rot = _rope(q.astype(F32), cos_ref[qs, :], sin_ref[qs, :]).astype(BF16)
        s_lat = _dot_nt(q_rot, kr_ref[ks, :]) * scale
        b = bias_ref[(r0 - s0) // ATTN_QROWS]
        s_lat = jnp.where(b > 0.5 * NEG_BIG, s_lat + b, NEG_BIG)
        s_ctx = _dot_nt(q, kc) * scale
        o = _softmax_pv((s_lat, s_ctx), (v_ref[ks, :], vc))
        o_ref[qs, :] = o.astype(o_ref.dtype)
        return carry

    lax.fori_loop(0, rows // ATTN_QROWS, body, 0, unroll=ATTN_UNROLL)
    if with_ctx:
        s = _dot_nt(qc_ref[...], kc) * scale
        o_ref[t_lat:, :] = _softmax_pv((s,), (vc,)).astype(o_ref.dtype)


def _attention(p, cos, sin, bias, layer, bw, t_lat, with_ctx):
    m = p.shape[0]
    n_ctx = m - t_lat
    nh = bw // HEAD_DIM
    rows = t_lat // GRID_W
    assert rows >= ATTN_KROWS and rows % ATTN_QROWS == 0 and t_lat % n_ctx == 0 and t_lat % ATTN_PREP_ROWS == 0
    m_out = m if with_ctx else t_lat
    q_col, k_col, v_col = 4 * nh, 5 * nh, 6 * nh
    lat = lambda c: pl.BlockSpec((t_lat, HEAD_DIM), lambda h: (0, c + h))
    ctx = lambda c: pl.BlockSpec((n_ctx, HEAD_DIM), lambda h: (t_lat // n_ctx, c + h))
    table = pl.BlockSpec((t_lat, HEAD_DIM), lambda h: (0, 0), pipeline_mode=pl.Buffered(1))
    nq, nk = ATTN_QROWS * GRID_W, ATTN_KROWS * GRID_W
    need = (2 * 3 * t_lat * HEAD_DIM * 2 + 2 * t_lat * HEAD_DIM * 4 + t_lat * HEAD_DIM * 2
            + 2 * m_out * HEAD_DIM * 2 + 2 * ATTN_VARIANTS * nq * nk * 4 + 8 * nq * (nk + n_ctx) * 4)
    return pl.pallas_call(
        functools.partial(_attn_kernel, rows=rows, t_lat=t_lat, with_ctx=with_ctx),
        grid=(nh,),
        in_specs=[lat(q_col), lat(k_col), lat(v_col), ctx(q_col), ctx(k_col), ctx(v_col), table, table,
                  pl.BlockSpec((None, None, ATTN_VARIANTS, nq, nk), lambda h: (layer, h, 0, 0, 0))],
        out_specs=pl.BlockSpec((m_out, HEAD_DIM), lambda h: (0, h)),
        out_shape=jax.ShapeDtypeStruct((m_out, bw), BF16),
        scratch_shapes=[pltpu.VMEM((t_lat, HEAD_DIM), BF16)],
        compiler_params=_params(("arbitrary",), need),
        name="neighbourhood_attention",
    )(p, p, p, p, p, p, cos, sin, bias)


def _scan_kernel(*refs, nh, reverse, readout):
    if readout:
        qd_ref, z_ref, i_ref, lb_ref, g_ref, other_ref, ng_ref, o_ref, st_ref = refs
    else:
        qd_ref, z_ref, i_ref, lb_ref, o_ref, st_ref = refs
    c = CHUNK

    @pl.when(pl.program_id(0) == 0)
    def _():
        st_ref[...] = jnp.zeros_like(st_ref)

    ri = lax.broadcasted_iota(jnp.int32, (c, 1), 0)
    ti = lax.broadcasted_iota(jnp.int32, (c, c), 0)
    si = lax.broadcasted_iota(jnp.int32, (c, c), 1)
    shift = SCAN_DIRECT.bit_length() - 1
    near = jnp.logical_and((ti >> shift) == (si >> shift), (si >= ti) if reverse else (si <= ti))
    widths = [1 << k for k in range(c.bit_length() - 1)]
    masks = [jnp.logical_and(((ti ^ si) >> (w.bit_length() - 1)) == 1,
                             ((ti & w) == 0) if reverse else ((ti & w) != 0)) for w in widths]

    for h in range(nh):
        sl = slice(h * HEAD_DIM, (h + 1) * HEAD_DIM)
        lb = lb_ref[:, sl]
        f = lb + (1.0 - lb) * _sigmoid(z_ref[:, sl])
        l = jnp.log(jnp.maximum(f, F_MIN)) * LOG2_E
        kg = 1.0 - f
        q = _silu(qd_ref[:, sl])
        v = i_ref[:, sl].astype(BF16)

        cf, tot = l, l
        att = None
        for w, mk in zip(widths, masks):
            if w == SCAN_DIRECT:
                e = (cf - l) if reverse else cf
                q_near = q * jnp.exp2(-e if reverse else e)
                k_near = kg * jnp.exp2(e if reverse else -e)
                att = jnp.where(near, _dot_nt(q_near.astype(BF16), k_near.astype(BF16)), 0.0)
            if w < SUBLANES:
                odd = (ri & w) != 0
                if w < SCAN_DIRECT:
                    x = None
                elif reverse:
                    x = jnp.where(odd, kg, q) * jnp.exp2(jnp.where(odd, cf - l, tot - cf + l))
                else:
                    x = jnp.where(odd, q, kg) * jnp.exp2(jnp.where(odd, cf, tot - cf))
                prev_tot = pltpu.roll(tot, w, 0)
                next_tot = pltpu.roll(tot, c - w, 0)
                cf = cf + jnp.where(odd, prev_tot, 0.0)
                tot = tot + jnp.where(odd, prev_tot, next_tot)
            else:
                xs, cfs, tots = [], [], []
                for k in range(c // w):
                    rows = slice(k * w, (k + 1) * w)
                    other = slice((k ^ 1) * w, ((k ^ 1) + 1) * w)
                    is_query = (k % 2 == 0) == reverse
                    if reverse:
                        arg = (tot[rows] - cf[rows] + l[rows]) if is_query else (cf[rows] - l[rows])
                    else:
                        arg = cf[rows] if is_query else (tot[rows] - cf[rows])
                    xs.append((q[rows] if is_query else kg[rows]) * jnp.exp2(arg))
                    cfs.append(cf[rows] + tot[other] if k % 2 else cf[rows])
                    tots.append(tot[rows] + tot[other])
                x = jnp.concatenate(xs, axis=0)
                cf = jnp.concatenate(cfs, axis=0)
                tot = jnp.concatenate(tots, axis=0)
            if x is not None:
                xb = x.astype(BF16)
                att = jnp.where(mk, _dot_nt(xb, xb), att)
        to_state = (tot - cf + l) if reverse else cf
        to_end = (cf - l) if reverse else (tot - cf)
        q_in = (q * jnp.exp2(to_state)).astype(BF16)
        k_out = (kg * jnp.exp2(to_end)).astype(BF16)

        st = st_ref[h]
        o = _dot_nt(q_in, st.astype(BF16)) + jnp.dot(att.astype(BF16), v, preferred_element_type=F32)
        st_ref[h] = st * jnp.exp2(tot[0:1, :]) + lax.dot_general(
            v, k_out, (((0,), (0,)), ((), ())), preferred_element_type=F32)
        if readout:
            o = o + other_ref[:, sl]
            o = o * lax.rsqrt(jnp.mean(o * o, axis=-1, keepdims=True) + EPS) * ng_ref[...]
            o_ref[:, sl] = (o * _silu(g_ref[:, sl])).astype(o_ref.dtype)
        else:
            o_ref[:, sl] = o


def _scan(p, lb, bw, t_lat, reverse, other=None, norm_g=None):
    m = p.shape[0]
    nh = bw // HEAD_DIM
    n_lat = t_lat // CHUNK
    n_all = m // CHUNK
    n_ctx = n_all - n_lat
    readout = other is not None
    if reverse:
        row_blk = lambda j: jnp.where(j < n_ctx, n_all - 1 - j, n_lat - 1 - (j - n_ctx))
    else:
        row_blk = lambda j: jnp.where(j < n_ctx, n_lat + j, j - n_ctx)
    blk = lambda c: pl.BlockSpec((CHUNK, bw), lambda j: (row_blk(j), c))
    qd_col, zf_col, zb_col, i_col, g_col = 0, 1, 2, 3, 4
    in_specs = [blk(qd_col), blk(zb_col if reverse else zf_col), blk(i_col), pl.BlockSpec((1, bw), lambda j: (0, 0))]
    args = [p, p, p, lb.reshape(1, bw)]
    if readout:
        in_specs += [blk(g_col), blk(0), pl.BlockSpec((1, HEAD_DIM), lambda j: (0, 0))]
        args += [p, other, norm_g.reshape(1, HEAD_DIM)]
    return pl.pallas_call(
        functools.partial(_scan_kernel, nh=nh, reverse=reverse, readout=readout),
        grid=(n_all,),
        in_specs=in_specs,
        out_specs=pl.BlockSpec((CHUNK, bw), lambda j: (row_blk(j), 0)),
        out_shape=jax.ShapeDtypeStruct((m, bw), BF16 if readout else F32),
        scratch_shapes=[pltpu.VMEM((nh, HEAD_DIM, HEAD_DIM), F32)],
        compiler_params=_params(("arbitrary",), 48 * CHUNK * bw * 4),
        name="hgrn2_scan_bwd" if reverse else "hgrn2_scan_fwd",
    )(*args)


def kernel(x, c, ctx, c_ctx, w_ada, b_ada, norm1_g, norm2_g, w_in, a_norm_g, a_norm_b, a_ws, a_bs, b_conv_w,
           b_conv_b, b_norm_g, b_norm_b, c_rpb, d_lb_logits, d_norm_g, w_branch, w_gate, b_gate, w_out, w_ffn1,
           w_ffn3, w_ffn2, final_g):
    batch, t_lat, d = x.shape
    n_ctx = ctx.shape[1]
    depth = w_ada.shape[0]
    bw = d // N_BRANCH
    assert batch == 1 and t_lat % GRID_W == 0 and t_lat % CHUNK == 0 and n_ctx % CHUNK == 0

    xa = jnp.concatenate([x[0], ctx[0]], axis=0)
    cond = jnp.zeros((SUBLANES, d), F32).at[0].set(c[0]).at[1].set(c_ctx)
    mod = _modulation(cond, w_ada, b_ada)

    prob = jax.nn.softmax(d_lb_logits.astype(F32), axis=1)
    lbound = jnp.cumsum(prob, axis=1) - prob[:, :1]

    cos, sin = _rope_tables(t_lat)
    bias = _bias_table(c_rpb)
    w_gate_b, w_branch_b, w2_b = (w.astype(BF16) for w in (w_gate, w_branch, w_ffn2))

    for l in range(depth):
        last = l == depth - 1
        m = t_lat if last else t_lat + n_ctx
        h = _norm_mod(xa, norm1_g[l], mod, l, 0, 1, t_lat)
        p = _ws_matmul(h, w_in, l, 0, 7 * bw, BF16)
        pd = _ws_matmul(h, w_in, l, 7 * bw, 5 * bw, F32)
        ya = _mix_a(p, a_ws[l], a_bs[l], a_norm_g[l], a_norm_b[l], bw)
        yb = _mix_b(p, b_conv_w[l], b_conv_b[l], b_norm_g[l], b_norm_b[l], bw, t_lat)
        yc = _attention(p, cos, sin, bias, l, bw, t_lat, with_ctx=not last)
        o_bwd = _scan(pd, lbound[1, l], bw, t_lat, reverse=True)
        yd = _scan(pd, lbound[0, l], bw, t_lat, reverse=False, other=o_bwd, norm_g=d_norm_g[l])
        merged = _gated_merge(h, (ya, yb, yc, yd), w_gate_b, b_gate, w_branch_b, l, m)
        x1 = _ws_matmul_residual(merged, w_out, l, xa, mod, 2, t_lat, m)
        hf = _norm_mod(x1, norm2_g[l], mod, l, 3, 4, t_lat)
        act = _ws_swiglu_up(hf, w_ffn1, w_ffn3, l, m)
        xa = _matmul_residual(act, w2_b, l, x1, mod, 5, t_lat, m)

    return _final_norm(xa[:t_lat], final_g)[None]
```

```python
import functools
import math

import jax
import jax.numpy as jnp
import numpy as np
from jax import lax
from jax.experimental import pallas as pl
from jax.experimental.pallas import tpu as pltpu

F32 = jnp.float32
BF16 = jnp.bfloat16

HEAD_DIM = 128
CHUNK = 128
GRID_W = 64
WIN_H = 8
WIN_W = 16
CONV_K = 31
N_BRANCH = 4
ROPE_THETA = 10000.0
EPS = 1e-6
F_MIN = 1e-6
NEG_BIG = -1e30
LOG2_E = 1.4426950408889634
SCAN_DIRECT = 4
assert (SCAN_DIRECT - 1) * -math.log2(F_MIN) < 120 <= (2 * SCAN_DIRECT - 1) * -math.log2(F_MIN)

V7X_VMEM_BYTES = 64 * 2**20
VMEM_CAP = V7X_VMEM_BYTES - 6 * 2**20
SUBLANES = 8
LANES = 128
CONV_HALO = 16
MXU_COLS = 256

WS_ROWS, WS_COLS = 1408, 512
WS_PAIR_COLS = MXU_COLS
AS_ROWS = 768
AS_ROWS_WIDE, AS_COLS_WIDE = 512, 512
ROW_TILE = 512
CONV_TILE = 256


def _pick(n, target, quantum):
    best = None
    for t in range(quantum, min(n, target) + 1, quantum):
        if n % t == 0:
            best = t
    assert best is not None, (n, target, quantum)
    return best


def _params(sem, need_bytes):
    limit = int(min(VMEM_CAP, max(need_bytes * 5 // 4, 16 * 2**20)))
    return pltpu.CompilerParams(dimension_semantics=sem, vmem_limit_bytes=limit)


def _sigmoid(x):
    return jax.nn.sigmoid(x)


def _silu(x):
    return x * jax.nn.sigmoid(x)


def _gelu(x):
    return 0.5 * x * (1.0 + lax.erf(x * 0.7071067811865476))


def _row_select(rows_is_ctx, p_ref):
    return jnp.where(rows_is_ctx, p_ref[1:2, :], p_ref[0:1, :])


def _mod_kernel(cond_ref, w_ref, b_ref, o_ref):
    s = _silu(cond_ref[...]).astype(BF16)
    o_ref[...] = jnp.dot(s, w_ref[...].astype(BF16), preferred_element_type=F32) + b_ref[...]


def _modulation(cond, w_ada, b_ada):
    depth, d, n = w_ada.shape
    tn = _pick(n, WS_COLS, LANES)
    need = 2 * d * tn * 4 + d * tn * 2
    return pl.pallas_call(
        _mod_kernel,
        grid=(depth, n // tn),
        in_specs=[
            pl.BlockSpec((SUBLANES, d), lambda l, j: (0, 0)),
            pl.BlockSpec((None, d, tn), lambda l, j: (l, 0, j)),
            pl.BlockSpec((None, 1, tn), lambda l, j: (l, 0, j)),
        ],
        out_specs=pl.BlockSpec((None, SUBLANES, tn), lambda l, j: (l, 0, j)),
        out_shape=jax.ShapeDtypeStruct((depth, SUBLANES, n), F32),
        compiler_params=_params(("arbitrary", "arbitrary"), need),
        name="adaln_modulation",
    )(cond, w_ada, b_ada.reshape(depth, 1, n))


NORM_ROWS = 8


def _row_rsqrt_stats(x_ref, rs_ref, tm):
    def stats(r, carry):
        r0 = pl.multiple_of(r * NORM_ROWS, NORM_ROWS)
        x = x_ref[pl.ds(r0, NORM_ROWS), :]
        rs_ref[pl.ds(r0, NORM_ROWS), :] = lax.rsqrt(jnp.mean(x * x, axis=-1, keepdims=True) + EPS)
        return carry

    lax.fori_loop(0, tm // NORM_ROWS, stats, 0, unroll=4)


def _norm_mod_tile(x_ref, which, g_ref, sh_ref, sc_ref, o_ref, rs_ref, gs_ref, tm):
    _row_rsqrt_stats(x_ref, rs_ref, tm)
    shape = (NORM_ROWS, x_ref.shape[1])
    gs_ref[0:NORM_ROWS, :] = jnp.broadcast_to(g_ref[...] * (1.0 + sc_ref[pl.ds(which, 1), :]), shape)
    gs_ref[NORM_ROWS:, :] = jnp.broadcast_to(sh_ref[pl.ds(which, 1), :], shape)

    def scale(r, carry):
        r0 = pl.multiple_of(r * NORM_ROWS, NORM_ROWS)
        y = x_ref[pl.ds(r0, NORM_ROWS), :] * rs_ref[pl.ds(r0, NORM_ROWS), :]
        o_ref[pl.ds(r0, NORM_ROWS), :] = (y * gs_ref[0:NORM_ROWS, :] + gs_ref[NORM_ROWS:, :]).astype(o_ref.dtype)
        return carry

    lax.fori_loop(0, tm // NORM_ROWS, scale, 0, unroll=8)


def _norm_mod_kernel(x_ref, g_ref, sh_ref, sc_ref, o_ref, rs_ref, gs_ref, *, t_lat, tm):
    which = (pl.program_id(0) * tm >= t_lat).astype(jnp.int32)
    _norm_mod_tile(x_ref, which, g_ref, sh_ref, sc_ref, o_ref, rs_ref, gs_ref, tm)


def _norm_mod2_kernel(x_ref, c_ref, g_ref, sh_ref, sc_ref, o_ref, rs_ref, gs_ref, *, lat_tiles, tm):
    @pl.when(pl.program_id(0) < lat_tiles)
    def _():
        _norm_mod_tile(x_ref, 0, g_ref, sh_ref, sc_ref, o_ref, rs_ref, gs_ref, tm)

    @pl.when(pl.program_id(0) >= lat_tiles)
    def _():
        _norm_mod_tile(c_ref, 1, g_ref, sh_ref, sc_ref, o_ref, rs_ref, gs_ref, tm)


def _norm_mod(x, g, mod, layer, shift_blk, scale_blk, t_lat, ctx=None):
    d = x.shape[1]
    m = x.shape[0] + (0 if ctx is None else ctx.shape[0])
    tm = _pick(math.gcd(t_lat, m), ROW_TILE, NORM_ROWS)
    lat_tiles = t_lat // tm
    param_specs = [
        pl.BlockSpec((1, d), lambda i: (0, 0)),
        pl.BlockSpec((None, SUBLANES, d), lambda i: (layer, 0, shift_blk)),
        pl.BlockSpec((None, SUBLANES, d), lambda i: (layer, 0, scale_blk)),
    ]
    if ctx is None:
        kern = functools.partial(_norm_mod_kernel, t_lat=t_lat, tm=tm)
        row_specs, rows = [pl.BlockSpec((tm, d), lambda i: (i, 0))], (x,)
    else:
        kern = functools.partial(_norm_mod2_kernel, lat_tiles=lat_tiles, tm=tm)
        row_specs = [pl.BlockSpec((tm, d), lambda i: (jnp.minimum(i, lat_tiles - 1), 0)),
                     pl.BlockSpec((tm, d), lambda i: (jnp.maximum(i - lat_tiles, 0), 0))]
        rows = (x, ctx)
    return pl.pallas_call(
        kern,
        grid=(m // tm,),
        in_specs=row_specs + param_specs,
        out_specs=pl.BlockSpec((tm, d), lambda i: (i, 0)),
        out_shape=jax.ShapeDtypeStruct((m, d), BF16),
        scratch_shapes=[pltpu.VMEM((tm, 1), F32), pltpu.VMEM((2 * NORM_ROWS, d), F32)],
        compiler_params=_params(("arbitrary",), 2 * tm * d * 10 + tm * LANES * 4 + 4 * SUBLANES * d * 4),
        name="rmsnorm_modulate",
    )(*rows, g.reshape(1, d), mod, mod)


def _final_norm_kernel(x_ref, g_ref, o_ref, rs_ref, *, tm):
    _row_rsqrt_stats(x_ref, rs_ref, tm)

    def scale(r, carry):
        r0 = pl.multiple_of(r * NORM_ROWS, NORM_ROWS)
        o_ref[pl.ds(r0, NORM_ROWS), :] = x_ref[pl.ds(r0, NORM_ROWS), :] * rs_ref[pl.ds(r0, NORM_ROWS), :] * g_ref[...]
        return carry

    lax.fori_loop(0, tm // NORM_ROWS, scale, 0, unroll=8)


def _final_norm(x, g):
    m, d = x.shape
    tm = _pick(m, ROW_TILE, NORM_ROWS)
    return pl.pallas_call(
        functools.partial(_final_norm_kernel, tm=tm),
        grid=(m // tm,),
        in_specs=[pl.BlockSpec((tm, d), lambda i: (i, 0)), pl.BlockSpec((NORM_ROWS, d), lambda i: (0, 0))],
        out_specs=pl.BlockSpec((tm, d), lambda i: (i, 0)),
        out_shape=jax.ShapeDtypeStruct((m, d), F32),
        scratch_shapes=[pltpu.VMEM((tm, 1), F32)],
        compiler_params=_params(("arbitrary",), 2 * tm * d * 8 + tm * LANES * 4),
        name="final_rmsnorm",
    )(x, jnp.broadcast_to(g.reshape(1, d), (NORM_ROWS, d)))


def _ws_mm_kernel(a_ref, w_ref, o_ref, wb):
    @pl.when(pl.program_id(1) == 0)
    def _():
        wb[...] = w_ref[...].astype(BF16)

    o_ref[...] = jnp.dot(a_ref[...], wb[...], preferred_element_type=F32).astype(o_ref.dtype)


def _ws_matmul(a, w, layer, col0, n, out_dtype):
    m, k = a.shape
    tm = _pick(m, WS_ROWS, LANES)
    tn = _pick(math.gcd(n, col0) if col0 else n, WS_COLS, LANES)
    jb = col0 // tn
    osz = jnp.dtype(out_dtype).itemsize
    need = 2 * (tm * k * 2 + k * tn * 4 + tm * tn * osz) + k * tn * 2 + tm * tn * 4
    return pl.pallas_call(
        _ws_mm_kernel,
        grid=(n // tn, m // tm),
        in_specs=[pl.BlockSpec((tm, k), lambda j, i: (i, 0)),
                  pl.BlockSpec((None, k, tn), lambda j, i: (layer, 0, jb + j))],
        out_specs=pl.BlockSpec((tm, tn), lambda j, i: (i, j)),
        out_shape=jax.ShapeDtypeStruct((m, n), out_dtype),
        scratch_shapes=[pltpu.VMEM((k, tn), BF16)],
        compiler_params=_params(("arbitrary", "arbitrary"), need),
        name="input_projection",
    )(a, w)


def _ws_resid_kernel(a_ref, w_ref, x_ref, gate_ref, o_ref, wb, *, t_lat, tm):
    @pl.when(pl.program_id(1) == 0)
    def _():
        wb[...] = w_ref[...].astype(BF16)

    rows = pl.program_id(1) * tm + lax.broadcasted_iota(jnp.int32, (tm, 1), 0)
    gate = _row_select(rows >= t_lat, gate_ref)
    o_ref[...] = x_ref[...] + gate * jnp.dot(a_ref[...], wb[...], preferred_element_type=F32)


def _ws_resid2_kernel(a_ref, w_ref, x_ref, c_ref, gate_ref, o_ref, wb, *, lat_rows_last):
    @pl.when(pl.program_id(1) == 0)
    def _():
        wb[...] = w_ref[...].astype(BF16)

    y = jnp.dot(a_ref[...], wb[...], preferred_element_type=F32)
    last = pl.num_programs(1) - 1

    @pl.when(pl.program_id(1) < last)
    def _():
        o_ref[...] = x_ref[...] + gate_ref[0:1, :] * y

    @pl.when(pl.program_id(1) == last)
    def _():
        k = lat_rows_last
        o_ref[0:k, :] = x_ref[0:k, :] + gate_ref[0:1, :] * y[0:k]
        o_ref[k:, :] = c_ref[...] + gate_ref[1:2, :] * y[k:]


def _ws_matmul_residual(a, w, layer, x, mod, gate_blk, t_lat, m, ctx=None):
    k = a.shape[1]
    n = w.shape[2]
    tm = _pick(m, min(WS_ROWS, t_lat), LANES)
    tn = _pick(n, WS_COLS, LANES)
    nb = n // tn
    need = 2 * (tm * k * 2 + k * tn * 4 + 2 * tm * tn * 4) + k * tn * 2 + tm * tn * 4
    specs = [
        pl.BlockSpec((tm, k), lambda j, i: (i, 0)),
        pl.BlockSpec((None, k, tn), lambda j, i: (layer, 0, j)),
        pl.BlockSpec((tm, tn), lambda j, i: (i, j)),
    ]
    if ctx is None:
        kern = functools.partial(_ws_resid_kernel, t_lat=t_lat, tm=tm)
        rows = (x,)
    else:
        n_ctx = ctx.shape[0]
        assert m == t_lat + n_ctx and n_ctx <= tm and (tm - n_ctx) % SUBLANES == 0
        kern = functools.partial(_ws_resid2_kernel, lat_rows_last=tm - n_ctx)
        specs.append(pl.BlockSpec((n_ctx, tn), lambda j, i: (0, j)))
        rows = (x, ctx)
        need += 2 * n_ctx * tn * 4
    specs.append(pl.BlockSpec((None, SUBLANES, tn), lambda j, i: (layer, 0, gate_blk * nb + j)))
    return pl.pallas_call(
        kern,
        grid=(nb, m // tm),
        in_specs=specs,
        out_specs=pl.BlockSpec((tm, tn), lambda j, i: (i, j)),
        out_shape=jax.ShapeDtypeStruct((m, n), F32),
        scratch_shapes=[pltpu.VMEM((k, tn), BF16)],
        compiler_params=_params(("arbitrary", "arbitrary"), need),
        name="output_projection",
    )(a, w, *rows, mod)


def _ws_swiglu_kernel(a_ref, w1_ref, w3_ref, o_ref, w1b, w3b):
    @pl.when(pl.program_id(1) == 0)
    def _():
        w1b[...] = w1_ref[...].astype(BF16)
        w3b[...] = w3_ref[...].astype(BF16)

    a = a_ref[...]
    u = jnp.dot(a, w1b[...], preferred_element_type=F32)
    g = jnp.dot(a, w3b[...], preferred_element_type=F32)
    o_ref[...] = (_silu(u) * g).astype(o_ref.dtype)


def _ws_swiglu_up(a, w1, w3, layer, m):
    k = a.shape[1]
    n = w1.shape[2]
    tm = _pick(m, WS_ROWS, LANES)
    tn = _pick(n, WS_PAIR_COLS, LANES)
    need = 2 * (tm * k * 2 + 2 * k * tn * 4 + tm * tn * 2) + 2 * k * tn * 2 + 3 * tm * tn * 4
    w_spec = pl.BlockSpec((None, k, tn), lambda j, i: (layer, 0, j))
    return pl.pallas_call(
        _ws_swiglu_kernel,
        grid=(n // tn, m // tm),
        in_specs=[pl.BlockSpec((tm, k), lambda j, i: (i, 0)), w_spec, w_spec],
        out_specs=pl.BlockSpec((tm, tn), lambda j, i: (i, j)),
        out_shape=jax.ShapeDtypeStruct((m, n), BF16),
        scratch_shapes=[pltpu.VMEM((k, tn), BF16), pltpu.VMEM((k, tn), BF16)],
        compiler_params=_params(("arbitrary", "arbitrary"), need),
        name="swiglu_up",
    )(a, w1, w3)


def _mm_resid_kernel(a_ref, w_ref, x_ref, gate_ref, o_ref, *, t_lat, tm):
    rows = pl.program_id(0) * tm + lax.broadcasted_iota(jnp.int32, (tm, 1), 0)
    gate = _row_select(rows >= t_lat, gate_ref)
    o_ref[...] = x_ref[...] + gate * jnp.dot(a_ref[...], w_ref[...], preferred_element_type=F32)


def _matmul_residual(a, w, layer, x, mod, gate_blk, t_lat, m):
    k = a.shape[1]
    n = w.shape[2]
    tm = _pick(m, AS_ROWS, MXU_COLS)
    tn = _pick(n, MXU_COLS if tm > AS_ROWS_WIDE else AS_COLS_WIDE, LANES)
    need = 2 * (tm * k * 2 + k * tn * 2 + 2 * tm * tn * 4) + tm * tn * 4
    nb = n // tn
    return pl.pallas_call(
        functools.partial(_mm_resid_kernel, t_lat=t_lat, tm=tm),
        grid=(m // tm, nb),
        in_specs=[
            pl.BlockSpec((tm, k), lambda i, j: (i, 0)),
            pl.BlockSpec((None, k, tn), lambda i, j: (layer, 0, j)),
            pl.BlockSpec((tm, tn), lambda i, j: (i, j)),
            pl.BlockSpec((None, SUBLANES, tn), lambda i, j: (layer, 0, gate_blk * nb + j)),
        ],
        out_specs=pl.BlockSpec((tm, tn), lambda i, j: (i, j)),
        out_shape=jax.ShapeDtypeStruct((m, n), F32),
        compiler_params=_params(("arbitrary", "arbitrary"), need),
        name="ffn_down",
    )(a, w, x, mod)


def _merge_kernel(h_ref, ya_ref, yb_ref, yc_ref, yd_ref, wg0, wg1, wg2, wg3, bg0, bg1, bg2, bg3, wb_ref, o_ref):
    h = h_ref[...]
    acc = None
    for i, (y_ref, wg, bg) in enumerate(((ya_ref, wg0, bg0), (yb_ref, wg1, bg1), (yc_ref, wg2, bg2), (yd_ref, wg3, bg3))):
        gate = _sigmoid(jnp.dot(h, wg[...], preferred_element_type=F32) + bg[...])
        term = gate * jnp.dot(y_ref[...], wb_ref[i], preferred_element_type=F32)
        acc = term if acc is None else acc + term
    o_ref[...] = acc.astype(o_ref.dtype)


def _gated_merge(h, ys, w_gate, b_gate, w_branch, layer, m):
    k = h.shape[1]
    depth, nbr, bw, n = w_branch.shape
    tm = _pick(m, AS_ROWS, MXU_COLS)
    tn = _pick(n, MXU_COLS, LANES)
    nb = n // tn
    need = 2 * (tm * k * 2 + nbr * tm * bw * 2 + nbr * k * tn * 2 + nbr * bw * tn * 2 + tm * tn * 2) + 4 * tm * tn * 4
    y_spec = pl.BlockSpec((tm, bw), lambda i, j: (i, 0))
    wg_specs = [pl.BlockSpec((None, k, tn), functools.partial(lambda i, j, b: (layer, 0, b * nb + j), b=b))
                for b in range(nbr)]
    bg_specs = [pl.BlockSpec((None, 1, tn), functools.partial(lambda i, j, b: (layer, 0, b * nb + j), b=b))
                for b in range(nbr)]
    bg = b_gate.reshape(depth, 1, -1)
    return pl.pallas_call(
        _merge_kernel,
        grid=(m // tm, nb),
        in_specs=[pl.BlockSpec((tm, k), lambda i, j: (i, 0)), y_spec, y_spec, y_spec, y_spec]
        + wg_specs + bg_specs + [pl.BlockSpec((None, nbr, bw, tn), lambda i, j: (layer, 0, 0, j))],
        out_specs=pl.BlockSpec((tm, tn), lambda i, j: (i, j)),
        out_shape=jax.ShapeDtypeStruct((m, n), BF16),
        compiler_params=_params(("arbitrary", "arbitrary"), need),
        name="gated_merge",
    )(h, *ys, w_gate, w_gate, w_gate, w_gate, bg, bg, bg, bg, w_branch)


def _mix_a_kernel(pu_ref, pv_ref, ws_ref, bias_ref, ng_ref, nb_ref, o_ref, *, nh):
    zu = _gelu(pu_ref[...].astype(F32))
    zv = _gelu(pv_ref[...].astype(F32))
    mu = jnp.mean(zv, axis=-1, keepdims=True)
    zc = zv - mu
    var = jnp.mean(zc * zc, axis=-1, keepdims=True)
    v = (zc * lax.rsqrt(var + EPS) * ng_ref[...] + nb_ref[...]).astype(BF16)
    for g in range(nh):
        sl = slice(g * HEAD_DIM, (g + 1) * HEAD_DIM)
        s = jnp.dot(ws_ref[g], v[:, sl], preferred_element_type=F32) + bias_ref[:, sl]
        o_ref[:, sl] = (zu[:, sl] * s).astype(o_ref.dtype)


def _mix_a(p, ws, bs, ng, nb, bw):
    m = p.shape[0]
    nh = bw // HEAD_DIM
    bias = jnp.repeat(bs.T, HEAD_DIM, axis=1)
    return pl.pallas_call(
        functools.partial(_mix_a_kernel, nh=nh),
        grid=(m // CHUNK,),
        in_specs=[
            pl.BlockSpec((CHUNK, bw), lambda i: (i, 0)),
            pl.BlockSpec((CHUNK, bw), lambda i: (i, 1)),
            pl.BlockSpec((nh, CHUNK, CHUNK), lambda i: (0, 0, 0)),
            pl.BlockSpec((CHUNK, bw), lambda i: (0, 0)),
            pl.BlockSpec((1, bw), lambda i: (0, 0)),
            pl.BlockSpec((1, bw), lambda i: (0, 0)),
        ],
        out_specs=pl.BlockSpec((CHUNK, bw), lambda i: (i, 0)),
        out_shape=jax.ShapeDtypeStruct((m, bw), BF16),
        compiler_params=_params(("arbitrary",), 16 * CHUNK * bw * 4),
        name="mixer_gmlp",
    )(p, p, ws.astype(BF16), bias, ng.reshape(1, bw), nb.reshape(1, bw))


CONV_ROWS = 32


def _mix_b_kernel(ap_ref, ac_ref, an_ref, gp_ref, gc_ref, gn_ref, cw_ref, cb_ref, ng_ref, nb_ref, o_ref, ysh, yconv,
                  *, tiles_lat, tiles_tot, tmb):
    i = pl.program_id(0)
    first = jnp.logical_or(i == 0, i == tiles_lat)
    last = jnp.logical_or(i == tiles_lat - 1, i == tiles_tot - 1)
    glu = lambda a_ref, g_ref: a_ref[...].astype(F32) * _sigmoid(g_ref[...].astype(F32))
    yp = glu(ap_ref, gp_ref)
    yn = glu(an_ref, gn_ref)
    ysh[0, 0:CONV_HALO, :] = jnp.where(first, 0.0, yp)
    ysh[0, CONV_HALO:CONV_HALO + tmb, :] = glu(ac_ref, gc_ref)
    ysh[0, CONV_HALO + tmb:, :] = jnp.where(last, 0.0, yn)
    nrow = tmb + 2 * CONV_HALO
    for c0 in range(0, ysh.shape[2], LANES):
        base = ysh[0, :, c0:c0 + LANES]
        for b in range(1, SUBLANES):
            ysh[b, :, c0:c0 + LANES] = pltpu.roll(base, nrow - b, 0)
    off = CONV_HALO - CONV_K // 2

    bw = ysh.shape[2]
    groups = CONV_ROWS // SUBLANES

    def conv_block(rb, carry):
        r0 = pl.multiple_of(rb * CONV_ROWS, CONV_ROWS)
        acc = [None] * groups
        for j in range(CONV_K):
            b = (off + j) % SUBLANES
            wj = cw_ref[j * SUBLANES:(j + 1) * SUBLANES, :]
            for g in range(groups):
                term = ysh[b, pl.ds(r0 + (off + j - b + g * SUBLANES), SUBLANES), :] * wj
                acc[g] = term if acc[g] is None else acc[g] + term
        for g in range(groups):
            yconv[pl.ds(r0 + g * SUBLANES, SUBLANES), :] = acc[g]
        return carry

    lax.fori_loop(0, tmb // CONV_ROWS, conv_block, 0)

    def norm_group(rg, carry):
        r0 = pl.multiple_of(rg * SUBLANES, SUBLANES)
        y = yconv[pl.ds(r0, SUBLANES), :] + cb_ref[...]
        mu = jnp.mean(y, axis=-1, keepdims=True)
        yc = y - mu
        var = jnp.mean(yc * yc, axis=-1, keepdims=True)
        z = yc * lax.rsqrt(var + EPS) * ng_ref[...] + nb_ref[...]
        o_ref[pl.ds(r0, SUBLANES), :] = _silu(z).astype(o_ref.dtype)
        return carry

    lax.fori_loop(0, tmb // SUBLANES, norm_group, 0, unroll=8)


def _mix_b(p, cw, cb, ng, nb, bw, t_lat):
    m = p.shape[0]
    tmb = _pick(math.gcd(t_lat, m - t_lat), CONV_TILE, CONV_ROWS)
    hb = tmb // CONV_HALO
    nhalo = m // CONV_HALO
    a_col, g_col = 2, 3
    prev = lambda c: (lambda i: (jnp.maximum(i * hb - 1, 0), c))
    cur = lambda c: (lambda i: (i, c))
    nxt = lambda c: (lambda i: (jnp.minimum((i + 1) * hb, nhalo - 1), c))
    vec = pl.BlockSpec((SUBLANES, bw), lambda i: (0, 0))
    rep = lambda a: jnp.broadcast_to(a.reshape(-1, 1, bw), (a.size // bw, SUBLANES, bw)).reshape(-1, bw)
    return pl.pallas_call(
        functools.partial(_mix_b_kernel, tiles_lat=t_lat // tmb, tiles_tot=m // tmb, tmb=tmb),
        grid=(m // tmb,),
        in_specs=[
            pl.BlockSpec((CONV_HALO, bw), prev(a_col)),
            pl.BlockSpec((tmb, bw), cur(a_col)),
            pl.BlockSpec((CONV_HALO, bw), nxt(a_col)),
            pl.BlockSpec((CONV_HALO, bw), prev(g_col)),
            pl.BlockSpec((tmb, bw), cur(g_col)),
            pl.BlockSpec((CONV_HALO, bw), nxt(g_col)),
            pl.BlockSpec((CONV_K * SUBLANES, bw), lambda i: (0, 0)),
            vec, vec, vec,
        ],
        out_specs=pl.BlockSpec((tmb, bw), lambda i: (i, 0)),
        out_shape=jax.ShapeDtypeStruct((m, bw), BF16),
        scratch_shapes=[pltpu.VMEM((SUBLANES, tmb + 2 * CONV_HALO, bw), F32), pltpu.VMEM((tmb, bw), F32)],
        compiler_params=_params(("arbitrary",), 20 * tmb * bw * 4),
        name="mixer_conformer_conv",
    )(p, p, p, p, p, p, rep(cw), rep(cb), rep(ng), rep(nb))


def _rope_tables(t_lat):
    m = HEAD_DIM // 4
    freqs = np.power(np.float32(ROPE_THETA), -np.arange(m, dtype=np.float32) / np.float32(m)).astype(np.float32)
    r = t_lat // GRID_W
    ang_r = np.repeat(np.arange(r, dtype=np.float32)[:, None] * freqs[None, :], GRID_W, axis=0)
    ang_c = np.tile(np.arange(GRID_W, dtype=np.float32)[:, None] * freqs[None, :], (r, 1))
    cos = np.concatenate([np.cos(ang_r), np.cos(ang_r), np.cos(ang_c), np.cos(ang_c)], axis=-1)
    sin = np.concatenate([-np.sin(ang_r), np.sin(ang_r), -np.sin(ang_c), np.sin(ang_c)], axis=-1)
    return jnp.asarray(cos, F32), jnp.asarray(sin, F32)


def _rope(x, cos, sin):
    lane = lax.broadcasted_iota(jnp.int32, x.shape, 1)
    low = (lane % (HEAD_DIM // 2)) < (HEAD_DIM // 4)
    swapped = jnp.where(low, pltpu.roll(x, HEAD_DIM - HEAD_DIM // 4, 1), pltpu.roll(x, HEAD_DIM // 4, 1))
    return x * cos + swapped * sin


def _bias_kernel(rpb_ref, o_ref):
    n = GRID_W * GRID_W
    col = lax.broadcasted_iota(jnp.int32, (1, n), 1)
    q = col // GRID_W
    k = col % GRID_W
    dc = jnp.clip(k - q, -(WIN_W - 1), WIN_W - 1) + (WIN_W - 1)
    start = jnp.clip(q - WIN_W // 2, 0, GRID_W - WIN_W)
    in_win = jnp.logical_and(k >= start, k < start + WIN_W)
    val = jnp.zeros(o_ref.shape, F32)
    for d in range(2 * WIN_W - 1):
        val = jnp.where(dc == d, rpb_ref[:, d:d + 1], val)
    o_ref[...] = jnp.where(in_win, val * LOG2_E, NEG_BIG)


ATTN_QROWS = 4
ATTN_KROWS = ATTN_QROWS + WIN_H
ATTN_SHIFT = WIN_H // 2
ATTN_VARIANTS = WIN_H // ATTN_QROWS + 1


def _bias_table(rpb):
    depth, nh, ndr, ndc = rpb.shape
    rows = depth * nh * ndr
    toe = pl.pallas_call(
        _bias_kernel,
        out_shape=jax.ShapeDtypeStruct((rows, GRID_W * GRID_W), F32),
        compiler_params=_params((), 16 * rows * GRID_W * GRID_W),
        name="attention_bias_table",
    )(rpb.reshape(rows, ndc))
    toe = toe.reshape(depth, nh, ndr, GRID_W, GRID_W)
    masked = jnp.full((depth, nh, GRID_W, GRID_W), NEG_BIG, F32)

    def variant(block_off):
        out = []
        for qr in range(ATTN_QROWS):
            first = min(max(block_off + qr - ATTN_SHIFT, 0), ATTN_KROWS - WIN_H)
            cols = []
            for j in range(ATTN_KROWS):
                visible = first <= j < first + WIN_H
                cols.append(toe[:, :, j - qr - block_off + WIN_H - 1] if visible else masked)
            out.append(jnp.concatenate(cols, axis=-1))
        return jnp.concatenate(out, axis=-2)

    return jnp.stack([variant(off) for off in range(0, WIN_H + 1, ATTN_QROWS)], axis=2)


ATTN_UNROLL = 4
ATTN_PREP_ROWS = 256


def _softmax_pv(scores, values):
    mx = None
    for s in scores:
        smax = jnp.max(s, axis=-1, keepdims=True)
        mx = smax if mx is None else jnp.maximum(mx, smax)
    den = None
    acc = None
    for s, v in zip(scores, values):
        e = jnp.exp2(s - mx)
        esum = jnp.sum(e, axis=-1, keepdims=True)
        den = esum if den is None else den + esum
        pv = jnp.dot(e.astype(BF16), v, preferred_element_type=F32)
        acc = pv if acc is None else acc + pv
    return acc / den


def _dot_nt(a, b):
    return lax.dot_general(a, b, (((1,), (1,)), ((), ())), preferred_element_type=F32)


def _attn_kernel(q_ref, k_ref, v_ref, qc_ref, kc_ref, vc_ref, cos_ref, sin_ref, bias_ref, o_ref, kr_ref,
                 *, rows, t_lat, with_ctx):
    scale = HEAD_DIM ** -0.5 * LOG2_E
    kc = kc_ref[...]
    vc = vc_ref[...]
    nq = ATTN_QROWS * GRID_W
    nk = ATTN_KROWS * GRID_W

    def prep(c, carry):
        rs = pl.ds(pl.multiple_of(c * ATTN_PREP_ROWS, ATTN_PREP_ROWS), ATTN_PREP_ROWS)
        kr_ref[rs, :] = _rope(k_ref[rs, :].astype(F32), cos_ref[rs, :], sin_ref[rs, :]).astype(BF16)
        return carry

    lax.fori_loop(0, t_lat // ATTN_PREP_ROWS, prep, 0)

    def body(blk, carry):
        r0 = blk * ATTN_QROWS
        s0 = jnp.clip(r0 - ATTN_SHIFT, 0, rows - ATTN_KROWS)
        qs = pl.ds(pl.multiple_of(r0 * GRID_W, nq), nq)
        ks = pl.ds(pl.multiple_of(s0 * GRID_W, GRID_W), nk)
        q = q_ref[qs, :]
        q_rot = _rope(q.astype(F32), cos_ref[qs, :], sin_ref[qs, :]).astype(BF16)
        s_lat = _dot_nt(q_rot, kr_ref[ks, :]) * scale
        b = bias_ref[(r0 - s0) // ATTN_QROWS]
        s_lat = jnp.where(b > 0.5 * NEG_BIG, s_lat + b, NEG_BIG)
        s_ctx = _dot_nt(q, kc) * scale
        o = _softmax_pv((s_lat, s_ctx), (v_ref[ks, :], vc))
        o_ref[qs, :] = o.astype(o_ref.dtype)
        return carry

    lax.fori_loop(0, rows // ATTN_QROWS, body, 0, unroll=ATTN_UNROLL)
    if with_ctx:
        s = _dot_nt(qc_ref[...], kc) * scale
        o_ref[t_lat:, :] = _softmax_pv((s,), (vc,)).astype(o_ref.dtype)


def _attention(p, cos, sin, bias, layer, bw, t_lat, with_ctx):
    m = p.shape[0]
    n_ctx = m - t_lat
    nh = bw // HEAD_DIM
    rows = t_lat // GRID_W
    assert rows >= ATTN_KROWS and rows % ATTN_QROWS == 0 and t_lat % n_ctx == 0 and t_lat % ATTN_PREP_ROWS == 0
    m_out = m if with_ctx else t_lat
    q_col, k_col, v_col = 4 * nh, 5 * nh, 6 * nh
    lat = lambda c: pl.BlockSpec((t_lat, HEAD_DIM), lambda h: (0, c + h))
    ctx = lambda c: pl.BlockSpec((n_ctx, HEAD_DIM), lambda h: (t_lat // n_ctx, c + h))
    table = pl.BlockSpec((t_lat, HEAD_DIM), lambda h: (0, 0), pipeline_mode=pl.Buffered(1))
    nq, nk = ATTN_QROWS * GRID_W, ATTN_KROWS * GRID_W
    need = (2 * 3 * t_lat * HEAD_DIM * 2 + 2 * t_lat * HEAD_DIM * 4 + t_lat * HEAD_DIM * 2
            + 2 * m_out * HEAD_DIM * 2 + 2 * ATTN_VARIANTS * nq * nk * 4 + 8 * nq * (nk + n_ctx) * 4)
    return pl.pallas_call(
        functools.partial(_attn_kernel, rows=rows, t_lat=t_lat, with_ctx=with_ctx),
        grid=(nh,),
        in_specs=[lat(q_col), lat(k_col), lat(v_col), ctx(q_col), ctx(k_col), ctx(v_col), table, table,
                  pl.BlockSpec((None, None, ATTN_VARIANTS, nq, nk), lambda h: (layer, h, 0, 0, 0))],
        out_specs=pl.BlockSpec((m_out, HEAD_DIM), lambda h: (0, h)),
        out_shape=jax.ShapeDtypeStruct((m_out, bw), BF16),
        scratch_shapes=[pltpu.VMEM((t_lat, HEAD_DIM), BF16)],
        compiler_params=_params(("arbitrary",), need),
        name="neighbourhood_attention",
    )(p, p, p, p, p, p, cos, sin, bias)


def _scan_kernel(*refs, nh, reverse, readout):
    if readout:
        qd_ref, z_ref, i_ref, lb_ref, g_ref, other_ref, ng_ref, o_ref, st_ref = refs
    else:
        qd_ref, z_ref, i_ref, lb_ref, o_ref, st_ref = refs
    c = CHUNK

    @pl.when(pl.program_id(0) == 0)
    def _():
        st_ref[...] = jnp.zeros_like(st_ref)

    ri = lax.broadcasted_iota(jnp.int32, (c, 1), 0)
    ti = lax.broadcasted_iota(jnp.int32, (c, c), 0)
    si = lax.broadcasted_iota(jnp.int32, (c, c), 1)
    shift = SCAN_DIRECT.bit_length() - 1
    near = jnp.logical_and((ti >> shift) == (si >> shift), (si >= ti) if reverse else (si <= ti))
    widths = [1 << k for k in range(c.bit_length() - 1)]
    masks = [jnp.logical_and(((ti ^ si) >> (w.bit_length() - 1)) == 1,
                             ((ti & w) == 0) if reverse else ((ti & w) != 0)) for w in widths]

    for h in range(nh):
        sl = slice(h * HEAD_DIM, (h + 1) * HEAD_DIM)
        lb = lb_ref[:, sl]
        f = lb + (1.0 - lb) * _sigmoid(z_ref[:, sl])
        l = jnp.log(jnp.maximum(f, F_MIN)) * LOG2_E
        kg = 1.0 - f
        q = _silu(qd_ref[:, sl])
        v = i_ref[:, sl].astype(BF16)

        cf, tot = l, l
        att = None
        for w, mk in zip(widths, masks):
            if w == SCAN_DIRECT:
                e = (cf - l) if reverse else cf
                q_near = q * jnp.exp2(-e if reverse else e)
                k_near = kg * jnp.exp2(e if reverse else -e)
                att = jnp.where(near, _dot_nt(q_near.astype(BF16), k_near.astype(BF16)), 0.0)
            if w < SUBLANES:
                odd = (ri & w) != 0
                if w < SCAN_DIRECT:
                    x = None
                elif reverse:
                    x = jnp.where(odd, kg, q) * jnp.exp2(jnp.where(odd, cf - l, tot - cf + l))
                else:
                    x = jnp.where(odd, q, kg) * jnp.exp2(jnp.where(odd, cf, tot - cf))
                prev_tot = pltpu.roll(tot, w, 0)
                next_tot = pltpu.roll(tot, c - w, 0)
                cf = cf + jnp.where(odd, prev_tot, 0.0)
                tot = tot + jnp.where(odd, prev_tot, next_tot)
            else:
                xs, cfs, tots = [], [], []
                for k in range(c // w):
                    rows = slice(k * w, (k + 1) * w)
                    other = slice((k ^ 1) * w, ((k ^ 1) + 1) * w)
                    is_query = (k % 2 == 0) == reverse
                    if reverse:
                        arg = (tot[rows] - cf[rows] + l[rows]) if is_query else (cf[rows] - l[rows])
                    else:
                        arg = cf[rows] if is_query else (tot[rows] - cf[rows])
                    xs.append((q[rows] if is_query else kg[rows]) * jnp.exp2(arg))
                    cfs.append(cf[rows] + tot[other] if k % 2 else cf[rows])
                    tots.append(tot[rows] + tot[other])
                x = jnp.concatenate(xs, axis=0)
                cf = jnp.concatenate(cfs, axis=0)
                tot = jnp.concatenate(tots, axis=0)
            if x is not None:
                xb = x.astype(BF16)
                att = jnp.where(mk, _dot_nt(xb, xb), att)
        to_state = (tot - cf + l) if reverse else cf
        to_end = (cf - l) if reverse else (tot - cf)
        q_in = (q * jnp.exp2(to_state)).astype(BF16)
        k_out = (kg * jnp.exp2(to_end)).astype(BF16)

        st = st_ref[h]
        o = _dot_nt(q_in, st.astype(BF16)) + jnp.dot(att.astype(BF16), v, preferred_element_type=F32)
        st_ref[h] = st * jnp.exp2(tot[0:1, :]) + lax.dot_general(
            v, k_out, (((0,), (0,)), ((), ())), preferred_element_type=F32)
        if readout:
            o = o + other_ref[:, sl]
            o = o * lax.rsqrt(jnp.mean(o * o, axis=-1, keepdims=True) + EPS) * ng_ref[...]
            o_ref[:, sl] = (o * _silu(g_ref[:, sl])).astype(o_ref.dtype)
        else:
            o_ref[:, sl] = o


def _scan(p, lb, bw, t_lat, reverse, other=None, norm_g=None):
    m = p.shape[0]
    nh = bw // HEAD_DIM
    n_lat = t_lat // CHUNK
    n_all = m // CHUNK
    n_ctx = n_all - n_lat
    readout = other is not None
    if reverse:
        row_blk = lambda j: jnp.where(j < n_ctx, n_all - 1 - j, n_lat - 1 - (j - n_ctx))
    else:
        row_blk = lambda j: jnp.where(j < n_ctx, n_lat + j, j - n_ctx)
    blk = lambda c: pl.BlockSpec((CHUNK, bw), lambda j: (row_blk(j), c))
    qd_col, zf_col, zb_col, i_col, g_col = 0, 1, 2, 3, 4
    in_specs = [blk(qd_col), blk(zb_col if reverse else zf_col), blk(i_col), pl.BlockSpec((1, bw), lambda j: (0, 0))]
    args = [p, p, p, lb.reshape(1, bw)]
    if readout:
        in_specs += [blk(g_col), blk(0), pl.BlockSpec((1, HEAD_DIM), lambda j: (0, 0))]
        args += [p, other, norm_g.reshape(1, HEAD_DIM)]
    return pl.pallas_call(
        functools.partial(_scan_kernel, nh=nh, reverse=reverse, readout=readout),
        grid=(n_all,),
        in_specs=in_specs,
        out_specs=pl.BlockSpec((CHUNK, bw), lambda j: (row_blk(j), 0)),
        out_shape=jax.ShapeDtypeStruct((m, bw), BF16 if readout else F32),
        scratch_shapes=[pltpu.VMEM((nh, HEAD_DIM, HEAD_DIM), F32)],
        compiler_params=_params(("arbitrary",), 48 * CHUNK * bw * 4),
        name="hgrn2_scan_bwd" if reverse else "hgrn2_scan_fwd",
    )(*args)


def kernel(x, c, ctx, c_ctx, w_ada, b_ada, norm1_g, norm2_g, w_in, a_norm_g, a_norm_b, a_ws, a_bs, b_conv_w,
           b_conv_b, b_norm_g, b_norm_b, c_rpb, d_lb_logits, d_norm_g, w_branch, w_gate, b_gate, w_out, w_ffn1,
           w_ffn3, w_ffn2, final_g):
    batch, t_lat, d = x.shape
    n_ctx = ctx.shape[1]
    depth = w_ada.shape[0]
    bw = d // N_BRANCH
    assert batch == 1 and t_lat % GRID_W == 0 and t_lat % CHUNK == 0 and n_ctx % CHUNK == 0

    xa, xc = x[0], ctx[0]
    cond = jnp.zeros((SUBLANES, d), F32).at[0].set(c[0]).at[1].set(c_ctx)
    mod = _modulation(cond, w_ada, b_ada)

    prob = jax.nn.softmax(d_lb_logits.astype(F32), axis=1)
    lbound = jnp.cumsum(prob, axis=1) - prob[:, :1]

    cos, sin = _rope_tables(t_lat)
    bias = _bias_table(c_rpb)
    w_gate_b, w_branch_b, w2_b = (w.astype(BF16) for w in (w_gate, w_branch, w_ffn2))

    for l in range(depth):
        last = l == depth - 1
        m = t_lat if last else t_lat + n_ctx
        h = _norm_mod(xa, norm1_g[l], mod, l, 0, 1, t_lat, ctx=xc)
        p = _ws_matmul(h, w_in, l, 0, 7 * bw, BF16)
        pd = _ws_matmul(h, w_in, l, 7 * bw, 5 * bw, F32)
        ya = _mix_a(p, a_ws[l], a_bs[l], a_norm_g[l], a_norm_b[l], bw)
        yb = _mix_b(p, b_conv_w[l], b_conv_b[l], b_norm_g[l], b_norm_b[l], bw, t_lat)
        yc = _attention(p, cos, sin, bias, l, bw, t_lat, with_ctx=not last)
        o_bwd = _scan(pd, lbound[1, l], bw, t_lat, reverse=True)
        yd = _scan(pd, lbound[0, l], bw, t_lat, reverse=False, other=o_bwd, norm_g=d_norm_g[l])
        merged = _gated_merge(h, (ya, yb, yc, yd), w_gate_b, b_gate, w_branch_b, l, m)
        x1 = _ws_matmul_residual(merged, w_out, l, xa, mod, 2, t_lat, m, ctx=None if last else xc)
        hf = _norm_mod(x1, norm2_g[l], mod, l, 3, 4, t_lat)
        act = _ws_swiglu_up(hf, w_ffn1, w_ffn3, l, m)
        xa, xc = _matmul_residual(act, w2_b, l, x1, mod, 5, t_lat, m), None

    return _final_norm(xa[:t_lat], final_g)[None]
```

```python
import functools
import math

import jax
import jax.numpy as jnp
import numpy as np
from jax import lax
from jax.experimental import pallas as pl
from jax.experimental.pallas import tpu as pltpu

F32 = jnp.float32
BF16 = jnp.bfloat16

HEAD_DIM = 128
CHUNK = 128
GRID_W = 64
WIN_H = 8
WIN_W = 16
CONV_K = 31
N_BRANCH = 4
ROPE_THETA = 10000.0
EPS = 1e-6
F_MIN = 1e-6
NEG_BIG = -1e30
LOG2_E = 1.4426950408889634
SCAN_DIRECT = 4
assert (SCAN_DIRECT - 1) * -math.log2(F_MIN) < 120 <= (2 * SCAN_DIRECT - 1) * -math.log2(F_MIN)

V7X_VMEM_BYTES = 64 * 2**20
VMEM_CAP = V7X_VMEM_BYTES - 6 * 2**20
SUBLANES = 8
LANES = 128
CONV_HALO = 16
MXU_COLS = 256

WS_ROWS, WS_COLS = 1408, 512
WS_PAIR_COLS = MXU_COLS
AS_ROWS = 768
AS_ROWS_WIDE, AS_COLS_WIDE = 512, 512
ROW_TILE = 512
CONV_TILE = 256


def _pick(n, target, quantum):
    best = None
    for t in range(quantum, min(n, target) + 1, quantum):
        if n % t == 0:
            best = t
    assert best is not None, (n, target, quantum)
    return best


def _params(sem, need_bytes):
    limit = int(min(VMEM_CAP, max(need_bytes * 5 // 4, 16 * 2**20)))
    return pltpu.CompilerParams(dimension_semantics=sem, vmem_limit_bytes=limit)


def _sigmoid(x):
    return jax.nn.sigmoid(x)


def _silu(x):
    return x * jax.nn.sigmoid(x)


def _gelu(x):
    return 0.5 * x * (1.0 + lax.erf(x * 0.7071067811865476))


def _row_select(rows_is_ctx, p_ref):
    return jnp.where(rows_is_ctx, p_ref[1:2, :], p_ref[0:1, :])


def _mod_kernel(cond_ref, w_ref, b_ref, o_ref):
    s = _silu(cond_ref[...]).astype(BF16)
    o_ref[...] = jnp.dot(s, w_ref[...].astype(BF16), preferred_element_type=F32) + b_ref[...]


def _modulation(cond, w_ada, b_ada):
    depth, d, n = w_ada.shape
    tn = _pick(n, WS_COLS, LANES)
    need = 2 * d * tn * 4 + d * tn * 2
    return pl.pallas_call(
        _mod_kernel,
        grid=(depth, n // tn),
        in_specs=[
            pl.BlockSpec((SUBLANES, d), lambda l, j: (0, 0)),
            pl.BlockSpec((None, d, tn), lambda l, j: (l, 0, j)),
            pl.BlockSpec((None, 1, tn), lambda l, j: (l, 0, j)),
        ],
        out_specs=pl.BlockSpec((None, SUBLANES, tn), lambda l, j: (l, 0, j)),
        out_shape=jax.ShapeDtypeStruct((depth, SUBLANES, n), F32),
        compiler_params=_params(("arbitrary", "arbitrary"), need),
        name="adaln_modulation",
    )(cond, w_ada, b_ada.reshape(depth, 1, n))


NORM_ROWS = 8


def _row_rsqrt_stats(x_ref, rs_ref, tm):
    def stats(r, carry):
        r0 = pl.multiple_of(r * NORM_ROWS, NORM_ROWS)
        x = x_ref[pl.ds(r0, NORM_ROWS), :]
        rs_ref[pl.ds(r0, NORM_ROWS), :] = lax.rsqrt(jnp.mean(x * x, axis=-1, keepdims=True) + EPS)
        return carry

    lax.fori_loop(0, tm // NORM_ROWS, stats, 0, unroll=4)


def _norm_mod_tile(x_ref, which, g_ref, sh_ref, sc_ref, o_ref, rs_ref, gs_ref, tm):
    _row_rsqrt_stats(x_ref, rs_ref, tm)
    shape = (NORM_ROWS, x_ref.shape[1])
    gs_ref[0:NORM_ROWS, :] = jnp.broadcast_to(g_ref[...] * (1.0 + sc_ref[pl.ds(which, 1), :]), shape)
    gs_ref[NORM_ROWS:, :] = jnp.broadcast_to(sh_ref[pl.ds(which, 1), :], shape)

    def scale(r, carry):
        r0 = pl.multiple_of(r * NORM_ROWS, NORM_ROWS)
        y = x_ref[pl.ds(r0, NORM_ROWS), :] * rs_ref[pl.ds(r0, NORM_ROWS), :]
        o_ref[pl.ds(r0, NORM_ROWS), :] = (y * gs_ref[0:NORM_ROWS, :] + gs_ref[NORM_ROWS:, :]).astype(o_ref.dtype)
        return carry

    lax.fori_loop(0, tm // NORM_ROWS, scale, 0, unroll=8)


def _norm_mod_kernel(x_ref, g_ref, sh_ref, sc_ref, o_ref, rs_ref, gs_ref, *, t_lat, tm):
    which = (pl.program_id(0) * tm >= t_lat).astype(jnp.int32)
    _norm_mod_tile(x_ref, which, g_ref, sh_ref, sc_ref, o_ref, rs_ref, gs_ref, tm)


def _norm_mod2_kernel(x_ref, c_ref, g_ref, sh_ref, sc_ref, o_ref, rs_ref, gs_ref, *, lat_tiles, tm):
    @pl.when(pl.program_id(0) < lat_tiles)
    def _():
        _norm_mod_tile(x_ref, 0, g_ref, sh_ref, sc_ref, o_ref, rs_ref, gs_ref, tm)

    @pl.when(pl.program_id(0) >= lat_tiles)
    def _():
        _norm_mod_tile(c_ref, 1, g_ref, sh_ref, sc_ref, o_ref, rs_ref, gs_ref, tm)


def _norm_mod(x, g, mod, layer, shift_blk, scale_blk, t_lat, ctx=None):
    d = x.shape[1]
    m = x.shape[0] + (0 if ctx is None else ctx.shape[0])
    tm = _pick(math.gcd(t_lat, m), ROW_TILE, NORM_ROWS)
    lat_tiles = t_lat // tm
    param_specs = [
        pl.BlockSpec((1, d), lambda i: (0, 0)),
        pl.BlockSpec((None, SUBLANES, d), lambda i: (layer, 0, shift_blk)),
        pl.BlockSpec((None, SUBLANES, d), lambda i: (layer, 0, scale_blk)),
    ]
    if ctx is None:
        kern = functools.partial(_norm_mod_kernel, t_lat=t_lat, tm=tm)
        row_specs, rows = [pl.BlockSpec((tm, d), lambda i: (i, 0))], (x,)
    else:
        kern = functools.partial(_norm_mod2_kernel, lat_tiles=lat_tiles, tm=tm)
        row_specs = [pl.BlockSpec((tm, d), lambda i: (jnp.minimum(i, lat_tiles - 1), 0)),
                     pl.BlockSpec((tm, d), lambda i: (jnp.maximum(i - lat_tiles, 0), 0))]
        rows = (x, ctx)
    return pl.pallas_call(
        kern,
        grid=(m // tm,),
        in_specs=row_specs + param_specs,
        out_specs=pl.BlockSpec((tm, d), lambda i: (i, 0)),
        out_shape=jax.ShapeDtypeStruct((m, d), BF16),
        scratch_shapes=[pltpu.VMEM((tm, 1), F32), pltpu.VMEM((2 * NORM_ROWS, d), F32)],
        compiler_params=_params(("arbitrary",), 2 * tm * d * 10 + tm * LANES * 4 + 4 * SUBLANES * d * 4),
        name="rmsnorm_modulate",
    )(*rows, g.reshape(1, d), mod, mod)


def _final_norm_kernel(x_ref, g_ref, o_ref, rs_ref, *, tm):
    _row_rsqrt_stats(x_ref, rs_ref, tm)

    def scale(r, carry):
        r0 = pl.multiple_of(r * NORM_ROWS, NORM_ROWS)
        o_ref[pl.ds(r0, NORM_ROWS), :] = x_ref[pl.ds(r0, NORM_ROWS), :] * rs_ref[pl.ds(r0, NORM_ROWS), :] * g_ref[...]
        return carry

    lax.fori_loop(0, tm // NORM_ROWS, scale, 0, unroll=8)


def _final_norm(x, g):
    m, d = x.shape
    tm = _pick(m, ROW_TILE, NORM_ROWS)
    return pl.pallas_call(
        functools.partial(_final_norm_kernel, tm=tm),
        grid=(m // tm,),
        in_specs=[pl.BlockSpec((tm, d), lambda i: (i, 0)), pl.BlockSpec((NORM_ROWS, d), lambda i: (0, 0))],
        out_specs=pl.BlockSpec((tm, d), lambda i: (i, 0)),
        out_shape=jax.ShapeDtypeStruct((m, d), F32),
        scratch_shapes=[pltpu.VMEM((tm, 1), F32)],
        compiler_params=_params(("arbitrary",), 2 * tm * d * 8 + tm * LANES * 4),
        name="final_rmsnorm",
    )(x, jnp.broadcast_to(g.reshape(1, d), (NORM_ROWS, d)))


def _ws_mm_kernel(a_ref, w_ref, o_ref, wb):
    @pl.when(pl.program_id(1) == 0)
    def _():
        wb[...] = w_ref[...].astype(BF16)

    o_ref[...] = jnp.dot(a_ref[...], wb[...], preferred_element_type=F32).astype(o_ref.dtype)


def _ws_matmul(a, w, layer, col0, n, out_dtype):
    m, k = a.shape
    tm = _pick(m, WS_ROWS, LANES)
    tn = _pick(math.gcd(n, col0) if col0 else n, WS_COLS, LANES)
    jb = col0 // tn
    osz = jnp.dtype(out_dtype).itemsize
    need = 2 * (tm * k * 2 + k * tn * 4 + tm * tn * osz) + k * tn * 2 + tm * tn * 4
    return pl.pallas_call(
        _ws_mm_kernel,
        grid=(n // tn, m // tm),
        in_specs=[pl.BlockSpec((tm, k), lambda j, i: (i, 0)),
                  pl.BlockSpec((None, k, tn), lambda j, i: (layer, 0, jb + j))],
        out_specs=pl.BlockSpec((tm, tn), lambda j, i: (i, j)),
        out_shape=jax.ShapeDtypeStruct((m, n), out_dtype),
        scratch_shapes=[pltpu.VMEM((k, tn), BF16)],
        compiler_params=_params(("arbitrary", "arbitrary"), need),
        name="input_projection",
    )(a, w)


def _ws_resid_kernel(a_ref, w_ref, x_ref, gate_ref, o_ref, wb, *, t_lat, tm):
    @pl.when(pl.program_id(1) == 0)
    def _():
        wb[...] = w_ref[...].astype(BF16)

    rows = pl.program_id(1) * tm + lax.broadcasted_iota(jnp.int32, (tm, 1), 0)
    gate = _row_select(rows >= t_lat, gate_ref)
    o_ref[...] = x_ref[...] + gate * jnp.dot(a_ref[...], wb[...], preferred_element_type=F32)


def _ws_resid2_kernel(a_ref, w_ref, x_ref, c_ref, gate_ref, o_ref, wb, *, lat_rows_last):
    @pl.when(pl.program_id(1) == 0)
    def _():
        wb[...] = w_ref[...].astype(BF16)

    y = jnp.dot(a_ref[...], wb[...], preferred_element_type=F32)
    last = pl.num_programs(1) - 1

    @pl.when(pl.program_id(1) < last)
    def _():
        o_ref[...] = x_ref[...] + gate_ref[0:1, :] * y

    @pl.when(pl.program_id(1) == last)
    def _():
        k = lat_rows_last
        o_ref[0:k, :] = x_ref[0:k, :] + gate_ref[0:1, :] * y[0:k]
        o_ref[k:, :] = c_ref[...] + gate_ref[1:2, :] * y[k:]


def _ws_matmul_residual(a, w, layer, x, mod, gate_blk, t_lat, m, ctx=None):
    k = a.shape[1]
    n = w.shape[2]
    tm = _pick(m, min(WS_ROWS, t_lat), LANES)
    tn = _pick(n, WS_COLS, LANES)
    nb = n // tn
    need = 2 * (tm * k * 2 + k * tn * 4 + 2 * tm * tn * 4) + k * tn * 2 + tm * tn * 4
    specs = [
        pl.BlockSpec((tm, k), lambda j, i: (i, 0)),
        pl.BlockSpec((None, k, tn), lambda j, i: (layer, 0, j)),
        pl.BlockSpec((tm, tn), lambda j, i: (i, j)),
    ]
    if ctx is None:
        kern = functools.partial(_ws_resid_kernel, t_lat=t_lat, tm=tm)
        rows = (x,)
    else:
        n_ctx = ctx.shape[0]
        assert m == t_lat + n_ctx and n_ctx <= tm and (tm - n_ctx) % SUBLANES == 0
        kern = functools.partial(_ws_resid2_kernel, lat_rows_last=tm - n_ctx)
        specs.append(pl.BlockSpec((n_ctx, tn), lambda j, i: (0, j)))
        rows = (x, ctx)
        need += 2 * n_ctx * tn * 4
    specs.append(pl.BlockSpec((None, SUBLANES, tn), lambda j, i: (layer, 0, gate_blk * nb + j)))
    return pl.pallas_call(
        kern,
        grid=(nb, m // tm),
        in_specs=specs,
        out_specs=pl.BlockSpec((tm, tn), lambda j, i: (i, j)),
        out_shape=jax.ShapeDtypeStruct((m, n), F32),
        scratch_shapes=[pltpu.VMEM((k, tn), BF16)],
        compiler_params=_params(("arbitrary", "arbitrary"), need),
        name="output_projection",
    )(a, w, *rows, mod)


def _ws_swiglu_kernel(a_ref, w1_ref, w3_ref, o_ref, w1b, w3b):
    @pl.when(pl.program_id(1) == 0)
    def _():
        w1b[...] = w1_ref[...].astype(BF16)
        w3b[...] = w3_ref[...].astype(BF16)

    a = a_ref[...]
    u = jnp.dot(a, w1b[...], preferred_element_type=F32)
    g = jnp.dot(a, w3b[...], preferred_element_type=F32)
    o_ref[...] = (_silu(u) * g).astype(o_ref.dtype)


def _ws_swiglu_up(a, w1, w3, layer, m):
    k = a.shape[1]
    n = w1.shape[2]
    tm = _pick(m, WS_ROWS, LANES)
    tn = _pick(n, WS_PAIR_COLS, LANES)
    need = 2 * (tm * k * 2 + 2 * k * tn * 4 + tm * tn * 2) + 2 * k * tn * 2 + 3 * tm * tn * 4
    w_spec = pl.BlockSpec((None, k, tn), lambda j, i: (layer, 0, j))
    return pl.pallas_call(
        _ws_swiglu_kernel,
        grid=(n // tn, m // tm),
        in_specs=[pl.BlockSpec((tm, k), lambda j, i: (i, 0)), w_spec, w_spec],
        out_specs=pl.BlockSpec((tm, tn), lambda j, i: (i, j)),
        out_shape=jax.ShapeDtypeStruct((m, n), BF16),
        scratch_shapes=[pltpu.VMEM((k, tn), BF16), pltpu.VMEM((k, tn), BF16)],
        compiler_params=_params(("arbitrary", "arbitrary"), need),
        name="swiglu_up",
    )(a, w1, w3)


def _mm_resid_kernel(a_ref, w_ref, x_ref, gate_ref, o_ref, *, t_lat, tm):
    rows = pl.program_id(0) * tm + lax.broadcasted_iota(jnp.int32, (tm, 1), 0)
    gate = _row_select(rows >= t_lat, gate_ref)
    o_ref[...] = x_ref[...] + gate * jnp.dot(a_ref[...], w_ref[...], preferred_element_type=F32)


def _matmul_residual(a, w, layer, x, mod, gate_blk, t_lat, m):
    k = a.shape[1]
    n = w.shape[2]
    tm = _pick(m, AS_ROWS, MXU_COLS)
    tn = _pick(n, MXU_COLS if tm > AS_ROWS_WIDE else AS_COLS_WIDE, LANES)
    need = 2 * (tm * k * 2 + k * tn * 2 + 2 * tm * tn * 4) + tm * tn * 4
    nb = n // tn
    return pl.pallas_call(
        functools.partial(_mm_resid_kernel, t_lat=t_lat, tm=tm),
        grid=(m // tm, nb),
        in_specs=[
            pl.BlockSpec((tm, k), lambda i, j: (i, 0)),
            pl.BlockSpec((None, k, tn), lambda i, j: (layer, 0, j)),
            pl.BlockSpec((tm, tn), lambda i, j: (i, j)),
            pl.BlockSpec((None, SUBLANES, tn), lambda i, j: (layer, 0, gate_blk * nb + j)),
        ],
        out_specs=pl.BlockSpec((tm, tn), lambda i, j: (i, j)),
        out_shape=jax.ShapeDtypeStruct((m, n), F32),
        compiler_params=_params(("arbitrary", "arbitrary"), need),
        name="ffn_down",
    )(a, w, x, mod)


def _merge_kernel(h_ref, ya_ref, yb_ref, yc_ref, yd_ref, wg0, wg1, wg2, wg3, bg0, bg1, bg2, bg3, wb_ref, o_ref):
    h = h_ref[...]
    acc = None
    for i, (y_ref, wg, bg) in enumerate(((ya_ref, wg0, bg0), (yb_ref, wg1, bg1), (yc_ref, wg2, bg2), (yd_ref, wg3, bg3))):
        gate = _sigmoid(jnp.dot(h, wg[...], preferred_element_type=F32) + bg[...])
        term = gate * jnp.dot(y_ref[...], wb_ref[i], preferred_element_type=F32)
        acc = term if acc is None else acc + term
    o_ref[...] = acc.astype(o_ref.dtype)


def _gated_merge(h, ys, w_gate, b_gate, w_branch, layer, m):
    k = h.shape[1]
    depth, nbr, bw, n = w_branch.shape
    tm = _pick(m, AS_ROWS, MXU_COLS)
    tn = _pick(n, MXU_COLS, LANES)
    nb = n // tn
    need = 2 * (tm * k * 2 + nbr * tm * bw * 2 + nbr * k * tn * 2 + nbr * bw * tn * 2 + tm * tn * 2) + 4 * tm * tn * 4
    y_spec = pl.BlockSpec((tm, bw), lambda i, j: (i, 0))
    wg_specs = [pl.BlockSpec((None, k, tn), functools.partial(lambda i, j, b: (layer, 0, b * nb + j), b=b))
                for b in range(nbr)]
    bg_specs = [pl.BlockSpec((None, 1, tn), functools.partial(lambda i, j, b: (layer, 0, b * nb + j), b=b))
                for b in range(nbr)]
    bg = b_gate.reshape(depth, 1, -1)
    return pl.pallas_call(
        _merge_kernel,
        grid=(m // tm, nb),
        in_specs=[pl.BlockSpec((tm, k), lambda i, j: (i, 0)), y_spec, y_spec, y_spec, y_spec]
        + wg_specs + bg_specs + [pl.BlockSpec((None, nbr, bw, tn), lambda i, j: (layer, 0, 0, j))],
        out_specs=pl.BlockSpec((tm, tn), lambda i, j: (i, j)),
        out_shape=jax.ShapeDtypeStruct((m, n), BF16),
        compiler_params=_params(("arbitrary", "arbitrary"), need),
        name="gated_merge",
    )(h, *ys, w_gate, w_gate, w_gate, w_gate, bg, bg, bg, bg, w_branch)


def _mix_a_kernel(pu_ref, pv_ref, ws_ref, bias_ref, ng_ref, nb_ref, o_ref, *, nh):
    zu = _gelu(pu_ref[...].astype(F32))
    zv = _gelu(pv_ref[...].astype(F32))
    mu = jnp.mean(zv, axis=-1, keepdims=True)
    zc = zv - mu
    var = jnp.mean(zc * zc, axis=-1, keepdims=True)
    v = (zc * lax.rsqrt(var + EPS) * ng_ref[...] + nb_ref[...]).astype(BF16)
    for g in range(nh):
        sl = slice(g * HEAD_DIM, (g + 1) * HEAD_DIM)
        s = jnp.dot(ws_ref[g], v[:, sl], preferred_element_type=F32) + bias_ref[:, sl]
        o_ref[:, sl] = (zu[:, sl] * s).astype(o_ref.dtype)


def _mix_a(p, ws, bs, ng, nb, bw):
    m = p.shape[0]
    nh = bw // HEAD_DIM
    bias = jnp.repeat(bs.T, HEAD_DIM, axis=1)
    return pl.pallas_call(
        functools.partial(_mix_a_kernel, nh=nh),
        grid=(m // CHUNK,),
        in_specs=[
            pl.BlockSpec((CHUNK, bw), lambda i: (i, 0)),
            pl.BlockSpec((CHUNK, bw), lambda i: (i, 1)),
            pl.BlockSpec((nh, CHUNK, CHUNK), lambda i: (0, 0, 0)),
            pl.BlockSpec((CHUNK, bw), lambda i: (0, 0)),
            pl.BlockSpec((1, bw), lambda i: (0, 0)),
            pl.BlockSpec((1, bw), lambda i: (0, 0)),
        ],
        out_specs=pl.BlockSpec((CHUNK, bw), lambda i: (i, 0)),
        out_shape=jax.ShapeDtypeStruct((m, bw), BF16),
        compiler_params=_params(("arbitrary",), 16 * CHUNK * bw * 4),
        name="mixer_gmlp",
    )(p, p, ws.astype(BF16), bias, ng.reshape(1, bw), nb.reshape(1, bw))


CONV_ROWS = 32


def _mix_b_kernel(ap_ref, ac_ref, an_ref, gp_ref, gc_ref, gn_ref, cw_ref, cb_ref, ng_ref, nb_ref, o_ref, ysh, yconv,
                  *, tiles_lat, tiles_tot, tmb):
    i = pl.program_id(0)
    first = jnp.logical_or(i == 0, i == tiles_lat)
    last = jnp.logical_or(i == tiles_lat - 1, i == tiles_tot - 1)
    glu = lambda a_ref, g_ref: a_ref[...].astype(F32) * _sigmoid(g_ref[...].astype(F32))
    yp = glu(ap_ref, gp_ref)
    yn = glu(an_ref, gn_ref)
    ysh[0, 0:CONV_HALO, :] = jnp.where(first, 0.0, yp)
    ysh[0, CONV_HALO:CONV_HALO + tmb, :] = glu(ac_ref, gc_ref)
    ysh[0, CONV_HALO + tmb:, :] = jnp.where(last, 0.0, yn)
    nrow = tmb + 2 * CONV_HALO
    for c0 in range(0, ysh.shape[2], LANES):
        base = ysh[0, :, c0:c0 + LANES]
        for b in range(1, SUBLANES):
            ysh[b, :, c0:c0 + LANES] = pltpu.roll(base, nrow - b, 0)
    off = CONV_HALO - CONV_K // 2

    bw = ysh.shape[2]
    groups = CONV_ROWS // SUBLANES

    def conv_block(rb, carry):
        r0 = pl.multiple_of(rb * CONV_ROWS, CONV_ROWS)
        acc = [None] * groups
        for j in range(CONV_K):
            b = (off + j) % SUBLANES
            wj = cw_ref[j * SUBLANES:(j + 1) * SUBLANES, :]
            for g in range(groups):
                term = ysh[b, pl.ds(r0 + (off + j - b + g * SUBLANES), SUBLANES), :] * wj
                acc[g] = term if acc[g] is None else acc[g] + term
        for g in range(groups):
            yconv[pl.ds(r0 + g * SUBLANES, SUBLANES), :] = acc[g]
        return carry

    lax.fori_loop(0, tmb // CONV_ROWS, conv_block, 0)

    def norm_group(rg, carry):
        r0 = pl.multiple_of(rg * SUBLANES, SUBLANES)
        y = yconv[pl.ds(r0, SUBLANES), :] + cb_ref[...]
        mu = jnp.mean(y, axis=-1, keepdims=True)
        yc = y - mu
        var = jnp.mean(yc * yc, axis=-1, keepdims=True)
        z = yc * lax.rsqrt(var + EPS) * ng_ref[...] + nb_ref[...]
        o_ref[pl.ds(r0, SUBLANES), :] = _silu(z).astype(o_ref.dtype)
        return carry

    lax.fori_loop(0, tmb // SUBLANES, norm_group, 0, unroll=8)


def _mix_b(p, cw, cb, ng, nb, bw, t_lat):
    m = p.shape[0]
    tmb = _pick(math.gcd(t_lat, m - t_lat), CONV_TILE, CONV_ROWS)
    hb = tmb // CONV_HALO
    nhalo = m // CONV_HALO
    a_col, g_col = 2, 3
    prev = lambda c: (lambda i: (jnp.maximum(i * hb - 1, 0), c))
    cur = lambda c: (lambda i: (i, c))
    nxt = lambda c: (lambda i: (jnp.minimum((i + 1) * hb, nhalo - 1), c))
    vec = pl.BlockSpec((SUBLANES, bw), lambda i: (0, 0))
    rep = lambda a: jnp.broadcast_to(a.reshape(-1, 1, bw), (a.size // bw, SUBLANES, bw)).reshape(-1, bw)
    return pl.pallas_call(
        functools.partial(_mix_b_kernel, tiles_lat=t_lat // tmb, tiles_tot=m // tmb, tmb=tmb),
        grid=(m // tmb,),
        in_specs=[
            pl.BlockSpec((CONV_HALO, bw), prev(a_col)),
            pl.BlockSpec((tmb, bw), cur(a_col)),
            pl.BlockSpec((CONV_HALO, bw), nxt(a_col)),
            pl.BlockSpec((CONV_HALO, bw), prev(g_col)),
            pl.BlockSpec((tmb, bw), cur(g_col)),
            pl.BlockSpec((CONV_HALO, bw), nxt(g_col)),
            pl.BlockSpec((CONV_K * SUBLANES, bw), lambda i: (0, 0)),
            vec, vec, vec,
        ],
        out_specs=pl.BlockSpec((tmb, bw), lambda i: (i, 0)),
        out_shape=jax.ShapeDtypeStruct((m, bw), BF16),
        scratch_shapes=[pltpu.VMEM((SUBLANES, tmb + 2 * CONV_HALO, bw), F32), pltpu.VMEM((tmb, bw), F32)],
        compiler_params=_params(("arbitrary",), 20 * tmb * bw * 4),
        name="mixer_conformer_conv",
    )(p, p, p, p, p, p, rep(cw), rep(cb), rep(ng), rep(nb))


def _rope_tables(t_lat):
    m = HEAD_DIM // 4
    freqs = np.power(np.float32(ROPE_THETA), -np.arange(m, dtype=np.float32) / np.float32(m)).astype(np.float32)
    r = t_lat // GRID_W
    ang_r = np.repeat(np.arange(r, dtype=np.float32)[:, None] * freqs[None, :], GRID_W, axis=0)
    ang_c = np.tile(np.arange(GRID_W, dtype=np.float32)[:, None] * freqs[None, :], (r, 1))
    cos = np.concatenate([np.cos(ang_r), np.cos(ang_r), np.cos(ang_c), np.cos(ang_c)], axis=-1)
    sin = np.concatenate([-np.sin(ang_r), np.sin(ang_r), -np.sin(ang_c), np.sin(ang_c)], axis=-1)
    return jnp.asarray(cos, F32), jnp.asarray(sin, F32)


def _rope(x, cos, sin):
    lane = lax.broadcasted_iota(jnp.int32, x.shape, 1)
    low = (lane % (HEAD_DIM // 2)) < (HEAD_DIM // 4)
    swapped = jnp.where(low, pltpu.roll(x, HEAD_DIM - HEAD_DIM // 4, 1), pltpu.roll(x, HEAD_DIM // 4, 1))
    return x * cos + swapped * sin


ATTN_QROWS = 4
ATTN_KROWS = ATTN_QROWS + WIN_H
ATTN_SHIFT = WIN_H // 2
ATTN_VARIANTS = WIN_H // ATTN_QROWS + 1


def _bias_kernel(rpb_ref, o_ref, *, ndr, ndc):
    base = pl.program_id(0) * (ndr * ndc)
    q = lax.broadcasted_iota(jnp.int32, (GRID_W, GRID_W), 0)
    k = lax.broadcasted_iota(jnp.int32, (GRID_W, GRID_W), 1)
    dc = jnp.clip(k - q, -(WIN_W - 1), WIN_W - 1) + (WIN_W - 1)
    start = jnp.clip(q - WIN_W // 2, 0, GRID_W - WIN_W)
    in_win = jnp.logical_and(k >= start, k < start + WIN_W)
    masked = jnp.full((GRID_W, GRID_W), NEG_BIG, F32)
    by_row_offset = []
    for dr in range(ndr):
        val = jnp.zeros((GRID_W, GRID_W), F32)
        for d in range(ndc):
            val = jnp.where(dc == d, rpb_ref[base + dr * ndc + d], val)
        by_row_offset.append(jnp.where(in_win, val * LOG2_E, NEG_BIG))
    for v in range(ATTN_VARIANTS):
        block_off = v * ATTN_QROWS
        for qr in range(ATTN_QROWS):
            first = min(max(block_off + qr - ATTN_SHIFT, 0), ATTN_KROWS - WIN_H)
            cols = [by_row_offset[j - qr - block_off + WIN_H - 1] if first <= j < first + WIN_H else masked
                    for j in range(ATTN_KROWS)]
            for j in range(0, ATTN_KROWS, 2):
                o_ref[v, qr * GRID_W:(qr + 1) * GRID_W, j * GRID_W:(j + 2) * GRID_W] = jnp.concatenate(
                    cols[j:j + 2], axis=-1)


def _bias_table(rpb):
    depth, nh, ndr, ndc = rpb.shape
    nq, nk = ATTN_QROWS * GRID_W, ATTN_KROWS * GRID_W
    assert 2 * GRID_W == LANES and ATTN_KROWS % 2 == 0
    tab = pl.pallas_call(
        functools.partial(_bias_kernel, ndr=ndr, ndc=ndc),
        grid=(depth * nh,),
        in_specs=[pl.BlockSpec(memory_space=pltpu.SMEM)],
        out_specs=pl.BlockSpec((None, ATTN_VARIANTS, nq, nk), lambda i: (i, 0, 0, 0)),
        out_shape=jax.ShapeDtypeStruct((depth * nh, ATTN_VARIANTS, nq, nk), F32),
        compiler_params=_params(("arbitrary",), 4 * ATTN_VARIANTS * nq * nk * 4),
        name="attention_bias_table",
    )(rpb.reshape(-1))
    return tab.reshape(depth, nh, ATTN_VARIANTS, nq, nk)


ATTN_UNROLL = 4
ATTN_PREP_ROWS = 256


def _softmax_pv(scores, values):
    mx = None
    for s in scores:
        smax = jnp.max(s, axis=-1, keepdims=True)
        mx = smax if mx is None else jnp.maximum(mx, smax)
    den = None
    acc = None
    for s, v in zip(scores, values):
        e = jnp.exp2(s - mx)
        esum = jnp.sum(e, axis=-1, keepdims=True)
        den = esum if den is None else den + esum
        pv = jnp.dot(e.astype(BF16), v, preferred_element_type=F32)
        acc = pv if acc is None else acc + pv
    return acc / den


def _dot_nt(a, b):
    return lax.dot_general(a, b, (((1,), (1,)), ((), ())), preferred_element_type=F32)


def _attn_kernel(q_ref, k_ref, v_ref, qc_ref, kc_ref, vc_ref, cos_ref, sin_ref, bias_ref, o_ref, kr_ref,
                 *, rows, t_lat, with_ctx):
    scale = HEAD_DIM ** -0.5 * LOG2_E
    kc = kc_ref[...]
    vc = vc_ref[...]
    nq = ATTN_QROWS * GRID_W
    nk = ATTN_KROWS * GRID_W

    def prep(c, carry):
        rs = pl.ds(pl.multiple_of(c * ATTN_PREP_ROWS, ATTN_PREP_ROWS), ATTN_PREP_ROWS)
        kr_ref[rs, :] = _rope(k_ref[rs, :].astype(F32), cos_ref[rs, :], sin_ref[rs, :]).astype(BF16)
        return carry

    lax.fori_loop(0, t_lat // ATTN_PREP_ROWS, prep, 0)

    def body(blk, carry):
        r0 = blk * ATTN_QROWS
        s0 = jnp.clip(r0 - ATTN_SHIFT, 0, rows - ATTN_KROWS)
        qs = pl.ds(pl.multiple_of(r0 * GRID_W, nq), nq)
        ks = pl.ds(pl.multiple_of(s0 * GRID_W, GRID_W), nk)
        q = q_ref[qs, :]
        q_rot = _rope(q.astype(F32), cos_ref[qs, :], sin_ref[qs, :]).astype(BF16)
        s_lat = _dot_nt(q_rot, kr_ref[ks, :]) * scale
        b = bias_ref[(r0 - s0) // ATTN_QROWS]
        s_lat = jnp.where(b > 0.5 * NEG_BIG, s_lat + b, NEG_BIG)
        s_ctx = _dot_nt(q, kc) * scale
        o = _softmax_pv((s_lat, s_ctx), (v_ref[ks, :], vc))
        o_ref[qs, :] = o.astype(o_ref.dtype)
        return carry

    lax.fori_loop(0, rows // ATTN_QROWS, body, 0, unroll=ATTN_UNROLL)
    if with_ctx:
        s = _dot_nt(qc_ref[...], kc) * scale
        o_ref[t_lat:, :] = _softmax_pv((s,), (vc,)).astype(o_ref.dtype)


def _attention(p, cos, sin, bias, layer, bw, t_lat, with_ctx):
    m = p.shape[0]
    n_ctx = m - t_lat
    nh = bw // HEAD_DIM
    rows = t_lat // GRID_W
    assert rows >= ATTN_KROWS and rows % ATTN_QROWS == 0 and t_lat % n_ctx == 0 and t_lat % ATTN_PREP_ROWS == 0
    m_out = m if with_ctx else t_lat
    q_col, k_col, v_col = 4 * nh, 5 * nh, 6 * nh
    lat = lambda c: pl.BlockSpec((t_lat, HEAD_DIM), lambda h: (0, c + h))
    ctx = lambda c: pl.BlockSpec((n_ctx, HEAD_DIM), lambda h: (t_lat // n_ctx, c + h))
    table = pl.BlockSpec((t_lat, HEAD_DIM), lambda h: (0, 0), pipeline_mode=pl.Buffered(1))
    nq, nk = ATTN_QROWS * GRID_W, ATTN_KROWS * GRID_W
    need = (2 * 3 * t_lat * HEAD_DIM * 2 + 2 * t_lat * HEAD_DIM * 4 + t_lat * HEAD_DIM * 2
            + 2 * m_out * HEAD_DIM * 2 + 2 * ATTN_VARIANTS * nq * nk * 4 + 8 * nq * (nk + n_ctx) * 4)
    return pl.pallas_call(
        functools.partial(_attn_kernel, rows=rows, t_lat=t_lat, with_ctx=with_ctx),
        grid=(nh,),
        in_specs=[lat(q_col), lat(k_col), lat(v_col), ctx(q_col), ctx(k_col), ctx(v_col), table, table,
                  pl.BlockSpec((None, None, ATTN_VARIANTS, nq, nk), lambda h: (layer, h, 0, 0, 0))],
        out_specs=pl.BlockSpec((m_out, HEAD_DIM), lambda h: (0, h)),
        out_shape=jax.ShapeDtypeStruct((m_out, bw), BF16),
        scratch_shapes=[pltpu.VMEM((t_lat, HEAD_DIM), BF16)],
        compiler_params=_params(("arbitrary",), need),
        name="neighbourhood_attention",
    )(p, p, p, p, p, p, cos, sin, bias)


def _scan_kernel(*refs, nh, reverse, readout):
    if readout:
        qd_ref, z_ref, i_ref, lb_ref, g_ref, other_ref, ng_ref, o_ref, st_ref = refs
    else:
        qd_ref, z_ref, i_ref, lb_ref, o_ref, st_ref = refs
    c = CHUNK

    @pl.when(pl.program_id(0) == 0)
    def _():
        st_ref[...] = jnp.zeros_like(st_ref)

    ri = lax.broadcasted_iota(jnp.int32, (c, 1), 0)
    ti = lax.broadcasted_iota(jnp.int32, (c, c), 0)
    si = lax.broadcasted_iota(jnp.int32, (c, c), 1)
    shift = SCAN_DIRECT.bit_length() - 1
    near = jnp.logical_and((ti >> shift) == (si >> shift), (si >= ti) if reverse else (si <= ti))
    widths = [1 << k for k in range(c.bit_length() - 1)]
    masks = [jnp.logical_and(((ti ^ si) >> (w.bit_length() - 1)) == 1,
                             ((ti & w) == 0) if reverse else ((ti & w) != 0)) for w in widths]

    for h in range(nh):
        sl = slice(h * HEAD_DIM, (h + 1) * HEAD_DIM)
        lb = lb_ref[:, sl]
        f = lb + (1.0 - lb) * _sigmoid(z_ref[:, sl])
        l = jnp.log(jnp.maximum(f, F_MIN)) * LOG2_E
        kg = 1.0 - f
        q = _silu(qd_ref[:, sl])
        v = i_ref[:, sl].astype(BF16)

        cf, tot = l, l
        att = None
        for w, mk in zip(widths, masks):
            if w == SCAN_DIRECT:
                e = (cf - l) if reverse else cf
                q_near = q * jnp.exp2(-e if reverse else e)
                k_near = kg * jnp.exp2(e if reverse else -e)
                att = jnp.where(near, _dot_nt(q_near.astype(BF16), k_near.astype(BF16)), 0.0)
            if w < SUBLANES:
                odd = (ri & w) != 0
                if w < SCAN_DIRECT:
                    x = None
                elif reverse:
                    x = jnp.where(odd, kg, q) * jnp.exp2(jnp.where(odd, cf - l, tot - cf + l))
                else:
                    x = jnp.where(odd, q, kg) * jnp.exp2(jnp.where(odd, cf, tot - cf))
                prev_tot = pltpu.roll(tot, w, 0)
                next_tot = pltpu.roll(tot, c - w, 0)
                cf = cf + jnp.where(odd, prev_tot, 0.0)
                tot = tot + jnp.where(odd, prev_tot, next_tot)
            else:
                xs, cfs, tots = [], [], []
                for k in range(c // w):
                    rows = slice(k * w, (k + 1) * w)
                    other = slice((k ^ 1) * w, ((k ^ 1) + 1) * w)
                    is_query = (k % 2 == 0) == reverse
                    if reverse:
                        arg = (tot[rows] - cf[rows] + l[rows]) if is_query else (cf[rows] - l[rows])
                    else:
                        arg = cf[rows] if is_query else (tot[rows] - cf[rows])
                    xs.append((q[rows] if is_query else kg[rows]) * jnp.exp2(arg))
                    cfs.append(cf[rows] + tot[other] if k % 2 else cf[rows])
                    tots.append(tot[rows] + tot[other])
                x = jnp.concatenate(xs, axis=0)
                cf = jnp.concatenate(cfs, axis=0)
                tot = jnp.concatenate(tots, axis=0)
            if x is not None:
                xb = x.astype(BF16)
                att = jnp.where(mk, _dot_nt(xb, xb), att)
        to_state = (tot - cf + l) if reverse else cf
        to_end = (cf - l) if reverse else (tot - cf)
        q_in = (q * jnp.exp2(to_state)).astype(BF16)
        k_out = (kg * jnp.exp2(to_end)).astype(BF16)

        st = st_ref[h]
        o = _dot_nt(q_in, st.astype(BF16)) + jnp.dot(att.astype(BF16), v, preferred_element_type=F32)
        st_ref[h] = st * jnp.exp2(tot[0:1, :]) + lax.dot_general(
            v, k_out, (((0,), (0,)), ((), ())), preferred_element_type=F32)
        if readout:
            o = o + other_ref[:, sl]
            o = o * lax.rsqrt(jnp.mean(o * o, axis=-1, keepdims=True) + EPS) * ng_ref[...]
            o_ref[:, sl] = (o * _silu(g_ref[:, sl])).astype(o_ref.dtype)
        else:
            o_ref[:, sl] = o


def _scan(p, lb, bw, t_lat, reverse, other=None, norm_g=None):
    m = p.shape[0]
    nh = bw // HEAD_DIM
    n_lat = t_lat // CHUNK
    n_all = m // CHUNK
    n_ctx = n_all - n_lat
    readout = other is not None
    if reverse:
        row_blk = lambda j: jnp.where(j < n_ctx, n_all - 1 - j, n_lat - 1 - (j - n_ctx))
    else:
        row_blk = lambda j: jnp.where(j < n_ctx, n_lat + j, j - n_ctx)
    blk = lambda c: pl.BlockSpec((CHUNK, bw), lambda j: (row_blk(j), c))
    qd_col, zf_col, zb_col, i_col, g_col = 0, 1, 2, 3, 4
    in_specs = [blk(qd_col), blk(zb_col if reverse else zf_col), blk(i_col), pl.BlockSpec((1, bw), lambda j: (0, 0))]
    args = [p, p, p, lb.reshape(1, bw)]
    if readout:
        in_specs += [blk(g_col), blk(0), pl.BlockSpec((1, HEAD_DIM), lambda j: (0, 0))]
        args += [p, other, norm_g.reshape(1, HEAD_DIM)]
    return pl.pallas_call(
        functools.partial(_scan_kernel, nh=nh, reverse=reverse, readout=readout),
        grid=(n_all,),
        in_specs=in_specs,
        out_specs=pl.BlockSpec((CHUNK, bw), lambda j: (row_blk(j), 0)),
        out_shape=jax.ShapeDtypeStruct((m, bw), BF16 if readout else F32),
        scratch_shapes=[pltpu.VMEM((nh, HEAD_DIM, HEAD_DIM), F32)],
        compiler_params=_params(("arbitrary",), 48 * CHUNK * bw * 4),
        name="hgrn2_scan_bwd" if reverse else "hgrn2_scan_fwd",
    )(*args)


def kernel(x, c, ctx, c_ctx, w_ada, b_ada, norm1_g, norm2_g, w_in, a_norm_g, a_norm_b, a_ws, a_bs, b_conv_w,
           b_conv_b, b_norm_g, b_norm_b, c_rpb, d_lb_logits, d_norm_g, w_branch, w_gate, b_gate, w_out, w_ffn1,
           w_ffn3, w_ffn2, final_g):
    batch, t_lat, d = x.shape
    n_ctx = ctx.shape[1]
    depth = w_ada.shape[0]
    bw = d // N_BRANCH
    assert batch == 1 and t_lat % GRID_W == 0 and t_lat % CHUNK == 0 and n_ctx % CHUNK == 0

    xa, xc = x[0], ctx[0]
    cond = jnp.zeros((SUBLANES, d), F32).at[0].set(c[0]).at[1].set(c_ctx)
    mod = _modulation(cond, w_ada, b_ada)

    prob = jax.nn.softmax(d_lb_logits.astype(F32), axis=1)
    lbound = jnp.cumsum(prob, axis=1) - prob[:, :1]

    cos, sin = _rope_tables(t_lat)
    bias = _bias_table(c_rpb)
    w_gate_b, w_branch_b, w2_b = (w.astype(BF16) for w in (w_gate, w_branch, w_ffn2))

    for l in range(depth):
        last = l == depth - 1
        m = t_lat if last else t_lat + n_ctx
        h = _norm_mod(xa, norm1_g[l], mod, l, 0, 1, t_lat, ctx=xc)
        p = _ws_matmul(h, w_in, l, 0, 7 * bw, BF16)
        pd = _ws_matmul(h, w_in, l, 7 * bw, 5 * bw, F32)
        ya = _mix_a(p, a_ws[l], a_bs[l], a_norm_g[l], a_norm_b[l], bw)
        yb = _mix_b(p, b_conv_w[l], b_conv_b[l], b_norm_g[l], b_norm_b[l], bw, t_lat)
        yc = _attention(p, cos, sin, bias, l, bw, t_lat, with_ctx=not last)
        o_bwd = _scan(pd, lbound[1, l], bw, t_lat, reverse=True)
        yd = _scan(pd, lbound[0, l], bw, t_lat, reverse=False, other=o_bwd, norm_g=d_norm_g[l])
        merged = _gated_merge(h, (ya, yb, yc, yd), w_gate_b, b_gate, w_branch_b, l, m)
        x1 = _ws_matmul_residual(merged, w_out, l, xa, mod, 2, t_lat, m, ctx=None if last else xc)
        hf = _norm_mod(x1, norm2_g[l], mod, l, 3, 4, t_lat)
        act = _ws_swiglu_up(hf, w_ffn1, w_ffn3, l, m)
        xa, xc = _matmul_residual(act, w2_b, l, x1, mod, 5, t_lat, m), None

    return _final_norm(xa[:t_lat], final_g)[None]
```

```python
import functools
import math

import jax
import jax.numpy as jnp
import numpy as np
from jax import lax
from jax.experimental import pallas as pl
from jax.experimental.pallas import tpu as pltpu

F32 = jnp.float32
BF16 = jnp.bfloat16

HEAD_DIM = 128
CHUNK = 128
GRID_W = 64
WIN_H = 8
WIN_W = 16
CONV_K = 31
N_BRANCH = 4
ROPE_THETA = 10000.0
EPS = 1e-6
F_MIN = 1e-6
NEG_BIG = -1e30
LOG2_E = 1.4426950408889634
SCAN_DIRECT = 4
assert (SCAN_DIRECT - 1) * -math.log2(F_MIN) < 120 <= (2 * SCAN_DIRECT - 1) * -math.log2(F_MIN)

V7X_VMEM_BYTES = 64 * 2**20
VMEM_CAP = V7X_VMEM_BYTES - 6 * 2**20
SUBLANES = 8
LANES = 128
CONV_HALO = 16
MXU_COLS = 256

WS_ROWS, WS_COLS = 1408, 512
WS_PAIR_COLS = MXU_COLS
AS_ROWS = 768
AS_ROWS_WIDE, AS_COLS_WIDE = 512, 512
ROW_TILE = 512
CONV_TILE = 256


def _pick(n, target, quantum):
    best = None
    for t in range(quantum, min(n, target) + 1, quantum):
        if n % t == 0:
            best = t
    assert best is not None, (n, target, quantum)
    return best


def _params(sem, need_bytes):
    limit = int(min(VMEM_CAP, max(need_bytes * 5 // 4, 16 * 2**20)))
    return pltpu.CompilerParams(dimension_semantics=sem, vmem_limit_bytes=limit)


def _sigmoid(x):
    return jax.nn.sigmoid(x)


def _silu(x):
    return x * jax.nn.sigmoid(x)


def _gelu(x):
    return 0.5 * x * (1.0 + lax.erf(x * 0.7071067811865476))


def _row_select(rows_is_ctx, p_ref):
    return jnp.where(rows_is_ctx, p_ref[1:2, :], p_ref[0:1, :])


def _mod_kernel(cond_ref, w_ref, b_ref, o_ref):
    s = _silu(cond_ref[...]).astype(BF16)
    o_ref[...] = jnp.dot(s, w_ref[...].astype(BF16), preferred_element_type=F32) + b_ref[...]


def _modulation(cond, w_ada, b_ada):
    depth, d, n = w_ada.shape
    tn = _pick(n, WS_COLS, LANES)
    need = 2 * d * tn * 4 + d * tn * 2
    return pl.pallas_call(
        _mod_kernel,
        grid=(depth, n // tn),
        in_specs=[
            pl.BlockSpec((SUBLANES, d), lambda l, j: (0, 0)),
            pl.BlockSpec((None, d, tn), lambda l, j: (l, 0, j)),
            pl.BlockSpec((None, 1, tn), lambda l, j: (l, 0, j)),
        ],
        out_specs=pl.BlockSpec((None, SUBLANES, tn), lambda l, j: (l, 0, j)),
        out_shape=jax.ShapeDtypeStruct((depth, SUBLANES, n), F32),
        compiler_params=_params(("arbitrary", "arbitrary"), need),
        name="adaln_modulation",
    )(cond, w_ada, b_ada.reshape(depth, 1, n))


NORM_ROWS = 8


def _row_rsqrt_stats(x_ref, rs_ref, tm):
    def stats(r, carry):
        r0 = pl.multiple_of(r * NORM_ROWS, NORM_ROWS)
        x = x_ref[pl.ds(r0, NORM_ROWS), :]
        rs_ref[pl.ds(r0, NORM_ROWS), :] = lax.rsqrt(jnp.mean(x * x, axis=-1, keepdims=True) + EPS)
        return carry

    lax.fori_loop(0, tm // NORM_ROWS, stats, 0, unroll=8)


def _norm_mod_tile(x_ref, which, g_ref, sh_ref, sc_ref, o_ref, rs_ref, gs_ref, tm):
    _row_rsqrt_stats(x_ref, rs_ref, tm)
    shape = (NORM_ROWS, x_ref.shape[1])
    gs_ref[0:NORM_ROWS, :] = jnp.broadcast_to(g_ref[...] * (1.0 + sc_ref[pl.ds(which, 1), :]), shape)
    gs_ref[NORM_ROWS:, :] = jnp.broadcast_to(sh_ref[pl.ds(which, 1), :], shape)

    def scale(r, carry):
        r0 = pl.multiple_of(r * NORM_ROWS, NORM_ROWS)
        y = x_ref[pl.ds(r0, NORM_ROWS), :] * rs_ref[pl.ds(r0, NORM_ROWS), :]
        o_ref[pl.ds(r0, NORM_ROWS), :] = (y * gs_ref[0:NORM_ROWS, :] + gs_ref[NORM_ROWS:, :]).astype(o_ref.dtype)
        return carry

    lax.fori_loop(0, tm // NORM_ROWS, scale, 0, unroll=8)


def _norm_mod_kernel(x_ref, g_ref, sh_ref, sc_ref, o_ref, rs_ref, gs_ref, *, t_lat, tm):
    which = (pl.program_id(0) * tm >= t_lat).astype(jnp.int32)
    _norm_mod_tile(x_ref, which, g_ref, sh_ref, sc_ref, o_ref, rs_ref, gs_ref, tm)


def _norm_mod2_kernel(x_ref, c_ref, g_ref, sh_ref, sc_ref, o_ref, rs_ref, gs_ref, *, lat_tiles, tm):
    @pl.when(pl.program_id(0) < lat_tiles)
    def _():
        _norm_mod_tile(x_ref, 0, g_ref, sh_ref, sc_ref, o_ref, rs_ref, gs_ref, tm)

    @pl.when(pl.program_id(0) >= lat_tiles)
    def _():
        _norm_mod_tile(c_ref, 1, g_ref, sh_ref, sc_ref, o_ref, rs_ref, gs_ref, tm)


def _norm_mod(x, g, mod, layer, shift_blk, scale_blk, t_lat, ctx=None):
    d = x.shape[1]
    m = x.shape[0] + (0 if ctx is None else ctx.shape[0])
    tm = _pick(math.gcd(t_lat, m), ROW_TILE, NORM_ROWS)
    lat_tiles = t_lat // tm
    param_specs = [
        pl.BlockSpec((1, d), lambda i: (0, 0)),
        pl.BlockSpec((None, SUBLANES, d), lambda i: (layer, 0, shift_blk)),
        pl.BlockSpec((None, SUBLANES, d), lambda i: (layer, 0, scale_blk)),
    ]
    if ctx is None:
        kern = functools.partial(_norm_mod_kernel, t_lat=t_lat, tm=tm)
        row_specs, rows = [pl.BlockSpec((tm, d), lambda i: (i, 0))], (x,)
    else:
        kern = functools.partial(_norm_mod2_kernel, lat_tiles=lat_tiles, tm=tm)
        row_specs = [pl.BlockSpec((tm, d), lambda i: (jnp.minimum(i, lat_tiles - 1), 0)),
                     pl.BlockSpec((tm, d), lambda i: (jnp.maximum(i - lat_tiles, 0), 0))]
        rows = (x, ctx)
    return pl.pallas_call(
        kern,
        grid=(m // tm,),
        in_specs=row_specs + param_specs,
        out_specs=pl.BlockSpec((tm, d), lambda i: (i, 0)),
        out_shape=jax.ShapeDtypeStruct((m, d), BF16),
        scratch_shapes=[pltpu.VMEM((tm, 1), F32), pltpu.VMEM((2 * NORM_ROWS, d), F32)],
        compiler_params=_params(("arbitrary",), 2 * tm * d * 10 + tm * LANES * 4 + 4 * SUBLANES * d * 4),
        name="rmsnorm_modulate",
    )(*rows, g.reshape(1, d), mod, mod)


def _final_norm_kernel(x_ref, g_ref, o_ref, rs_ref, *, tm):
    _row_rsqrt_stats(x_ref, rs_ref, tm)

    def scale(r, carry):
        r0 = pl.multiple_of(r * NORM_ROWS, NORM_ROWS)
        o_ref[pl.ds(r0, NORM_ROWS), :] = x_ref[pl.ds(r0, NORM_ROWS), :] * rs_ref[pl.ds(r0, NORM_ROWS), :] * g_ref[...]
        return carry

    lax.fori_loop(0, tm // NORM_ROWS, scale, 0, unroll=8)


def _final_norm(x, g):
    m, d = x.shape
    tm = _pick(m, ROW_TILE, NORM_ROWS)
    return pl.pallas_call(
        functools.partial(_final_norm_kernel, tm=tm),
        grid=(m // tm,),
        in_specs=[pl.BlockSpec((tm, d), lambda i: (i, 0)), pl.BlockSpec((NORM_ROWS, d), lambda i: (0, 0))],
        out_specs=pl.BlockSpec((tm, d), lambda i: (i, 0)),
        out_shape=jax.ShapeDtypeStruct((m, d), F32),
        scratch_shapes=[pltpu.VMEM((tm, 1), F32)],
        compiler_params=_params(("arbitrary",), 2 * tm * d * 8 + tm * LANES * 4),
        name="final_rmsnorm",
    )(x, jnp.broadcast_to(g.reshape(1, d), (NORM_ROWS, d)))


def _ws_mm_kernel(a_ref, w_ref, o_ref, wb):
    @pl.when(pl.program_id(1) == 0)
    def _():
        wb[...] = w_ref[...].astype(BF16)

    o_ref[...] = jnp.dot(a_ref[...], wb[...], preferred_element_type=F32).astype(o_ref.dtype)


def _ws_matmul(a, w, layer, col0, n, out_dtype):
    m, k = a.shape
    tm = _pick(m, WS_ROWS, LANES)
    tn = _pick(math.gcd(n, col0) if col0 else n, WS_COLS, LANES)
    jb = col0 // tn
    osz = jnp.dtype(out_dtype).itemsize
    need = 2 * (tm * k * 2 + k * tn * 4 + tm * tn * osz) + k * tn * 2 + tm * tn * 4
    return pl.pallas_call(
        _ws_mm_kernel,
        grid=(n // tn, m // tm),
        in_specs=[pl.BlockSpec((tm, k), lambda j, i: (i, 0)),
                  pl.BlockSpec((None, k, tn), lambda j, i: (layer, 0, jb + j))],
        out_specs=pl.BlockSpec((tm, tn), lambda j, i: (i, j)),
        out_shape=jax.ShapeDtypeStruct((m, n), out_dtype),
        scratch_shapes=[pltpu.VMEM((k, tn), BF16)],
        compiler_params=_params(("arbitrary", "arbitrary"), need),
        name="input_projection",
    )(a, w)


def _ws_resid_kernel(a_ref, w_ref, x_ref, gate_ref, o_ref, wb, *, t_lat, tm):
    @pl.when(pl.program_id(1) == 0)
    def _():
        wb[...] = w_ref[...].astype(BF16)

    rows = pl.program_id(1) * tm + lax.broadcasted_iota(jnp.int32, (tm, 1), 0)
    gate = _row_select(rows >= t_lat, gate_ref)
    o_ref[...] = x_ref[...] + gate * jnp.dot(a_ref[...], wb[...], preferred_element_type=F32)


def _ws_resid2_kernel(a_ref, w_ref, x_ref, c_ref, gate_ref, o_ref, wb, *, lat_rows_last):
    @pl.when(pl.program_id(1) == 0)
    def _():
        wb[...] = w_ref[...].astype(BF16)

    y = jnp.dot(a_ref[...], wb[...], preferred_element_type=F32)
    last = pl.num_programs(1) - 1

    @pl.when(pl.program_id(1) < last)
    def _():
        o_ref[...] = x_ref[...] + gate_ref[0:1, :] * y

    @pl.when(pl.program_id(1) == last)
    def _():
        k = lat_rows_last
        o_ref[0:k, :] = x_ref[0:k, :] + gate_ref[0:1, :] * y[0:k]
        o_ref[k:, :] = c_ref[...] + gate_ref[1:2, :] * y[k:]


def _ws_matmul_residual(a, w, layer, x, mod, gate_blk, t_lat, m, ctx=None):
    k = a.shape[1]
    n = w.shape[2]
    tm = _pick(m, min(WS_ROWS, t_lat), LANES)
    tn = _pick(n, WS_COLS, LANES)
    nb = n // tn
    need = 2 * (tm * k * 2 + k * tn * 4 + 2 * tm * tn * 4) + k * tn * 2 + tm * tn * 4
    specs = [
        pl.BlockSpec((tm, k), lambda j, i: (i, 0)),
        pl.BlockSpec((None, k, tn), lambda j, i: (layer, 0, j)),
        pl.BlockSpec((tm, tn), lambda j, i: (i, j)),
    ]
    if ctx is None:
        kern = functools.partial(_ws_resid_kernel, t_lat=t_lat, tm=tm)
        rows = (x,)
    else:
        n_ctx = ctx.shape[0]
        assert m == t_lat + n_ctx and n_ctx <= tm and (tm - n_ctx) % SUBLANES == 0
        kern = functools.partial(_ws_resid2_kernel, lat_rows_last=tm - n_ctx)
        specs.append(pl.BlockSpec((n_ctx, tn), lambda j, i: (0, j)))
        rows = (x, ctx)
        need += 2 * n_ctx * tn * 4
    specs.append(pl.BlockSpec((None, SUBLANES, tn), lambda j, i: (layer, 0, gate_blk * nb + j)))
    return pl.pallas_call(
        kern,
        grid=(nb, m // tm),
        in_specs=specs,
        out_specs=pl.BlockSpec((tm, tn), lambda j, i: (i, j)),
        out_shape=jax.ShapeDtypeStruct((m, n), F32),
        scratch_shapes=[pltpu.VMEM((k, tn), BF16)],
        compiler_params=_params(("arbitrary", "arbitrary"), need),
        name="output_projection",
    )(a, w, *rows, mod)


def _ws_swiglu_kernel(a_ref, w1_ref, w3_ref, o_ref, w1b, w3b):
    @pl.when(pl.program_id(1) == 0)
    def _():
        w1b[...] = w1_ref[...].astype(BF16)
        w3b[...] = w3_ref[...].astype(BF16)

    a = a_ref[...]
    u = jnp.dot(a, w1b[...], preferred_element_type=F32)
    g = jnp.dot(a, w3b[...], preferred_element_type=F32)
    o_ref[...] = (_silu(u) * g).astype(o_ref.dtype)


def _ws_swiglu_up(a, w1, w3, layer, m):
    k = a.shape[1]
    n = w1.shape[2]
    tm = _pick(m, WS_ROWS, LANES)
    tn = _pick(n, WS_PAIR_COLS, LANES)
    need = 2 * (tm * k * 2 + 2 * k * tn * 4 + tm * tn * 2) + 2 * k * tn * 2 + 3 * tm * tn * 4
    w_spec = pl.BlockSpec((None, k, tn), lambda j, i: (layer, 0, j))
    return pl.pallas_call(
        _ws_swiglu_kernel,
        grid=(n // tn, m // tm),
        in_specs=[pl.BlockSpec((tm, k), lambda j, i: (i, 0)), w_spec, w_spec],
        out_specs=pl.BlockSpec((tm, tn), lambda j, i: (i, j)),
        out_shape=jax.ShapeDtypeStruct((m, n), BF16),
        scratch_shapes=[pltpu.VMEM((k, tn), BF16), pltpu.VMEM((k, tn), BF16)],
        compiler_params=_params(("arbitrary", "arbitrary"), need),
        name="swiglu_up",
    )(a, w1, w3)


def _mm_resid_kernel(a_ref, w_ref, x_ref, gate_ref, o_ref, *, t_lat, tm):
    rows = pl.program_id(0) * tm + lax.broadcasted_iota(jnp.int32, (tm, 1), 0)
    gate = _row_select(rows >= t_lat, gate_ref)
    o_ref[...] = x_ref[...] + gate * jnp.dot(a_ref[...], w_ref[...], preferred_element_type=F32)


def _matmul_residual(a, w, layer, x, mod, gate_blk, t_lat, m):
    k = a.shape[1]
    n = w.shape[2]
    tm = _pick(m, AS_ROWS, MXU_COLS)
    tn = _pick(n, MXU_COLS if tm > AS_ROWS_WIDE else AS_COLS_WIDE, LANES)
    need = 2 * (tm * k * 2 + k * tn * 2 + 2 * tm * tn * 4) + tm * tn * 4
    nb = n // tn
    return pl.pallas_call(
        functools.partial(_mm_resid_kernel, t_lat=t_lat, tm=tm),
        grid=(m // tm, nb),
        in_specs=[
            pl.BlockSpec((tm, k), lambda i, j: (i, 0)),
            pl.BlockSpec((None, k, tn), lambda i, j: (layer, 0, j)),
            pl.BlockSpec((tm, tn), lambda i, j: (i, j)),
            pl.BlockSpec((None, SUBLANES, tn), lambda i, j: (layer, 0, gate_blk * nb + j)),
        ],
        out_specs=pl.BlockSpec((tm, tn), lambda i, j: (i, j)),
        out_shape=jax.ShapeDtypeStruct((m, n), F32),
        compiler_params=_params(("arbitrary", "arbitrary"), need),
        name="ffn_down",
    )(a, w, x, mod)


def _merge_kernel(h_ref, ya_ref, yb_ref, yc_ref, yd_ref, wg0, wg1, wg2, wg3, bg0, bg1, bg2, bg3, wb_ref, o_ref):
    h = h_ref[...]
    acc = None
    for i, (y_ref, wg, bg) in enumerate(((ya_ref, wg0, bg0), (yb_ref, wg1, bg1), (yc_ref, wg2, bg2), (yd_ref, wg3, bg3))):
        gate = _sigmoid(jnp.dot(h, wg[...], preferred_element_type=F32) + bg[...])
        term = gate * jnp.dot(y_ref[...], wb_ref[i], preferred_element_type=F32)
        acc = term if acc is None else acc + term
    o_ref[...] = acc.astype(o_ref.dtype)


def _gated_merge(h, ys, w_gate, b_gate, w_branch, layer, m):
    k = h.shape[1]
    depth, nbr, bw, n = w_branch.shape
    tm = _pick(m, AS_ROWS, MXU_COLS)
    tn = _pick(n, MXU_COLS, LANES)
    nb = n // tn
    need = 2 * (tm * k * 2 + nbr * tm * bw * 2 + nbr * k * tn * 2 + nbr * bw * tn * 2 + tm * tn * 2) + 4 * tm * tn * 4
    y_spec = pl.BlockSpec((tm, bw), lambda i, j: (i, 0))
    wg_specs = [pl.BlockSpec((None, k, tn), functools.partial(lambda i, j, b: (layer, 0, b * nb + j), b=b))
                for b in range(nbr)]
    bg_specs = [pl.BlockSpec((None, 1, tn), functools.partial(lambda i, j, b: (layer, 0, b * nb + j), b=b))
                for b in range(nbr)]
    bg = b_gate.reshape(depth, 1, -1)
    return pl.pallas_call(
        _merge_kernel,
        grid=(m // tm, nb),
        in_specs=[pl.BlockSpec((tm, k), lambda i, j: (i, 0)), y_spec, y_spec, y_spec, y_spec]
        + wg_specs + bg_specs + [pl.BlockSpec((None, nbr, bw, tn), lambda i, j: (layer, 0, 0, j))],
        out_specs=pl.BlockSpec((tm, tn), lambda i, j: (i, j)),
        out_shape=jax.ShapeDtypeStruct((m, n), BF16),
        compiler_params=_params(("arbitrary", "arbitrary"), need),
        name="gated_merge",
    )(h, *ys, w_gate, w_gate, w_gate, w_gate, bg, bg, bg, bg, w_branch)


def _mix_a_kernel(pu_ref, pv_ref, ws_ref, bias_ref, ng_ref, nb_ref, o_ref, *, nh):
    zu = _gelu(pu_ref[...].astype(F32))
    zv = _gelu(pv_ref[...].astype(F32))
    mu = jnp.mean(zv, axis=-1, keepdims=True)
    zc = zv - mu
    var = jnp.mean(zc * zc, axis=-1, keepdims=True)
    v = (zc * lax.rsqrt(var + EPS) * ng_ref[...] + nb_ref[...]).astype(BF16)
    for g in range(nh):
        sl = slice(g * HEAD_DIM, (g + 1) * HEAD_DIM)
        s = jnp.dot(ws_ref[g], v[:, sl], preferred_element_type=F32) + bias_ref[:, sl]
        o_ref[:, sl] = (zu[:, sl] * s).astype(o_ref.dtype)


def _mix_a(p, ws, bs, ng, nb, bw):
    m = p.shape[0]
    nh = bw // HEAD_DIM
    bias = jnp.repeat(bs.T, HEAD_DIM, axis=1)
    return pl.pallas_call(
        functools.partial(_mix_a_kernel, nh=nh),
        grid=(m // CHUNK,),
        in_specs=[
            pl.BlockSpec((CHUNK, bw), lambda i: (i, 0)),
            pl.BlockSpec((CHUNK, bw), lambda i: (i, 1)),
            pl.BlockSpec((nh, CHUNK, CHUNK), lambda i: (0, 0, 0)),
            pl.BlockSpec((CHUNK, bw), lambda i: (0, 0)),
            pl.BlockSpec((1, bw), lambda i: (0, 0)),
            pl.BlockSpec((1, bw), lambda i: (0, 0)),
        ],
        out_specs=pl.BlockSpec((CHUNK, bw), lambda i: (i, 0)),
        out_shape=jax.ShapeDtypeStruct((m, bw), BF16),
        compiler_params=_params(("arbitrary",), 16 * CHUNK * bw * 4),
        name="mixer_gmlp",
    )(p, p, ws.astype(BF16), bias, ng.reshape(1, bw), nb.reshape(1, bw))


CONV_ROWS = 32


def _mix_b_kernel(ap_ref, ac_ref, an_ref, gp_ref, gc_ref, gn_ref, cw_ref, cb_ref, ng_ref, nb_ref, o_ref, ysh, yconv,
                  *, tiles_lat, tiles_tot, tmb):
    i = pl.program_id(0)
    first = jnp.logical_or(i == 0, i == tiles_lat)
    last = jnp.logical_or(i == tiles_lat - 1, i == tiles_tot - 1)
    glu = lambda a_ref, g_ref: a_ref[...].astype(F32) * _sigmoid(g_ref[...].astype(F32))
    yp = glu(ap_ref, gp_ref)
    yn = glu(an_ref, gn_ref)
    ysh[0, 0:CONV_HALO, :] = jnp.where(first, 0.0, yp)
    ysh[0, CONV_HALO:CONV_HALO + tmb, :] = glu(ac_ref, gc_ref)
    ysh[0, CONV_HALO + tmb:, :] = jnp.where(last, 0.0, yn)
    nrow = tmb + 2 * CONV_HALO
    for c0 in range(0, ysh.shape[2], LANES):
        base = ysh[0, :, c0:c0 + LANES]
        for b in range(1, SUBLANES):
            ysh[b, :, c0:c0 + LANES] = pltpu.roll(base, nrow - b, 0)
    off = CONV_HALO - CONV_K // 2

    bw = ysh.shape[2]
    groups = CONV_ROWS // SUBLANES

    def conv_block(rb, carry):
        r0 = pl.multiple_of(rb * CONV_ROWS, CONV_ROWS)
        acc = [None] * groups
        for j in range(CONV_K):
            b = (off + j) % SUBLANES
            wj = cw_ref[j * SUBLANES:(j + 1) * SUBLANES, :]
            for g in range(groups):
                term = ysh[b, pl.ds(r0 + (off + j - b + g * SUBLANES), SUBLANES), :] * wj
                acc[g] = term if acc[g] is None else acc[g] + term
        for g in range(groups):
            yconv[pl.ds(r0 + g * SUBLANES, SUBLANES), :] = acc[g]
        return carry

    lax.fori_loop(0, tmb // CONV_ROWS, conv_block, 0)

    def norm_group(rg, carry):
        r0 = pl.multiple_of(rg * SUBLANES, SUBLANES)
        y = yconv[pl.ds(r0, SUBLANES), :] + cb_ref[...]
        mu = jnp.mean(y, axis=-1, keepdims=True)
        yc = y - mu
        var = jnp.mean(yc * yc, axis=-1, keepdims=True)
        z = yc * lax.rsqrt(var + EPS) * ng_ref[...] + nb_ref[...]
        o_ref[pl.ds(r0, SUBLANES), :] = _silu(z).astype(o_ref.dtype)
        return carry

    lax.fori_loop(0, tmb // SUBLANES, norm_group, 0, unroll=8)


def _mix_b(p, cw, cb, ng, nb, bw, t_lat):
    m = p.shape[0]
    tmb = _pick(math.gcd(t_lat, m - t_lat), CONV_TILE, CONV_ROWS)
    hb = tmb // CONV_HALO
    nhalo = m // CONV_HALO
    a_col, g_col = 2, 3
    prev = lambda c: (lambda i: (jnp.maximum(i * hb - 1, 0), c))
    cur = lambda c: (lambda i: (i, c))
    nxt = lambda c: (lambda i: (jnp.minimum((i + 1) * hb, nhalo - 1), c))
    vec = pl.BlockSpec((SUBLANES, bw), lambda i: (0, 0))
    rep = lambda a: jnp.broadcast_to(a.reshape(-1, 1, bw), (a.size // bw, SUBLANES, bw)).reshape(-1, bw)
    return pl.pallas_call(
        functools.partial(_mix_b_kernel, tiles_lat=t_lat // tmb, tiles_tot=m // tmb, tmb=tmb),
        grid=(m // tmb,),
        in_specs=[
            pl.BlockSpec((CONV_HALO, bw), prev(a_col)),
            pl.BlockSpec((tmb, bw), cur(a_col)),
            pl.BlockSpec((CONV_HALO, bw), nxt(a_col)),
            pl.BlockSpec((CONV_HALO, bw), prev(g_col)),
            pl.BlockSpec((tmb, bw), cur(g_col)),
            pl.BlockSpec((CONV_HALO, bw), nxt(g_col)),
            pl.BlockSpec((CONV_K * SUBLANES, bw), lambda i: (0, 0)),
            vec, vec, vec,
        ],
        out_specs=pl.BlockSpec((tmb, bw), lambda i: (i, 0)),
        out_shape=jax.ShapeDtypeStruct((m, bw), BF16),
        scratch_shapes=[pltpu.VMEM((SUBLANES, tmb + 2 * CONV_HALO, bw), F32), pltpu.VMEM((tmb, bw), F32)],
        compiler_params=_params(("arbitrary",), 20 * tmb * bw * 4),
        name="mixer_conformer_conv",
    )(p, p, p, p, p, p, rep(cw), rep(cb), rep(ng), rep(nb))


def _rope_tables(t_lat):
    m = HEAD_DIM // 4
    freqs = np.power(np.float32(ROPE_THETA), -np.arange(m, dtype=np.float32) / np.float32(m)).astype(np.float32)
    r = t_lat // GRID_W
    ang_r = np.repeat(np.arange(r, dtype=np.float32)[:, None] * freqs[None, :], GRID_W, axis=0)
    ang_c = np.tile(np.arange(GRID_W, dtype=np.float32)[:, None] * freqs[None, :], (r, 1))
    cos = np.concatenate([np.cos(ang_r), np.cos(ang_r), np.cos(ang_c), np.cos(ang_c)], axis=-1)
    sin = np.concatenate([-np.sin(ang_r), np.sin(ang_r), -np.sin(ang_c), np.sin(ang_c)], axis=-1)
    return jnp.asarray(cos, F32), jnp.asarray(sin, F32)


def _rope(x, cos, sin):
    lane = lax.broadcasted_iota(jnp.int32, x.shape, 1)
    low = (lane % (HEAD_DIM // 2)) < (HEAD_DIM // 4)
    swapped = jnp.where(low, pltpu.roll(x, HEAD_DIM - HEAD_DIM // 4, 1), pltpu.roll(x, HEAD_DIM // 4, 1))
    return x * cos + swapped * sin


ATTN_QROWS = 4
ATTN_KROWS = ATTN_QROWS + WIN_H
ATTN_SHIFT = WIN_H // 2
ATTN_VARIANTS = WIN_H // ATTN_QROWS + 1


def _bias_kernel(rpb_ref, o_ref, *, ndr, ndc):
    base = pl.program_id(0) * (ndr * ndc)
    q = lax.broadcasted_iota(jnp.int32, (GRID_W, GRID_W), 0)
    k = lax.broadcasted_iota(jnp.int32, (GRID_W, GRID_W), 1)
    dc = jnp.clip(k - q, -(WIN_W - 1), WIN_W - 1) + (WIN_W - 1)
    start = jnp.clip(q - WIN_W // 2, 0, GRID_W - WIN_W)
    in_win = jnp.logical_and(k >= start, k < start + WIN_W)
    masked = jnp.full((GRID_W, GRID_W), NEG_BIG, F32)
    by_row_offset = []
    for dr in range(ndr):
        val = jnp.zeros((GRID_W, GRID_W), F32)
        for d in range(ndc):
            val = jnp.where(dc == d, rpb_ref[base + dr * ndc + d], val)
        by_row_offset.append(jnp.where(in_win, val * LOG2_E, NEG_BIG))
    for v in range(ATTN_VARIANTS):
        block_off = v * ATTN_QROWS
        for qr in range(ATTN_QROWS):
            first = min(max(block_off + qr - ATTN_SHIFT, 0), ATTN_KROWS - WIN_H)
            cols = [by_row_offset[j - qr - block_off + WIN_H - 1] if first <= j < first + WIN_H else masked
                    for j in range(ATTN_KROWS)]
            for j in range(0, ATTN_KROWS, 2):
                o_ref[v, qr * GRID_W:(qr + 1) * GRID_W, j * GRID_W:(j + 2) * GRID_W] = jnp.concatenate(
                    cols[j:j + 2], axis=-1)


def _bias_table(rpb):
    depth, nh, ndr, ndc = rpb.shape
    nq, nk = ATTN_QROWS * GRID_W, ATTN_KROWS * GRID_W
    assert 2 * GRID_W == LANES and ATTN_KROWS % 2 == 0
    tab = pl.pallas_call(
        functools.partial(_bias_kernel, ndr=ndr, ndc=ndc),
        grid=(depth * nh,),
        in_specs=[pl.BlockSpec(memory_space=pltpu.SMEM)],
        out_specs=pl.BlockSpec((None, ATTN_VARIANTS, nq, nk), lambda i: (i, 0, 0, 0)),
        out_shape=jax.ShapeDtypeStruct((depth * nh, ATTN_VARIANTS, nq, nk), F32),
        compiler_params=_params(("arbitrary",), 4 * ATTN_VARIANTS * nq * nk * 4),
        name="attention_bias_table",
    )(rpb.reshape(-1))
    return tab.reshape(depth, nh, ATTN_VARIANTS, nq, nk)


ATTN_UNROLL = 4
ATTN_PREP_ROWS = 256


def _softmax_pv(scores, values):
    mx = None
    for s in scores:
        smax = jnp.max(s, axis=-1, keepdims=True)
        mx = smax if mx is None else jnp.maximum(mx, smax)
    den = None
    acc = None
    for s, v in zip(scores, values):
        e = jnp.exp2(s - mx)
        esum = jnp.sum(e, axis=-1, keepdims=True)
        den = esum if den is None else den + esum
        pv = jnp.dot(e.astype(BF16), v, preferred_element_type=F32)
        acc = pv if acc is None else acc + pv
    return acc / den


def _dot_nt(a, b):
    return lax.dot_general(a, b, (((1,), (1,)), ((), ())), preferred_element_type=F32)


def _attn_kernel(q_ref, k_ref, v_ref, qc_ref, kc_ref, vc_ref, cos_ref, sin_ref, bias_ref, o_ref, kr_ref,
                 *, rows, t_lat, with_ctx):
    scale = HEAD_DIM ** -0.5 * LOG2_E
    kc = kc_ref[...]
    vc = vc_ref[...]
    nq = ATTN_QROWS * GRID_W
    nk = ATTN_KROWS * GRID_W

    def prep(c, carry):
        rs = pl.ds(pl.multiple_of(c * ATTN_PREP_ROWS, ATTN_PREP_ROWS), ATTN_PREP_ROWS)
        kr_ref[rs, :] = _rope(k_ref[rs, :].astype(F32), cos_ref[rs, :], sin_ref[rs, :]).astype(BF16)
        return carry

    lax.fori_loop(0, t_lat // ATTN_PREP_ROWS, prep, 0)

    def body(blk, carry):
        r0 = blk * ATTN_QROWS
        s0 = jnp.clip(r0 - ATTN_SHIFT, 0, rows - ATTN_KROWS)
        qs = pl.ds(pl.multiple_of(r0 * GRID_W, nq), nq)
        ks = pl.ds(pl.multiple_of(s0 * GRID_W, GRID_W), nk)
        q = q_ref[qs, :]
        q_rot = _rope(q.astype(F32), cos_ref[qs, :], sin_ref[qs, :]).astype(BF16)
        s_lat = _dot_nt(q_rot, kr_ref[ks, :]) * scale
        b = bias_ref[(r0 - s0) // ATTN_QROWS]
        s_lat = jnp.where(b > 0.5 * NEG_BIG, s_lat + b, NEG_BIG)
        s_ctx = _dot_nt(q, kc) * scale
        o = _softmax_pv((s_lat, s_ctx), (v_ref[ks, :], vc))
        o_ref[qs, :] = o.astype(o_ref.dtype)
        return carry

    lax.fori_loop(0, rows // ATTN_QROWS, body, 0, unroll=ATTN_UNROLL)
    if with_ctx:
        s = _dot_nt(qc_ref[...], kc) * scale
        o_ref[t_lat:, :] = _softmax_pv((s,), (vc,)).astype(o_ref.dtype)


def _attention(p, cos, sin, bias, layer, bw, t_lat, with_ctx):
    m = p.shape[0]
    n_ctx = m - t_lat
    nh = bw // HEAD_DIM
    rows = t_lat // GRID_W
    assert rows >= ATTN_KROWS and rows % ATTN_QROWS == 0 and t_lat % n_ctx == 0 and t_lat % ATTN_PREP_ROWS == 0
    m_out = m if with_ctx else t_lat
    q_col, k_col, v_col = 4 * nh, 5 * nh, 6 * nh
    lat = lambda c: pl.BlockSpec((t_lat, HEAD_DIM), lambda h: (0, c + h))
    ctx = lambda c: pl.BlockSpec((n_ctx, HEAD_DIM), lambda h: (t_lat // n_ctx, c + h))
    table = pl.BlockSpec((t_lat, HEAD_DIM), lambda h: (0, 0), pipeline_mode=pl.Buffered(1))
    nq, nk = ATTN_QROWS * GRID_W, ATTN_KROWS * GRID_W
    need = (2 * 3 * t_lat * HEAD_DIM * 2 + 2 * t_lat * HEAD_DIM * 4 + t_lat * HEAD_DIM * 2
            + 2 * m_out * HEAD_DIM * 2 + 2 * ATTN_VARIANTS * nq * nk * 4 + 8 * nq * (nk + n_ctx) * 4)
    return pl.pallas_call(
        functools.partial(_attn_kernel, rows=rows, t_lat=t_lat, with_ctx=with_ctx),
        grid=(nh,),
        in_specs=[lat(q_col), lat(k_col), lat(v_col), ctx(q_col), ctx(k_col), ctx(v_col), table, table,
                  pl.BlockSpec((None, None, ATTN_VARIANTS, nq, nk), lambda h: (layer, h, 0, 0, 0))],
        out_specs=pl.BlockSpec((m_out, HEAD_DIM), lambda h: (0, h)),
        out_shape=jax.ShapeDtypeStruct((m_out, bw), BF16),
        scratch_shapes=[pltpu.VMEM((t_lat, HEAD_DIM), BF16)],
        compiler_params=_params(("arbitrary",), need),
        name="neighbourhood_attention",
    )(p, p, p, p, p, p, cos, sin, bias)


def _scan_kernel(*refs, nh, reverse, readout):
    if readout:
        qd_ref, z_ref, i_ref, lb_ref, g_ref, other_ref, ng_ref, o_ref, st_ref = refs
    else:
        qd_ref, z_ref, i_ref, lb_ref, o_ref, st_ref = refs
    c = CHUNK

    @pl.when(pl.program_id(0) == 0)
    def _():
        st_ref[...] = jnp.zeros_like(st_ref)

    ri = lax.broadcasted_iota(jnp.int32, (c, 1), 0)
    ti = lax.broadcasted_iota(jnp.int32, (c, c), 0)
    si = lax.broadcasted_iota(jnp.int32, (c, c), 1)
    shift = SCAN_DIRECT.bit_length() - 1
    near = jnp.logical_and((ti >> shift) == (si >> shift), (si >= ti) if reverse else (si <= ti))
    widths = [1 << k for k in range(c.bit_length() - 1)]
    masks = [jnp.logical_and(((ti ^ si) >> (w.bit_length() - 1)) == 1,
                             ((ti & w) == 0) if reverse else ((ti & w) != 0)) for w in widths]

    for h in range(nh):
        sl = slice(h * HEAD_DIM, (h + 1) * HEAD_DIM)
        lb = lb_ref[:, sl]
        f = lb + (1.0 - lb) * _sigmoid(z_ref[:, sl])
        l = jnp.log(jnp.maximum(f, F_MIN)) * LOG2_E
        kg = 1.0 - f
        q = _silu(qd_ref[:, sl])
        v = i_ref[:, sl].astype(BF16)

        cf, tot = l, l
        att = None
        for w, mk in zip(widths, masks):
            if w == SCAN_DIRECT:
                e = (cf - l) if reverse else cf
                q_near = q * jnp.exp2(-e if reverse else e)
                k_near = kg * jnp.exp2(e if reverse else -e)
                att = jnp.where(near, _dot_nt(q_near.astype(BF16), k_near.astype(BF16)), 0.0)
            if w < SUBLANES:
                odd = (ri & w) != 0
                if w < SCAN_DIRECT:
                    x = None
                elif reverse:
                    x = jnp.where(odd, kg, q) * jnp.exp2(jnp.where(odd, cf - l, tot - cf + l))
                else:
                    x = jnp.where(odd, q, kg) * jnp.exp2(jnp.where(odd, cf, tot - cf))
                prev_tot = pltpu.roll(tot, w, 0)
                next_tot = pltpu.roll(tot, c - w, 0)
                cf = cf + jnp.where(odd, prev_tot, 0.0)
                tot = tot + jnp.where(odd, prev_tot, next_tot)
            else:
                xs, cfs, tots = [], [], []
                for k in range(c // w):
                    rows = slice(k * w, (k + 1) * w)
                    other = slice((k ^ 1) * w, ((k ^ 1) + 1) * w)
                    is_query = (k % 2 == 0) == reverse
                    if reverse:
                        arg = (tot[rows] - cf[rows] + l[rows]) if is_query else (cf[rows] - l[rows])
                    else:
                        arg = cf[rows] if is_query else (tot[rows] - cf[rows])
                    xs.append((q[rows] if is_query else kg[rows]) * jnp.exp2(arg))
                    cfs.append(cf[rows] + tot[other] if k % 2 else cf[rows])
                    tots.append(tot[rows] + tot[other])
                x = jnp.concatenate(xs, axis=0)
                cf = jnp.concatenate(cfs, axis=0)
                tot = jnp.concatenate(tots, axis=0)
            if x is not None:
                xb = x.astype(BF16)
                att = jnp.where(mk, _dot_nt(xb, xb), att)
        to_state = (tot - cf + l) if reverse else cf
        to_end = (cf - l) if reverse else (tot - cf)
        q_in = (q * jnp.exp2(to_state)).astype(BF16)
        k_out = (kg * jnp.exp2(to_end)).astype(BF16)

        st = st_ref[h]
        o = _dot_nt(q_in, st.astype(BF16)) + jnp.dot(att.astype(BF16), v, preferred_element_type=F32)
        st_ref[h] = st * jnp.exp2(tot[0:1, :]) + lax.dot_general(
            v, k_out, (((0,), (0,)), ((), ())), preferred_element_type=F32)
        if readout:
            o = o + other_ref[:, sl]
            o = o * lax.rsqrt(jnp.mean(o * o, axis=-1, keepdims=True) + EPS) * ng_ref[...]
            o_ref[:, sl] = (o * _silu(g_ref[:, sl])).astype(o_ref.dtype)
        else:
            o_ref[:, sl] = o


def _scan(p, lb, bw, t_lat, reverse, other=None, norm_g=None):
    m = p.shape[0]
    nh = bw // HEAD_DIM
    n_lat = t_lat // CHUNK
    n_all = m // CHUNK
    n_ctx = n_all - n_lat
    readout = other is not None
    if reverse:
        row_blk = lambda j: jnp.where(j < n_ctx, n_all - 1 - j, n_lat - 1 - (j - n_ctx))
    else:
        row_blk = lambda j: jnp.where(j < n_ctx, n_lat + j, j - n_ctx)
    blk = lambda c: pl.BlockSpec((CHUNK, bw), lambda j: (row_blk(j), c))
    qd_col, zf_col, zb_col, i_col, g_col = 0, 1, 2, 3, 4
    in_specs = [blk(qd_col), blk(zb_col if reverse else zf_col), blk(i_col), pl.BlockSpec((1, bw), lambda j: (0, 0))]
    args = [p, p, p, lb.reshape(1, bw)]
    if readout:
        in_specs += [blk(g_col), blk(0), pl.BlockSpec((1, HEAD_DIM), lambda j: (0, 0))]
        args += [p, other, norm_g.reshape(1, HEAD_DIM)]
    return pl.pallas_call(
        functools.partial(_scan_kernel, nh=nh, reverse=reverse, readout=readout),
        grid=(n_all,),
        in_specs=in_specs,
        out_specs=pl.BlockSpec((CHUNK, bw), lambda j: (row_blk(j), 0)),
        out_shape=jax.ShapeDtypeStruct((m, bw), BF16 if readout else F32),
        scratch_shapes=[pltpu.VMEM((nh, HEAD_DIM, HEAD_DIM), F32)],
        compiler_params=_params(("arbitrary",), 48 * CHUNK * bw * 4),
        name="hgrn2_scan_bwd" if reverse else "hgrn2_scan_fwd",
    )(*args)


def kernel(x, c, ctx, c_ctx, w_ada, b_ada, norm1_g, norm2_g, w_in, a_norm_g, a_norm_b, a_ws, a_bs, b_conv_w,
           b_conv_b, b_norm_g, b_norm_b, c_rpb, d_lb_logits, d_norm_g, w_branch, w_gate, b_gate, w_out, w_ffn1,
           w_ffn3, w_ffn2, final_g):
    batch, t_lat, d = x.shape
    n_ctx = ctx.shape[1]
    depth = w_ada.shape[0]
    bw = d // N_BRANCH
    assert batch == 1 and t_lat % GRID_W == 0 and t_lat % CHUNK == 0 and n_ctx % CHUNK == 0

    xa, xc = x[0], ctx[0]
    cond = jnp.zeros((SUBLANES, d), F32).at[0].set(c[0]).at[1].set(c_ctx)
    mod = _modulation(cond, w_ada, b_ada)

    prob = jax.nn.softmax(d_lb_logits.astype(F32), axis=1)
    lbound = jnp.cumsum(prob, axis=1) - prob[:, :1]

    cos, sin = _rope_tables(t_lat)
    bias = _bias_table(c_rpb)
    w_gate_b, w_branch_b, w2_b = (w.astype(BF16) for w in (w_gate, w_branch, w_ffn2))

    for l in range(depth):
        last = l == depth - 1
        m = t_lat if last else t_lat + n_ctx
        h = _norm_mod(xa, norm1_g[l], mod, l, 0, 1, t_lat, ctx=xc)
        p = _ws_matmul(h, w_in, l, 0, 7 * bw, BF16)
        pd = _ws_matmul(h, w_in, l, 7 * bw, 5 * bw, F32)
        ya = _mix_a(p, a_ws[l], a_bs[l], a_norm_g[l], a_norm_b[l], bw)
        yb = _mix_b(p, b_conv_w[l], b_conv_b[l], b_norm_g[l], b_norm_b[l], bw, t_lat)
        yc = _attention(p, cos, sin, bias, l, bw, t_lat, with_ctx=not last)
        o_bwd = _scan(pd, lbound[1, l], bw, t_lat, reverse=True)
        yd = _scan(pd, lbound[0, l], bw, t_lat, reverse=False, other=o_bwd, norm_g=d_norm_g[l])
        merged = _gated_merge(h, (ya, yb, yc, yd), w_gate_b, b_gate, w_branch_b, l, m)
        x1 = _ws_matmul_residual(merged, w_out, l, xa, mod, 2, t_lat, m, ctx=None if last else xc)
        hf = _norm_mod(x1, norm2_g[l], mod, l, 3, 4, t_lat)
        act = _ws_swiglu_up(hf, w_ffn1, w_ffn3, l, m)
        xa, xc = _matmul_residual(act, w2_b, l, x1, mod, 5, t_lat, m), None

    return _final_norm(xa[:t_lat], final_g)[None]
```

```python
import functools
import math

import jax
import jax.numpy as jnp
import numpy as np
from jax import lax
from jax.experimental import pallas as pl
from jax.experimental.pallas import tpu as pltpu

F32 = jnp.float32
BF16 = jnp.bfloat16

HEAD_DIM = 128
CHUNK = 128
GRID_W = 64
WIN_H = 8
WIN_W = 16
CONV_K = 31
N_BRANCH = 4
ROPE_THETA = 10000.0
EPS = 1e-6
F_MIN = 1e-6
NEG_BIG = -1e30
LOG2_E = 1.4426950408889634
SCAN_DIRECT = 4
assert (SCAN_DIRECT - 1) * -math.log2(F_MIN) < 120 <= (2 * SCAN_DIRECT - 1) * -math.log2(F_MIN)

V7X_VMEM_BYTES = 64 * 2**20
VMEM_CAP = V7X_VMEM_BYTES - 6 * 2**20
SUBLANES = 8
LANES = 128
CONV_HALO = 16
MXU_COLS = 256

WS_ROWS, WS_COLS = 1408, 512
WS_PAIR_COLS = MXU_COLS
AS_ROWS = 768
AS_ROWS_WIDE, AS_COLS_WIDE = 512, 512
ROW_TILE = 512
CONV_TILE = 256


def _pick(n, target, quantum):
    best = None
    for t in range(quantum, min(n, target) + 1, quantum):
        if n % t == 0:
            best = t
    assert best is not None, (n, target, quantum)
    return best


def _params(sem, need_bytes):
    limit = int(min(VMEM_CAP, max(need_bytes * 5 // 4, 16 * 2**20)))
    return pltpu.CompilerParams(dimension_semantics=sem, vmem_limit_bytes=limit)


def _sigmoid(x):
    return jax.nn.sigmoid(x)


def _silu(x):
    return x * jax.nn.sigmoid(x)


def _gelu(x):
    return 0.5 * x * (1.0 + lax.erf(x * 0.7071067811865476))


def _row_select(rows_is_ctx, p_ref):
    return jnp.where(rows_is_ctx, p_ref[1:2, :], p_ref[0:1, :])


def _mod_kernel(cond_ref, w_ref, b_ref, o_ref):
    s = _silu(cond_ref[...]).astype(BF16)
    o_ref[...] = jnp.dot(s, w_ref[...].astype(BF16), preferred_element_type=F32) + b_ref[...]


def _modulation(cond, w_ada, b_ada):
    depth, d, n = w_ada.shape
    tn = _pick(n, WS_COLS, LANES)
    need = 2 * d * tn * 4 + d * tn * 2
    return pl.pallas_call(
        _mod_kernel,
        grid=(depth, n // tn),
        in_specs=[
            pl.BlockSpec((SUBLANES, d), lambda l, j: (0, 0)),
            pl.BlockSpec((None, d, tn), lambda l, j: (l, 0, j)),
            pl.BlockSpec((None, 1, tn), lambda l, j: (l, 0, j)),
        ],
        out_specs=pl.BlockSpec((None, SUBLANES, tn), lambda l, j: (l, 0, j)),
        out_shape=jax.ShapeDtypeStruct((depth, SUBLANES, n), F32),
        compiler_params=_params(("arbitrary", "arbitrary"), need),
        name="adaln_modulation",
    )(cond, w_ada, b_ada.reshape(depth, 1, n))


NORM_ROWS = 8


def _row_rsqrt_stats(x_ref, rs_ref, tm):
    def stats(r, carry):
        r0 = pl.multiple_of(r * NORM_ROWS, NORM_ROWS)
        x = x_ref[pl.ds(r0, NORM_ROWS), :]
        rs_ref[pl.ds(r0, NORM_ROWS), :] = lax.rsqrt(jnp.mean(x * x, axis=-1, keepdims=True) + EPS)
        return carry

    lax.fori_loop(0, tm // NORM_ROWS, stats, 0, unroll=8)


def _norm_mod_tile(x_ref, which, g_ref, sh_ref, sc_ref, o_ref, rs_ref, gs_ref, tm):
    _row_rsqrt_stats(x_ref, rs_ref, tm)
    shape = (NORM_ROWS, x_ref.shape[1])
    gs_ref[0:NORM_ROWS, :] = jnp.broadcast_to(g_ref[...] * (1.0 + sc_ref[pl.ds(which, 1), :]), shape)
    gs_ref[NORM_ROWS:, :] = jnp.broadcast_to(sh_ref[pl.ds(which, 1), :], shape)

    def scale(r, carry):
        r0 = pl.multiple_of(r * NORM_ROWS, NORM_ROWS)
        y = x_ref[pl.ds(r0, NORM_ROWS), :] * rs_ref[pl.ds(r0, NORM_ROWS), :]
        o_ref[pl.ds(r0, NORM_ROWS), :] = (y * gs_ref[0:NORM_ROWS, :] + gs_ref[NORM_ROWS:, :]).astype(o_ref.dtype)
        return carry

    lax.fori_loop(0, tm // NORM_ROWS, scale, 0, unroll=8)


def _norm_mod_kernel(x_ref, g_ref, sh_ref, sc_ref, o_ref, rs_ref, gs_ref, *, t_lat, tm):
    which = (pl.program_id(0) * tm >= t_lat).astype(jnp.int32)
    _norm_mod_tile(x_ref, which, g_ref, sh_ref, sc_ref, o_ref, rs_ref, gs_ref, tm)


def _norm_mod2_kernel(x_ref, c_ref, g_ref, sh_ref, sc_ref, o_ref, rs_ref, gs_ref, *, lat_tiles, tm):
    @pl.when(pl.program_id(0) < lat_tiles)
    def _():
        _norm_mod_tile(x_ref, 0, g_ref, sh_ref, sc_ref, o_ref, rs_ref, gs_ref, tm)

    @pl.when(pl.program_id(0) >= lat_tiles)
    def _():
        _norm_mod_tile(c_ref, 1, g_ref, sh_ref, sc_ref, o_ref, rs_ref, gs_ref, tm)


def _norm_mod(x, g, mod, layer, shift_blk, scale_blk, t_lat, ctx=None):
    d = x.shape[1]
    m = x.shape[0] + (0 if ctx is None else ctx.shape[0])
    tm = _pick(math.gcd(t_lat, m), ROW_TILE, NORM_ROWS)
    lat_tiles = t_lat // tm
    param_specs = [
        pl.BlockSpec((1, d), lambda i: (0, 0)),
        pl.BlockSpec((None, SUBLANES, d), lambda i: (layer, 0, shift_blk)),
        pl.BlockSpec((None, SUBLANES, d), lambda i: (layer, 0, scale_blk)),
    ]
    if ctx is None:
        kern = functools.partial(_norm_mod_kernel, t_lat=t_lat, tm=tm)
        row_specs, rows = [pl.BlockSpec((tm, d), lambda i: (i, 0))], (x,)
    else:
        kern = functools.partial(_norm_mod2_kernel, lat_tiles=lat_tiles, tm=tm)
        row_specs = [pl.BlockSpec((tm, d), lambda i: (jnp.minimum(i, lat_tiles - 1), 0)),
                     pl.BlockSpec((tm, d), lambda i: (jnp.maximum(i - lat_tiles, 0), 0))]
        rows = (x, ctx)
    return pl.pallas_call(
        kern,
        grid=(m // tm,),
        in_specs=row_specs + param_specs,
        out_specs=pl.BlockSpec((tm, d), lambda i: (i, 0)),
        out_shape=jax.ShapeDtypeStruct((m, d), BF16),
        scratch_shapes=[pltpu.VMEM((tm, 1), F32), pltpu.VMEM((2 * NORM_ROWS, d), F32)],
        compiler_params=_params(("arbitrary",), 2 * tm * d * 10 + tm * LANES * 4 + 4 * SUBLANES * d * 4),
        name="rmsnorm_modulate",
    )(*rows, g.reshape(1, d), mod, mod)


def _final_norm_kernel(x_ref, g_ref, o_ref, rs_ref, *, tm):
    _row_rsqrt_stats(x_ref, rs_ref, tm)

    def scale(r, carry):
        r0 = pl.multiple_of(r * NORM_ROWS, NORM_ROWS)
        o_ref[pl.ds(r0, NORM_ROWS), :] = x_ref[pl.ds(r0, NORM_ROWS), :] * rs_ref[pl.ds(r0, NORM_ROWS), :] * g_ref[...]
        return carry

    lax.fori_loop(0, tm // NORM_ROWS, scale, 0, unroll=8)


def _final_norm(x, g):
    m, d = x.shape
    tm = _pick(m, ROW_TILE, NORM_ROWS)
    return pl.pallas_call(
        functools.partial(_final_norm_kernel, tm=tm),
        grid=(m // tm,),
        in_specs=[pl.BlockSpec((tm, d), lambda i: (i, 0)), pl.BlockSpec((NORM_ROWS, d), lambda i: (0, 0))],
        out_specs=pl.BlockSpec((tm, d), lambda i: (i, 0)),
        out_shape=jax.ShapeDtypeStruct((m, d), F32),
        scratch_shapes=[pltpu.VMEM((tm, 1), F32)],
        compiler_params=_params(("arbitrary",), 2 * tm * d * 8 + tm * LANES * 4),
        name="final_rmsnorm",
    )(x, jnp.broadcast_to(g.reshape(1, d), (NORM_ROWS, d)))


def _ws_mm_kernel(a_ref, w_ref, o_ref, wb):
    @pl.when(pl.program_id(1) == 0)
    def _():
        wb[...] = w_ref[...].astype(BF16)

    o_ref[...] = jnp.dot(a_ref[...], wb[...], preferred_element_type=F32).astype(o_ref.dtype)


def _ws_matmul(a, w, layer, col0, n, out_dtype):
    m, k = a.shape
    tm = _pick(m, WS_ROWS, LANES)
    tn = _pick(math.gcd(n, col0) if col0 else n, WS_COLS, LANES)
    jb = col0 // tn
    osz = jnp.dtype(out_dtype).itemsize
    need = 2 * (tm * k * 2 + k * tn * 4 + tm * tn * osz) + k * tn * 2 + tm * tn * 4
    return pl.pallas_call(
        _ws_mm_kernel,
        grid=(n // tn, m // tm),
        in_specs=[pl.BlockSpec((tm, k), lambda j, i: (i, 0)),
                  pl.BlockSpec((None, k, tn), lambda j, i: (layer, 0, jb + j))],
        out_specs=pl.BlockSpec((tm, tn), lambda j, i: (i, j)),
        out_shape=jax.ShapeDtypeStruct((m, n), out_dtype),
        scratch_shapes=[pltpu.VMEM((k, tn), BF16)],
        compiler_params=_params(("arbitrary", "arbitrary"), need),
        name="input_projection",
    )(a, w)


def _ws_resid_kernel(a_ref, w_ref, x_ref, gate_ref, o_ref, wb, *, t_lat, tm):
    @pl.when(pl.program_id(1) == 0)
    def _():
        wb[...] = w_ref[...].astype(BF16)

    rows = pl.program_id(1) * tm + lax.broadcasted_iota(jnp.int32, (tm, 1), 0)
    gate = _row_select(rows >= t_lat, gate_ref)
    o_ref[...] = x_ref[...] + gate * jnp.dot(a_ref[...], wb[...], preferred_element_type=F32)


def _ws_resid2_kernel(a_ref, w_ref, x_ref, c_ref, gate_ref, o_ref, wb, *, lat_rows_last):
    @pl.when(pl.program_id(1) == 0)
    def _():
        wb[...] = w_ref[...].astype(BF16)

    y = jnp.dot(a_ref[...], wb[...], preferred_element_type=F32)
    last = pl.num_programs(1) - 1

    @pl.when(pl.program_id(1) < last)
    def _():
        o_ref[...] = x_ref[...] + gate_ref[0:1, :] * y

    @pl.when(pl.program_id(1) == last)
    def _():
        k = lat_rows_last
        o_ref[0:k, :] = x_ref[0:k, :] + gate_ref[0:1, :] * y[0:k]
        o_ref[k:, :] = c_ref[...] + gate_ref[1:2, :] * y[k:]


def _ws_matmul_residual(a, w, layer, x, mod, gate_blk, t_lat, m, ctx=None):
    k = a.shape[1]
    n = w.shape[2]
    tm = _pick(m, min(WS_ROWS, t_lat), LANES)
    tn = _pick(n, WS_COLS, LANES)
    nb = n // tn
    need = 2 * (tm * k * 2 + k * tn * 4 + 2 * tm * tn * 4) + k * tn * 2 + tm * tn * 4
    specs = [
        pl.BlockSpec((tm, k), lambda j, i: (i, 0)),
        pl.BlockSpec((None, k, tn), lambda j, i: (layer, 0, j)),
        pl.BlockSpec((tm, tn), lambda j, i: (i, j)),
    ]
    if ctx is None:
        kern = functools.partial(_ws_resid_kernel, t_lat=t_lat, tm=tm)
        rows = (x,)
    else:
        n_ctx = ctx.shape[0]
        assert m == t_lat + n_ctx and n_ctx <= tm and (tm - n_ctx) % SUBLANES == 0
        kern = functools.partial(_ws_resid2_kernel, lat_rows_last=tm - n_ctx)
        specs.append(pl.BlockSpec((n_ctx, tn), lambda j, i: (0, j)))
        rows = (x, ctx)
        need += 2 * n_ctx * tn * 4
    specs.append(pl.BlockSpec((None, SUBLANES, tn), lambda j, i: (layer, 0, gate_blk * nb + j)))
    return pl.pallas_call(
        kern,
        grid=(nb, m // tm),
        in_specs=specs,
        out_specs=pl.BlockSpec((tm, tn), lambda j, i: (i, j)),
        out_shape=jax.ShapeDtypeStruct((m, n), F32),
        scratch_shapes=[pltpu.VMEM((k, tn), BF16)],
        compiler_params=_params(("arbitrary", "arbitrary"), need),
        name="output_projection",
    )(a, w, *rows, mod)


def _ws_swiglu_kernel(a_ref, w1_ref, w3_ref, o_ref, w1b, w3b):
    @pl.when(pl.program_id(1) == 0)
    def _():
        w1b[...] = w1_ref[...].astype(BF16)
        w3b[...] = w3_ref[...].astype(BF16)

    a = a_ref[...]
    u = jnp.dot(a, w1b[...], preferred_element_type=F32)
    g = jnp.dot(a, w3b[...], preferred_element_type=F32)
    o_ref[...] = (_silu(u) * g).astype(o_ref.dtype)


def _ws_swiglu_up(a, w1, w3, layer, m):
    k = a.shape[1]
    n = w1.shape[2]
    tm = _pick(m, WS_ROWS, LANES)
    tn = _pick(n, WS_PAIR_COLS, LANES)
    need = 2 * (tm * k * 2 + 2 * k * tn * 4 + tm * tn * 2) + 2 * k * tn * 2 + 3 * tm * tn * 4
    w_spec = pl.BlockSpec((None, k, tn), lambda j, i: (layer, 0, j))
    return pl.pallas_call(
        _ws_swiglu_kernel,
        grid=(n // tn, m // tm),
        in_specs=[pl.BlockSpec((tm, k), lambda j, i: (i, 0)), w_spec, w_spec],
        out_specs=pl.BlockSpec((tm, tn), lambda j, i: (i, j)),
        out_shape=jax.ShapeDtypeStruct((m, n), BF16),
        scratch_shapes=[pltpu.VMEM((k, tn), BF16), pltpu.VMEM((k, tn), BF16)],
        compiler_params=_params(("arbitrary", "arbitrary"), need),
        name="swiglu_up",
    )(a, w1, w3)


def _mm_resid_kernel(a_ref, w_ref, x_ref, gate_ref, o_ref, *, t_lat, tm):
    rows = pl.program_id(0) * tm + lax.broadcasted_iota(jnp.int32, (tm, 1), 0)
    gate = _row_select(rows >= t_lat, gate_ref)
    o_ref[...] = x_ref[...] + gate * jnp.dot(a_ref[...], w_ref[...], preferred_element_type=F32)


def _matmul_residual(a, w, layer, x, mod, gate_blk, t_lat, m):
    k = a.shape[1]
    n = w.shape[2]
    tm = _pick(m, AS_ROWS, MXU_COLS)
    tn = _pick(n, MXU_COLS if tm > AS_ROWS_WIDE else AS_COLS_WIDE, LANES)
    need = 2 * (tm * k * 2 + k * tn * 2 + 2 * tm * tn * 4) + tm * tn * 4
    nb = n // tn
    return pl.pallas_call(
        functools.partial(_mm_resid_kernel, t_lat=t_lat, tm=tm),
        grid=(m // tm, nb),
        in_specs=[
            pl.BlockSpec((tm, k), lambda i, j: (i, 0)),
            pl.BlockSpec((None, k, tn), lambda i, j: (layer, 0, j)),
            pl.BlockSpec((tm, tn), lambda i, j: (i, j)),
            pl.BlockSpec((None, SUBLANES, tn), lambda i, j: (layer, 0, gate_blk * nb + j)),
        ],
        out_specs=pl.BlockSpec((tm, tn), lambda i, j: (i, j)),
        out_shape=jax.ShapeDtypeStruct((m, n), F32),
        compiler_params=_params(("arbitrary", "arbitrary"), need),
        name="ffn_down",
    )(a, w, x, mod)


def _merge_kernel(h_ref, ya_ref, yb_ref, yc_ref, yd_ref, wg0, wg1, wg2, wg3, bg0, bg1, bg2, bg3, wb_ref, o_ref):
    h = h_ref[...]
    acc = None
    for i, (y_ref, wg, bg) in enumerate(((ya_ref, wg0, bg0), (yb_ref, wg1, bg1), (yc_ref, wg2, bg2), (yd_ref, wg3, bg3))):
        gate = _sigmoid(jnp.dot(h, wg[...], preferred_element_type=F32) + bg[...])
        term = gate * jnp.dot(y_ref[...], wb_ref[i], preferred_element_type=F32)
        acc = term if acc is None else acc + term
    o_ref[...] = acc.astype(o_ref.dtype)


def _gated_merge(h, ys, w_gate, b_gate, w_branch, layer, m):
    k = h.shape[1]
    depth, nbr, bw, n = w_branch.shape
    tm = _pick(m, AS_ROWS, MXU_COLS)
    tn = _pick(n, MXU_COLS, LANES)
    nb = n // tn
    need = 2 * (tm * k * 2 + nbr * tm * bw * 2 + nbr * k * tn * 2 + nbr * bw * tn * 2 + tm * tn * 2) + 4 * tm * tn * 4
    y_spec = pl.BlockSpec((tm, bw), lambda i, j: (i, 0))
    wg_specs = [pl.BlockSpec((None, k, tn), functools.partial(lambda i, j, b: (layer, 0, b * nb + j), b=b))
                for b in range(nbr)]
    bg_specs = [pl.BlockSpec((None, 1, tn), functools.partial(lambda i, j, b: (layer, 0, b * nb + j), b=b))
                for b in range(nbr)]
    bg = b_gate.reshape(depth, 1, -1)
    return pl.pallas_call(
        _merge_kernel,
        grid=(m // tm, nb),
        in_specs=[pl.BlockSpec((tm, k), lambda i, j: (i, 0)), y_spec, y_spec, y_spec, y_spec]
        + wg_specs + bg_specs + [pl.BlockSpec((None, nbr, bw, tn), lambda i, j: (layer, 0, 0, j))],
        out_specs=pl.BlockSpec((tm, tn), lambda i, j: (i, j)),
        out_shape=jax.ShapeDtypeStruct((m, n), BF16),
        compiler_params=_params(("arbitrary", "arbitrary"), need),
        name="gated_merge",
    )(h, *ys, w_gate, w_gate, w_gate, w_gate, bg, bg, bg, bg, w_branch)


def _mix_a_kernel(pu_ref, pv_ref, ws_ref, bias_ref, ng_ref, nb_ref, o_ref, *, nh):
    zu = _gelu(pu_ref[...].astype(F32))
    zv = _gelu(pv_ref[...].astype(F32))
    mu = jnp.mean(zv, axis=-1, keepdims=True)
    zc = zv - mu
    var = jnp.mean(zc * zc, axis=-1, keepdims=True)
    v = (zc * lax.rsqrt(var + EPS) * ng_ref[...] + nb_ref[...]).astype(BF16)
    for g in range(nh):
        sl = slice(g * HEAD_DIM, (g + 1) * HEAD_DIM)
        s = jnp.dot(ws_ref[g], v[:, sl], preferred_element_type=F32) + bias_ref[:, sl]
        o_ref[:, sl] = (zu[:, sl] * s).astype(o_ref.dtype)


def _mix_a(p, ws, bs, ng, nb, bw):
    m = p.shape[0]
    nh = bw // HEAD_DIM
    bias = jnp.repeat(bs.T, HEAD_DIM, axis=1)
    return pl.pallas_call(
        functools.partial(_mix_a_kernel, nh=nh),
        grid=(m // CHUNK,),
        in_specs=[
            pl.BlockSpec((CHUNK, bw), lambda i: (i, 0)),
            pl.BlockSpec((CHUNK, bw), lambda i: (i, 1)),
            pl.BlockSpec((nh, CHUNK, CHUNK), lambda i: (0, 0, 0)),
            pl.BlockSpec((CHUNK, bw), lambda i: (0, 0)),
            pl.BlockSpec((1, bw), lambda i: (0, 0)),
            pl.BlockSpec((1, bw), lambda i: (0, 0)),
        ],
        out_specs=pl.BlockSpec((CHUNK, bw), lambda i: (i, 0)),
        out_shape=jax.ShapeDtypeStruct((m, bw), BF16),
        compiler_params=_params(("arbitrary",), 16 * CHUNK * bw * 4),
        name="mixer_gmlp",
    )(p, p, ws.astype(BF16), bias, ng.reshape(1, bw), nb.reshape(1, bw))


CONV_ROWS = 32


def _mix_b_kernel(ap_ref, ac_ref, an_ref, gp_ref, gc_ref, gn_ref, cw_ref, cb_ref, ng_ref, nb_ref, o_ref, ysh, yconv,
                  *, tiles_lat, tiles_tot, tmb):
    i = pl.program_id(0)
    first = jnp.logical_or(i == 0, i == tiles_lat)
    last = jnp.logical_or(i == tiles_lat - 1, i == tiles_tot - 1)
    glu = lambda a_ref, g_ref: a_ref[...].astype(F32) * _sigmoid(g_ref[...].astype(F32))
    yp = glu(ap_ref, gp_ref)
    yn = glu(an_ref, gn_ref)
    ysh[0, 0:CONV_HALO, :] = jnp.where(first, 0.0, yp)
    ysh[0, CONV_HALO:CONV_HALO + tmb, :] = glu(ac_ref, gc_ref)
    ysh[0, CONV_HALO + tmb:, :] = jnp.where(last, 0.0, yn)
    nrow = tmb + 2 * CONV_HALO
    for c0 in range(0, ysh.shape[2], LANES):
        base = ysh[0, :, c0:c0 + LANES]
        for b in range(1, SUBLANES):
            ysh[b, :, c0:c0 + LANES] = pltpu.roll(base, nrow - b, 0)
    off = CONV_HALO - CONV_K // 2

    bw = ysh.shape[2]
    groups = CONV_ROWS // SUBLANES

    def conv_block(rb, carry):
        r0 = pl.multiple_of(rb * CONV_ROWS, CONV_ROWS)
        acc = [None] * groups
        for j in range(CONV_K):
            b = (off + j) % SUBLANES
            wj = cw_ref[j * SUBLANES:(j + 1) * SUBLANES, :]
            for g in range(groups):
                term = ysh[b, pl.ds(r0 + (off + j - b + g * SUBLANES), SUBLANES), :] * wj
                acc[g] = term if acc[g] is None else acc[g] + term
        for g in range(groups):
            yconv[pl.ds(r0 + g * SUBLANES, SUBLANES), :] = acc[g]
        return carry

    lax.fori_loop(0, tmb // CONV_ROWS, conv_block, 0)

    def norm_group(rg, carry):
        r0 = pl.multiple_of(rg * SUBLANES, SUBLANES)
        y = yconv[pl.ds(r0, SUBLANES), :] + cb_ref[...]
        mu = jnp.mean(y, axis=-1, keepdims=True)
        yc = y - mu
        var = jnp.mean(yc * yc, axis=-1, keepdims=True)
        z = yc * lax.rsqrt(var + EPS) * ng_ref[...] + nb_ref[...]
        o_ref[pl.ds(r0, SUBLANES), :] = _silu(z).astype(o_ref.dtype)
        return carry

    lax.fori_loop(0, tmb // SUBLANES, norm_group, 0, unroll=8)


def _mix_b(p, cw, cb, ng, nb, bw, t_lat):
    m = p.shape[0]
    tmb = _pick(math.gcd(t_lat, m - t_lat), CONV_TILE, CONV_ROWS)
    hb = tmb // CONV_HALO
    nhalo = m // CONV_HALO
    a_col, g_col = 2, 3
    prev = lambda c: (lambda i: (jnp.maximum(i * hb - 1, 0), c))
    cur = lambda c: (lambda i: (i, c))
    nxt = lambda c: (lambda i: (jnp.minimum((i + 1) * hb, nhalo - 1), c))
    vec = pl.BlockSpec((SUBLANES, bw), lambda i: (0, 0))
    rep = lambda a: jnp.broadcast_to(a.reshape(-1, 1, bw), (a.size // bw, SUBLANES, bw)).reshape(-1, bw)
    return pl.pallas_call(
        functools.partial(_mix_b_kernel, tiles_lat=t_lat // tmb, tiles_tot=m // tmb, tmb=tmb),
        grid=(m // tmb,),
        in_specs=[
            pl.BlockSpec((CONV_HALO, bw), prev(a_col)),
            pl.BlockSpec((tmb, bw), cur(a_col)),
            pl.BlockSpec((CONV_HALO, bw), nxt(a_col)),
            pl.BlockSpec((CONV_HALO, bw), prev(g_col)),
            pl.BlockSpec((tmb, bw), cur(g_col)),
            pl.BlockSpec((CONV_HALO, bw), nxt(g_col)),
            pl.BlockSpec((CONV_K * SUBLANES, bw), lambda i: (0, 0)),
            vec, vec, vec,
        ],
        out_specs=pl.BlockSpec((tmb, bw), lambda i: (i, 0)),
        out_shape=jax.ShapeDtypeStruct((m, bw), BF16),
        scratch_shapes=[pltpu.VMEM((SUBLANES, tmb + 2 * CONV_HALO, bw), F32), pltpu.VMEM((tmb, bw), F32)],
        compiler_params=_params(("arbitrary",), 20 * tmb * bw * 4),
        name="mixer_conformer_conv",
    )(p, p, p, p, p, p, rep(cw), rep(cb), rep(ng), rep(nb))


def _rope_tables(t_lat):
    m = HEAD_DIM // 4
    freqs = np.power(np.float32(ROPE_THETA), -np.arange(m, dtype=np.float32) / np.float32(m)).astype(np.float32)
    r = t_lat // GRID_W
    ang_r = np.repeat(np.arange(r, dtype=np.float32)[:, None] * freqs[None, :], GRID_W, axis=0)
    ang_c = np.tile(np.arange(GRID_W, dtype=np.float32)[:, None] * freqs[None, :], (r, 1))
    cos = np.concatenate([np.cos(ang_r), np.cos(ang_r), np.cos(ang_c), np.cos(ang_c)], axis=-1)
    sin = np.concatenate([-np.sin(ang_r), np.sin(ang_r), -np.sin(ang_c), np.sin(ang_c)], axis=-1)
    return jnp.asarray(cos, F32), jnp.asarray(sin, F32)


def _rope(x, cos, sin):
    lane = lax.broadcasted_iota(jnp.int32, x.shape, 1)
    low = (lane % (HEAD_DIM // 2)) < (HEAD_DIM // 4)
    swapped = jnp.where(low, pltpu.roll(x, HEAD_DIM - HEAD_DIM // 4, 1), pltpu.roll(x, HEAD_DIM // 4, 1))
    return x * cos + swapped * sin


ATTN_QROWS = 4
ATTN_KROWS = ATTN_QROWS + WIN_H
ATTN_SHIFT = WIN_H // 2
ATTN_VARIANTS = WIN_H // ATTN_QROWS + 1


def _bias_kernel(rpb_ref, o_ref, *, ndr, ndc):
    base = pl.program_id(0) * (ndr * ndc)
    q = lax.broadcasted_iota(jnp.int32, (GRID_W, GRID_W), 0)
    k = lax.broadcasted_iota(jnp.int32, (GRID_W, GRID_W), 1)
    dc = jnp.clip(k - q, -(WIN_W - 1), WIN_W - 1) + (WIN_W - 1)
    start = jnp.clip(q - WIN_W // 2, 0, GRID_W - WIN_W)
    in_win = jnp.logical_and(k >= start, k < start + WIN_W)
    masked = jnp.full((GRID_W, GRID_W), NEG_BIG, F32)
    by_row_offset = []
    for dr in range(ndr):
        val = jnp.zeros((GRID_W, GRID_W), F32)
        for d in range(ndc):
            val = jnp.where(dc == d, rpb_ref[base + dr * ndc + d], val)
        by_row_offset.append(jnp.where(in_win, val * LOG2_E, NEG_BIG))
    for v in range(ATTN_VARIANTS):
        block_off = v * ATTN_QROWS
        for qr in range(ATTN_QROWS):
            first = min(max(block_off + qr - ATTN_SHIFT, 0), ATTN_KROWS - WIN_H)
            cols = [by_row_offset[j - qr - block_off + WIN_H - 1] if first <= j < first + WIN_H else masked
                    for j in range(ATTN_KROWS)]
            for j in range(0, ATTN_KROWS, 2):
                o_ref[v, qr * GRID_W:(qr + 1) * GRID_W, j * GRID_W:(j + 2) * GRID_W] = jnp.concatenate(
                    cols[j:j + 2], axis=-1)


def _bias_table(rpb):
    depth, nh, ndr, ndc = rpb.shape
    nq, nk = ATTN_QROWS * GRID_W, ATTN_KROWS * GRID_W
    assert 2 * GRID_W == LANES and ATTN_KROWS % 2 == 0
    tab = pl.pallas_call(
        functools.partial(_bias_kernel, ndr=ndr, ndc=ndc),
        grid=(depth * nh,),
        in_specs=[pl.BlockSpec(memory_space=pltpu.SMEM)],
        out_specs=pl.BlockSpec((None, ATTN_VARIANTS, nq, nk), lambda i: (i, 0, 0, 0)),
        out_shape=jax.ShapeDtypeStruct((depth * nh, ATTN_VARIANTS, nq, nk), F32),
        compiler_params=_params(("arbitrary",), 4 * ATTN_VARIANTS * nq * nk * 4),
        name="attention_bias_table",
    )(rpb.reshape(-1))
    return tab.reshape(depth, nh, ATTN_VARIANTS, nq, nk)


ATTN_UNROLL = 4
ATTN_PREP_ROWS = 256


def _softmax_pv(scores, values):
    mx = None
    for s in scores:
        smax = jnp.max(s, axis=-1, keepdims=True)
        mx = smax if mx is None else jnp.maximum(mx, smax)
    den = None
    acc = None
    for s, v in zip(scores, values):
        e = jnp.exp2(s - mx)
        esum = jnp.sum(e, axis=-1, keepdims=True)
        den = esum if den is None else den + esum
        pv = jnp.dot(e.astype(BF16), v, preferred_element_type=F32)
        acc = pv if acc is None else acc + pv
    return acc / den


def _dot_nt(a, b):
    return lax.dot_general(a, b, (((1,), (1,)), ((), ())), preferred_element_type=F32)


def _attn_kernel(q_ref, k_ref, v_ref, qc_ref, kc_ref, vc_ref, cos_ref, sin_ref, bias_ref, o_ref, kr_ref,
                 *, rows, t_lat, with_ctx):
    scale = HEAD_DIM ** -0.5 * LOG2_E
    kc = kc_ref[...]
    vc = vc_ref[...]
    nq = ATTN_QROWS * GRID_W
    nk = ATTN_KROWS * GRID_W

    def prep(c, carry):
        rs = pl.ds(pl.multiple_of(c * ATTN_PREP_ROWS, ATTN_PREP_ROWS), ATTN_PREP_ROWS)
        kr_ref[rs, :] = _rope(k_ref[rs, :].astype(F32), cos_ref[rs, :], sin_ref[rs, :]).astype(BF16)
        return carry

    lax.fori_loop(0, t_lat // ATTN_PREP_ROWS, prep, 0, unroll=2)

    def body(blk, carry):
        r0 = blk * ATTN_QROWS
        s0 = jnp.clip(r0 - ATTN_SHIFT, 0, rows - ATTN_KROWS)
        qs = pl.ds(pl.multiple_of(r0 * GRID_W, nq), nq)
        ks = pl.ds(pl.multiple_of(s0 * GRID_W, GRID_W), nk)
        q = q_ref[qs, :]
        q_rot = _rope(q.astype(F32), cos_ref[qs, :], sin_ref[qs, :]).astype(BF16)
        s_lat = _dot_nt(q_rot, kr_ref[ks, :]) * scale
        b = bias_ref[(r0 - s0) // ATTN_QROWS]
        s_lat = jnp.where(b > 0.5 * NEG_BIG, s_lat + b, NEG_BIG)
        s_ctx = _dot_nt(q, kc) * scale
        o = _softmax_pv((s_lat, s_ctx), (v_ref[ks, :], vc))
        o_ref[qs, :] = o.astype(o_ref.dtype)
        return carry

    lax.fori_loop(0, rows // ATTN_QROWS, body, 0, unroll=ATTN_UNROLL)
    if with_ctx:
        s = _dot_nt(qc_ref[...], kc) * scale
        o_ref[t_lat:, :] = _softmax_pv((s,), (vc,)).astype(o_ref.dtype)


def _attention(p, cos, sin, bias, layer, bw, t_lat, with_ctx):
    m = p.shape[0]
    n_ctx = m - t_lat
    nh = bw // HEAD_DIM
    rows = t_lat // GRID_W
    assert rows >= ATTN_KROWS and rows % ATTN_QROWS == 0 and t_lat % n_ctx == 0 and t_lat % ATTN_PREP_ROWS == 0
    m_out = m if with_ctx else t_lat
    q_col, k_col, v_col = 4 * nh, 5 * nh, 6 * nh
    lat = lambda c: pl.BlockSpec((t_lat, HEAD_DIM), lambda h: (0, c + h))
    ctx = lambda c: pl.BlockSpec((n_ctx, HEAD_DIM), lambda h: (t_lat // n_ctx, c + h))
    table = pl.BlockSpec((t_lat, HEAD_DIM), lambda h: (0, 0), pipeline_mode=pl.Buffered(1))
    nq, nk = ATTN_QROWS * GRID_W, ATTN_KROWS * GRID_W
    need = (2 * 3 * t_lat * HEAD_DIM * 2 + 2 * t_lat * HEAD_DIM * 4 + t_lat * HEAD_DIM * 2
            + 2 * m_out * HEAD_DIM * 2 + 2 * ATTN_VARIANTS * nq * nk * 4 + 8 * nq * (nk + n_ctx) * 4)
    return pl.pallas_call(
        functools.partial(_attn_kernel, rows=rows, t_lat=t_lat, with_ctx=with_ctx),
        grid=(nh,),
        in_specs=[lat(q_col), lat(k_col), lat(v_col), ctx(q_col), ctx(k_col), ctx(v_col), table, table,
                  pl.BlockSpec((None, None, ATTN_VARIANTS, nq, nk), lambda h: (layer, h, 0, 0, 0))],
        out_specs=pl.BlockSpec((m_out, HEAD_DIM), lambda h: (0, h)),
        out_shape=jax.ShapeDtypeStruct((m_out, bw), BF16),
        scratch_shapes=[pltpu.VMEM((t_lat, HEAD_DIM), BF16)],
        compiler_params=_params(("arbitrary",), need),
        name="neighbourhood_attention",
    )(p, p, p, p, p, p, cos, sin, bias)


def _scan_kernel(*refs, nh, reverse, readout):
    if readout:
        qd_ref, z_ref, i_ref, lb_ref, g_ref, other_ref, ng_ref, o_ref, st_ref = refs
    else:
        qd_ref, z_ref, i_ref, lb_ref, o_ref, st_ref = refs
    c = CHUNK

    @pl.when(pl.program_id(0) == 0)
    def _():
        st_ref[...] = jnp.zeros_like(st_ref)

    ri = lax.broadcasted_iota(jnp.int32, (c, 1), 0)
    ti = lax.broadcasted_iota(jnp.int32, (c, c), 0)
    si = lax.broadcasted_iota(jnp.int32, (c, c), 1)
    shift = SCAN_DIRECT.bit_length() - 1
    near = jnp.logical_and((ti >> shift) == (si >> shift), (si >= ti) if reverse else (si <= ti))
    widths = [1 << k for k in range(c.bit_length() - 1)]
    masks = [jnp.logical_and(((ti ^ si) >> (w.bit_length() - 1)) == 1,
                             ((ti & w) == 0) if reverse else ((ti & w) != 0)) for w in widths]

    for h in range(nh):
        sl = slice(h * HEAD_DIM, (h + 1) * HEAD_DIM)
        lb = lb_ref[:, sl]
        f = lb + (1.0 - lb) * _sigmoid(z_ref[:, sl])
        l = jnp.log(jnp.maximum(f, F_MIN)) * LOG2_E
        kg = 1.0 - f
        q = _silu(qd_ref[:, sl])
        v = i_ref[:, sl].astype(BF16)

        cf, tot = l, l
        att = None
        for w, mk in zip(widths, masks):
            if w == SCAN_DIRECT:
                e = (cf - l) if reverse else cf
                q_near = q * jnp.exp2(-e if reverse else e)
                k_near = kg * jnp.exp2(e if reverse else -e)
                att = jnp.where(near, _dot_nt(q_near.astype(BF16), k_near.astype(BF16)), 0.0)
            if w < SUBLANES:
                odd = (ri & w) != 0
                if w < SCAN_DIRECT:
                    x = None
                elif reverse:
                    x = jnp.where(odd, kg, q) * jnp.exp2(jnp.where(odd, cf - l, tot - cf + l))
                else:
                    x = jnp.where(odd, q, kg) * jnp.exp2(jnp.where(odd, cf, tot - cf))
                prev_tot = pltpu.roll(tot, w, 0)
                next_tot = pltpu.roll(tot, c - w, 0)
                cf = cf + jnp.where(odd, prev_tot, 0.0)
                tot = tot + jnp.where(odd, prev_tot, next_tot)
            else:
                xs, cfs, tots = [], [], []
                for k in range(c // w):
                    rows = slice(k * w, (k + 1) * w)
                    other = slice((k ^ 1) * w, ((k ^ 1) + 1) * w)
                    is_query = (k % 2 == 0) == reverse
                    if reverse:
                        arg = (tot[rows] - cf[rows] + l[rows]) if is_query else (cf[rows] - l[rows])
                    else:
                        arg = cf[rows] if is_query else (tot[rows] - cf[rows])
                    xs.append((q[rows] if is_query else kg[rows]) * jnp.exp2(arg))
                    cfs.append(cf[rows] + tot[other] if k % 2 else cf[rows])
                    tots.append(tot[rows] + tot[other])
                x = jnp.concatenate(xs, axis=0)
                cf = jnp.concatenate(cfs, axis=0)
                tot = jnp.concatenate(tots, axis=0)
            if x is not None:
                xb = x.astype(BF16)
                att = jnp.where(mk, _dot_nt(xb, xb), att)
        to_state = (tot - cf + l) if reverse else cf
        to_end = (cf - l) if reverse else (tot - cf)
        q_in = (q * jnp.exp2(to_state)).astype(BF16)
        k_out = (kg * jnp.exp2(to_end)).astype(BF16)

        st = st_ref[h]
        o = _dot_nt(q_in, st.astype(BF16)) + jnp.dot(att.astype(BF16), v, preferred_element_type=F32)
        st_ref[h] = st * jnp.exp2(tot[0:1, :]) + lax.dot_general(
            v, k_out, (((0,), (0,)), ((), ())), preferred_element_type=F32)
        if readout:
            o = o + other_ref[:, sl]
            o = o * lax.rsqrt(jnp.mean(o * o, axis=-1, keepdims=True) + EPS) * ng_ref[...]
            o_ref[:, sl] = (o * _silu(g_ref[:, sl])).astype(o_ref.dtype)
        else:
            o_ref[:, sl] = o


def _scan(p, lb, bw, t_lat, reverse, other=None, norm_g=None):
    m = p.shape[0]
    nh = bw // HEAD_DIM
    n_lat = t_lat // CHUNK
    n_all = m // CHUNK
    n_ctx = n_all - n_lat
    readout = other is not None
    if reverse:
        row_blk = lambda j: jnp.where(j < n_ctx, n_all - 1 - j, n_lat - 1 - (j - n_ctx))
    else:
        row_blk = lambda j: jnp.where(j < n_ctx, n_lat + j, j - n_ctx)
    blk = lambda c: pl.BlockSpec((CHUNK, bw), lambda j: (row_blk(j), c))
    qd_col, zf_col, zb_col, i_col, g_col = 0, 1, 2, 3, 4
    in_specs = [blk(qd_col), blk(zb_col if reverse else zf_col), blk(i_col), pl.BlockSpec((1, bw), lambda j: (0, 0))]
    args = [p, p, p, lb.reshape(1, bw)]
    if readout:
        in_specs += [blk(g_col), blk(0), pl.BlockSpec((1, HEAD_DIM), lambda j: (0, 0))]
        args += [p, other, norm_g.reshape(1, HEAD_DIM)]
    return pl.pallas_call(
        functools.partial(_scan_kernel, nh=nh, reverse=reverse, readout=readout),
        grid=(n_all,),
        in_specs=in_specs,
        out_specs=pl.BlockSpec((CHUNK, bw), lambda j: (row_blk(j), 0)),
        out_shape=jax.ShapeDtypeStruct((m, bw), BF16 if readout else F32),
        scratch_shapes=[pltpu.VMEM((nh, HEAD_DIM, HEAD_DIM), F32)],
        compiler_params=_params(("arbitrary",), 48 * CHUNK * bw * 4),
        name="hgrn2_scan_bwd" if reverse else "hgrn2_scan_fwd",
    )(*args)


def kernel(x, c, ctx, c_ctx, w_ada, b_ada, norm1_g, norm2_g, w_in, a_norm_g, a_norm_b, a_ws, a_bs, b_conv_w,
           b_conv_b, b_norm_g, b_norm_b, c_rpb, d_lb_logits, d_norm_g, w_branch, w_gate, b_gate, w_out, w_ffn1,
           w_ffn3, w_ffn2, final_g):
    batch, t_lat, d = x.shape
    n_ctx = ctx.shape[1]
    depth = w_ada.shape[0]
    bw = d // N_BRANCH
    assert batch == 1 and t_lat % GRID_W == 0 and t_lat % CHUNK == 0 and n_ctx % CHUNK == 0

    xa, xc = x[0], ctx[0]
    cond = jnp.zeros((SUBLANES, d), F32).at[0].set(c[0]).at[1].set(c_ctx)
    mod = _modulation(cond, w_ada, b_ada)

    prob = jax.nn.softmax(d_lb_logits.astype(F32), axis=1)
    lbound = jnp.cumsum(prob, axis=1) - prob[:, :1]

    cos, sin = _rope_tables(t_lat)
    bias = _bias_table(c_rpb)
    w_gate_b, w_branch_b, w2_b = (w.astype(BF16) for w in (w_gate, w_branch, w_ffn2))

    for l in range(depth):
        last = l == depth - 1
        m = t_lat if last else t_lat + n_ctx
        h = _norm_mod(xa, norm1_g[l], mod, l, 0, 1, t_lat, ctx=xc)
        p = _ws_matmul(h, w_in, l, 0, 7 * bw, BF16)
        pd = _ws_matmul(h, w_in, l, 7 * bw, 5 * bw, F32)
        ya = _mix_a(p, a_ws[l], a_bs[l], a_norm_g[l], a_norm_b[l], bw)
        yb = _mix_b(p, b_conv_w[l], b_conv_b[l], b_norm_g[l], b_norm_b[l], bw, t_lat)
        yc = _attention(p, cos, sin, bias, l, bw, t_lat, with_ctx=not last)
        o_bwd = _scan(pd, lbound[1, l], bw, t_lat, reverse=True)
        yd = _scan(pd, lbound[0, l], bw, t_lat, reverse=False, other=o_bwd, norm_g=d_norm_g[l])
        merged = _gated_merge(h, (ya, yb, yc, yd), w_gate_b, b_gate, w_branch_b, l, m)
        x1 = _ws_matmul_residual(merged, w_out, l, xa, mod, 2, t_lat, m, ctx=None if last else xc)
        hf = _norm_mod(x1, norm2_g[l], mod, l, 3, 4, t_lat)
        act = _ws_swiglu_up(hf, w_ffn1, w_ffn3, l, m)
        xa, xc = _matmul_residual(act, w2_b, l, x1, mod, 5, t_lat, m), None

    return _final_norm(xa[:t_lat], final_g)[None]
```
